```python
import jax, jax.numpy as jnp
from jax import lax
import numpy as np

D_MODEL = 1024
BATCH = 2
SEQ = 8192
DEPTH = 4

RET_HEADS = 4
RET_DK = 64
RET_DV = 128
FOX_HEADS = 4
FOX_DH = 64
MEM_HEADS = 4
MEM_DH = 64
MEM_LEN = 256
D_MIX = RET_HEADS * RET_DV + FOX_HEADS * FOX_DH + MEM_HEADS * MEM_DH
RET_CHUNK = 128
Q_BLOCK = 128
D_FF = 2752
N_EXPERTS = 8
TOP_K = 2
ROPE_BASE = 10000.0
EPS = 1e-6
N_DENSE = (DEPTH + 1) // 2
N_MOE = DEPTH // 2
IN_SPLIT = (RET_HEADS * RET_DK, RET_HEADS * RET_DK, RET_HEADS * RET_DV, RET_HEADS * RET_DV,
            FOX_HEADS * FOX_DH, FOX_HEADS * FOX_DH, FOX_HEADS * FOX_DH, FOX_HEADS,
            MEM_HEADS * MEM_DH)
IN_COLS = sum(IN_SPLIT)

kernel_name = "hybrid_retention_fox_memory_moe_trunk"


def rms_norm(x, g):
    xf = x.astype(jnp.float32)
    y = xf * lax.rsqrt(jnp.mean(xf * xf, axis=-1, keepdims=True) + EPS)
    return (y * g.astype(jnp.float32)).astype(x.dtype)


def rope(x, pos):
    half = x.shape[-1] // 2
    inv_freq = ROPE_BASE ** (-jnp.arange(half, dtype=jnp.float32) / half)
    ang = pos[:, None] * inv_freq[None, :]
    cos = jnp.cos(ang)[:, None, :]
    sin = jnp.sin(ang)[:, None, :]
    xf = x.astype(jnp.float32)
    x1, x2 = xf[..., :half], xf[..., half:]
    return jnp.concatenate([x1 * cos - x2 * sin, x2 * cos + x1 * sin], axis=-1).astype(x.dtype)


def retention(q, k, v):
    B, S, H, dk = q.shape
    dv = v.shape[-1]
    C = RET_CHUNK
    N = S // C
    log_g = jnp.log(1.0 - 2.0 ** (-5.0 - jnp.arange(H, dtype=jnp.float32)))

    def chunks(t):
        return t.astype(jnp.float32).reshape(B, N, C, H, t.shape[-1]).transpose(0, 3, 1, 2, 4)

    qc = chunks(q) * (dk ** -0.5)
    kc = chunks(k)
    vc = chunks(v)
    idx = jnp.arange(C, dtype=jnp.float32)
    diff = idx[:, None] - idx[None, :]
    dmat = jnp.where(diff >= 0, jnp.exp(log_g[:, None, None] * jnp.maximum(diff, 0.0)), 0.0)
    scores = jnp.einsum('bhnid,bhnjd->bhnij', qc, kc) * dmat[None, :, None]
    intra = jnp.einsum('bhnij,bhnje->bhnie', scores, vc)
    k_decay = jnp.exp(log_g[:, None] * (C - 1.0 - idx))
    kv = jnp.einsum('bhncd,bhnce->bhnde', kc * k_decay[None, :, None, :, None], vc)
    chunk_decay = jnp.exp(log_g * C)[None, :, None, None]

    def step(state, kv_n):
        return state * chunk_decay + kv_n, state

    _, prev = lax.scan(step, jnp.zeros((B, H, dk, dv), jnp.float32), jnp.moveaxis(kv, 2, 0))
    prev = jnp.moveaxis(prev, 0, 2)
    q_decay = jnp.exp(log_g[:, None] * (idx + 1.0))
    cross = jnp.einsum('bhncd,bhnde->bhnce', qc * q_decay[None, :, None, :, None], prev)
    out = intra + cross
    return out.transpose(0, 2, 3, 1, 4).reshape(B, S, H, dv)


def forgetting_attention(q, k, v, log_f):
    B, S, H, d = q.shape
    NB = S // Q_BLOCK
    scale = d ** -0.5
    c = lax.cumsum(log_f, axis=1).transpose(0, 2, 1)
    qh = q.transpose(0, 2, 1, 3)
    kh = k.transpose(0, 2, 1, 3)
    vh = v.transpose(0, 2, 1, 3)
    q_blocks = qh.reshape(B, H, NB, Q_BLOCK, d).transpose(2, 0, 1, 3, 4)
    c_blocks = c.reshape(B, H, NB, Q_BLOCK).transpose(2, 0, 1, 3)
    k_pos = jnp.arange(S)

    def block(args):
        qb, cb, i = args
        s = jnp.einsum('bhqd,bhkd->bhqk', qb, kh).astype(jnp.float32) * scale
        s = s + (cb[..., :, None] - c[:, :, None, :])
        q_pos = i * Q_BLOCK + jnp.arange(Q_BLOCK)
        s = jnp.where(k_pos[None, :] <= q_pos[:, None], s, -jnp.inf)
        p = jax.nn.softmax(s, axis=-1)
        return jnp.einsum('bhqk,bhkd->bhqd', p.astype(vh.dtype), vh)

    out = lax.map(block, (q_blocks, c_blocks, jnp.arange(NB)))
    return out.transpose(1, 0, 3, 2, 4).reshape(B, S, H, d)


def memory_attention(q, mk, mv):
    d = q.shape[-1]
    s = jnp.einsum('bshd,bmhd->bhsm', q, mk).astype(jnp.float32) * (d ** -0.5)
    p = jax.nn.softmax(s, axis=-1)
    return jnp.einsum('bhsm,bmhd->bshd', p.astype(mv.dtype), mv)


def hybrid_mixer(h, mem, w_in, forget_bias, fox_q_g, fox_k_g, mem_q_g, mem_k_g,
                 mem_norm_g, w_mem_kv, out_norm_g, w_out):
    B, S, _ = h.shape
    M = mem.shape[1]
    pos = jnp.arange(S, dtype=jnp.float32)
    proj = h @ w_in
    cuts = tuple(int(i) for i in np.cumsum(IN_SPLIT)[:-1])
    rq, rk, rv, rg, fq, fk, fv, ff, mq = jnp.split(proj, cuts, axis=-1)

    rq = rope(rq.reshape(B, S, RET_HEADS, RET_DK), pos)
    rk = rope(rk.reshape(B, S, RET_HEADS, RET_DK), pos)
    ret = retention(rq, rk, rv.reshape(B, S, RET_HEADS, RET_DV))

    fq = rms_norm(fq.reshape(B, S, FOX_HEADS, FOX_DH), fox_q_g)
    fk = rms_norm(fk.reshape(B, S, FOX_HEADS, FOX_DH), fox_k_g)
    log_f = jax.nn.log_sigmoid((ff + forget_bias).astype(jnp.float32))
    fox = forgetting_attention(fq, fk, fv.reshape(B, S, FOX_HEADS, FOX_DH), log_f)

    mkv = rms_norm(mem, mem_norm_g) @ w_mem_kv
    mk, mv = jnp.split(mkv, 2, axis=-1)
    mk = rms_norm(mk.reshape(B, M, MEM_HEADS, MEM_DH), mem_k_g)
    mv = mv.reshape(B, M, MEM_HEADS, MEM_DH)
    mq = rms_norm(mq.reshape(B, S, MEM_HEADS, MEM_DH), mem_q_g)
    memo = memory_attention(mq, mk, mv)

    g_ret, g_fox, g_mem = jnp.split(out_norm_g, (RET_HEADS * RET_DV, RET_HEADS * RET_DV + FOX_HEADS * FOX_DH))
    ret_o = rms_norm(ret.astype(h.dtype), g_ret.reshape(RET_HEADS, RET_DV)).reshape(B, S, -1) * jax.nn.silu(rg)
    fox_o = rms_norm(fox, g_fox.reshape(FOX_HEADS, FOX_DH)).reshape(B, S, -1)
    mem_o = rms_norm(memo, g_mem.reshape(MEM_HEADS, MEM_DH)).reshape(B, S, -1)
    y = jnp.concatenate([ret_o, fox_o.astype(ret_o.dtype), mem_o.astype(ret_o.dtype)], axis=-1)
    return (y @ w_out).astype(h.dtype)


def swiglu(h, w_gate, w_up, w_down):
    return (jax.nn.silu(h @ w_gate) * (h @ w_up)) @ w_down


def moe_ffn(h, router_w, wg, wu, wd):
    B, S, D = h.shape
    t = h.reshape(B * S, D)
    logits = (t @ router_w).astype(jnp.float32)
    top_vals, top_idx = lax.top_k(logits, TOP_K)
    gates = jax.nn.softmax(top_vals, axis=-1)
    combine = jnp.sum(jax.nn.one_hot(top_idx, N_EXPERTS, dtype=jnp.float32) * gates[..., None], axis=1)
    y = jnp.zeros((B * S, D), jnp.float32)
    for e in range(N_EXPERTS):
        y = y + combine[:, e:e + 1] * swiglu(t, wg[e], wu[e], wd[e]).astype(jnp.float32)
    return y.astype(h.dtype).reshape(B, S, D)


def setup_inputs(seed: int = 0) -> dict:
    key = jax.random.key(seed)
    ks = jax.random.split(key, 22)
    f32 = jnp.float32

    def w(k, shape, fan_in):
        return jax.random.normal(k, shape, f32) * (fan_in ** -0.5)

    def gain(k, shape):
        return 1.0 + 0.02 * jax.random.normal(k, shape, f32)

    return {
        "x": jax.random.normal(ks[0], (BATCH, SEQ, D_MODEL), f32),
        "mem": jax.random.normal(ks[1], (BATCH, MEM_LEN, D_MODEL), f32),
        "attn_norm_g": gain(ks[2], (DEPTH, D_MODEL)),
        "w_in": w(ks[3], (DEPTH, D_MODEL, IN_COLS), D_MODEL),
        "forget_bias": jax.random.uniform(ks[4], (DEPTH, FOX_HEADS), f32, minval=1.0, maxval=3.0),
        "fox_q_norm_g": gain(ks[5], (DEPTH, FOX_DH)),
        "fox_k_norm_g": gain(ks[6], (DEPTH, FOX_DH)),
        "mem_q_norm_g": gain(ks[7], (DEPTH, MEM_DH)),
        "mem_k_norm_g": gain(ks[8], (DEPTH, MEM_DH)),
        "mem_norm_g": gain(ks[9], (DEPTH, D_MODEL)),
        "w_mem_kv": w(ks[10], (DEPTH, D_MODEL, 2 * MEM_HEADS * MEM_DH), D_MODEL),
        "out_norm_g": gain(ks[11], (DEPTH, D_MIX)),
        "w_out": w(ks[12], (DEPTH, D_MIX, D_MODEL), D_MIX),
        "ffn_norm_g": gain(ks[13], (DEPTH, D_MODEL)),
        "dense_w_gate": w(ks[14], (N_DENSE, D_MODEL, D_FF), D_MODEL),
        "dense_w_up": w(ks[15], (N_DENSE, D_MODEL, D_FF), D_MODEL),
        "dense_w_down": w(ks[16], (N_DENSE, D_FF, D_MODEL), D_FF),
        "router_w": w(ks[17], (N_MOE, D_MODEL, N_EXPERTS), D_MODEL),
        "expert_w_gate": w(ks[18], (N_MOE, N_EXPERTS, D_MODEL, D_FF), D_MODEL),
        "expert_w_up": w(ks[19], (N_MOE, N_EXPERTS, D_MODEL, D_FF), D_MODEL),
        "expert_w_down": w(ks[20], (N_MOE, N_EXPERTS, D_FF, D_MODEL), D_FF),
    }


def reference(x, mem, attn_norm_g, w_in, forget_bias, fox_q_norm_g, fox_k_norm_g, mem_q_norm_g,
              mem_k_norm_g, mem_norm_g, w_mem_kv, out_norm_g, w_out, ffn_norm_g, dense_w_gate,
              dense_w_up, dense_w_down, router_w, expert_w_gate, expert_w_up, expert_w_down):
    for l in range(DEPTH):
        h = rms_norm(x, attn_norm_g[l])
        x = x + hybrid_mixer(h, mem, w_in[l], forget_bias[l], fox_q_norm_g[l], fox_k_norm_g[l],
                             mem_q_norm_g[l], mem_k_norm_g[l], mem_norm_g[l], w_mem_kv[l],
                             out_norm_g[l], w_out[l])
        h = rms_norm(x, ffn_norm_g[l])
        j = l // 2
        if l % 2 == 0:
            x = x + swiglu(h, dense_w_gate[j], dense_w_up[j], dense_w_down[j])
        else:
            x = x + moe_ffn(h, router_w[j], expert_w_gate[j], expert_w_up[j], expert_w_down[j])
    return x
```

```python
import functools

import numpy as np
import jax
import jax.numpy as jnp
from jax import lax
from jax.experimental import pallas as pl
from jax.experimental.pallas import tpu as pltpu

F32 = jnp.float32
BF16 = jnp.bfloat16

D_MODEL = 1024
RET_HEADS = 4
RET_DK = 64
RET_DV = 128
FOX_HEADS = 4
FOX_DH = 64
MEM_HEADS = 4
MEM_DH = 64
RET_CHUNK = 128
D_FF = 2752
N_EXPERTS = 8
ROPE_BASE = 10000.0
EPS = 1e-6

LANE = 128
D_FF_PAD = 2816
FF_TILE = D_FF_PAD // 2
TOK_TILE = 512
MOE_TILE = 512
VMEM_LIMIT = 56 * 1024 * 1024
NEG = -1e30

RQ, RK, RV, RG = 0, 256, 512, 1024
FQ, FK, FV = 1536, 2048, 2560
MQ, FF_COL, P_IN = 3072, 3328, 3456
C_LANE = 64
N_PIECE = 3


def _in_proj_columns():
    cols = []
    for base in (0, 256):
        for half in range(2):
            for h in range(RET_HEADS):
                cols += [base + h * RET_DK + half * 32 + i for i in range(32)]
    cols += list(range(512, 1536))
    for base in (1536, 1792, 2048):
        for h in range(FOX_HEADS):
            cols += [base + h * FOX_DH + i for i in range(FOX_DH)] + [-1] * (LANE - FOX_DH)
    cols += list(range(2308, 2564))
    cols += list(range(2304, 2308)) + [-1] * (LANE - FOX_HEADS)
    assert len(cols) == P_IN
    return np.array(cols, np.int32)


def _split3(x):
    hi = x.astype(BF16)
    r1 = x - hi.astype(F32)
    mid = r1.astype(BF16)
    lo = (r1 - mid.astype(F32)).astype(BF16)
    return hi, mid, lo


def _group_mean_sq(x, bd):
    sq = x * x
    hi = sq.astype(BF16)
    lo = (sq - hi.astype(F32)).astype(BF16)
    return (jnp.dot(hi, bd, preferred_element_type=F32)
            + jnp.dot(lo, bd, preferred_element_type=F32))


def _silu(x):
    return x / (1.0 + jnp.exp(-x))


def _in_proj_kernel(tiles_per_seq, x_ref, g_ref, w_ref, cos_ref, sin_ref, fqg_ref, fkg_ref, mqg_ref,
                    fb_ref, tri_ref, bd_ref, pq_ref, pk_ref, oneq_ref, onek_ref,
                    rq_ref, rk_ref, rv_ref, rg_ref, fqa_ref, fka_ref, fv_ref, mq_ref, carry_ref):
    i = pl.program_id(0)

    @pl.when(i % tiles_per_seq == 0)
    def _():
        carry_ref[...] = jnp.zeros_like(carry_ref)

    x = x_ref[...]
    ms = jnp.mean(x * x, axis=-1, keepdims=True)
    h = (x * lax.rsqrt(ms + EPS) * g_ref[...]).astype(BF16)

    def proj(a, b):
        return jnp.dot(h, w_ref[:, a:b], preferred_element_type=F32)

    cos = cos_ref[...]
    sin = sin_ref[...]
    for base, ref, scale in ((RQ, rq_ref, RET_DK ** -0.5), (RK, rk_ref, 1.0)):
        x1 = proj(base, base + LANE)
        x2 = proj(base + LANE, base + 2 * LANE)
        ref[:, 0:LANE] = ((x1 * cos - x2 * sin) * scale).astype(BF16)
        ref[:, LANE:2 * LANE] = ((x2 * cos + x1 * sin) * scale).astype(BF16)
    rv_ref[...] = proj(RV, RV + 512).astype(BF16)
    rg_ref[...] = proj(RG, RG + 512).astype(BF16)
    fv_ref[...] = proj(FV, FV + 512).astype(BF16)

    mq = proj(MQ, MQ + 256)
    mq = mq * lax.rsqrt(_group_mean_sq(mq, bd_ref[...]) + EPS) * mqg_ref[...] * (MEM_DH ** -0.5)
    mq_ref[...] = mq.astype(BF16)

    z = proj(FF_COL, FF_COL + LANE) + fb_ref[...]
    logf = jnp.minimum(z, 0.0) - jnp.log(1.0 + jnp.exp(-jnp.abs(z)))
    lane = lax.broadcasted_iota(jnp.int32, logf.shape, 1)
    logf = jnp.where(lane < FOX_HEADS, logf, 0.0)
    tri = tri_ref[...]
    c = carry_ref[0:1, :]
    for piece in _split3(logf):
        c = c + jnp.dot(tri, piece, preferred_element_type=F32)
    tm = c.shape[0]
    carry_ref[0:1, :] = c[tm - 1:tm, :]

    cp = jnp.concatenate(_split3(c), axis=1)
    extra_q = jnp.dot(cp, pq_ref[...], preferred_element_type=F32)
    extra_k = jnp.dot(cp, pk_ref[...], preferred_element_type=F32)
    fq = proj(FQ, FQ + 512)
    fk = proj(FK, FK + 512)
    for hh in range(FOX_HEADS):
        sl = slice(hh * LANE, (hh + 1) * LANE)
        for t, gref, scale, extra, one, oref in ((fq[:, sl], fqg_ref, FOX_DH ** -0.5, extra_q[:, sl], oneq_ref, fqa_ref),
                                                 (fk[:, sl], fkg_ref, 1.0, extra_k[:, sl], onek_ref, fka_ref)):
            msq = jnp.sum(t * t, axis=-1, keepdims=True) * (1.0 / FOX_DH)
            tn = t * lax.rsqrt(msq + EPS) * gref[...] * scale
            oref[:, sl] = (tn + extra + one[...]).astype(BF16)


def _in_proj(x, g, w, cos, sin, fqg, fkg, mqg, fb, consts, seq):
    T = x.shape[0]
    tm = TOK_TILE
    nt = T // tm
    tps = seq // tm
    row = lambda i: (i, 0)
    fix = lambda i: (0, 0)
    full = lambda a: pl.BlockSpec(a.shape, fix)
    tri, bd, pq, pk, oneq, onek = consts
    outs = [jax.ShapeDtypeStruct((T, n), BF16) for n in (256, 256, 512, 512, 512, 512, 512, 256)]
    out_specs = [pl.BlockSpec((tm, n), row) for n in (256, 256, 512, 512, 512, 512, 512, 256)]
    return pl.pallas_call(
        functools.partial(_in_proj_kernel, tps),
        grid=(nt,),
        in_specs=[pl.BlockSpec((tm, D_MODEL), row), full(g), full(w),
                  pl.BlockSpec((tm, LANE), lambda i: (i % tps, 0)),
                  pl.BlockSpec((tm, LANE), lambda i: (i % tps, 0)),
                  full(fqg), full(fkg), full(mqg), full(fb), full(tri), full(bd), full(pq), full(pk),
                  full(oneq), full(onek)],
        out_specs=out_specs,
        out_shape=outs,
        scratch_shapes=[pltpu.VMEM((8, LANE), F32)],
        compiler_params=pltpu.CompilerParams(dimension_semantics=("arbitrary",),
                                             vmem_limit_bytes=VMEM_LIMIT),
        name="in_proj",
    )(x, g, w, cos, sin, fqg, fkg, mqg, fb, tri, bd, pq, pk, oneq, onek)


def _retention_kernel(chunks, rq_ref, rk_ref, rv_ref, rg_ref, hm_ref, dst_ref, qd_ref, kd_ref, cd_ref,
                      bm_ref, g_ref, o_ref, state_ref):
    @pl.when(pl.program_id(1) == 0)
    def _():
        state_ref[...] = jnp.zeros_like(state_ref)

    C = RET_CHUNK
    for ci in range(chunks):
        rows = slice(ci * C, (ci + 1) * C)
        qc = rq_ref[rows, :]
        kc = rk_ref[rows, :]
        vc = rv_ref[rows, :]
        qs = jnp.concatenate([qc * hm_ref[hh:hh + 1, :] for hh in range(RET_HEADS)], axis=0)
        sc = lax.dot_general(qs, kc, (((1,), (1,)), ((), ())), preferred_element_type=F32)
        sc = (sc * dst_ref[...]).astype(BF16)
        state = state_ref[...]
        qdec = (qc.astype(F32) * qd_ref[...]).astype(BF16)
        cross = jnp.dot(qdec, state.astype(BF16), preferred_element_type=F32)
        kdec = (kc.astype(F32) * kd_ref[...]).astype(BF16)
        kv = lax.dot_general(kdec, vc, (((0,), (0,)), ((), ())), preferred_element_type=F32)
        state_ref[...] = state * cd_ref[...] + kv * bm_ref[...]
        for hh in range(RET_HEADS):
            cols = slice(hh * RET_DV, (hh + 1) * RET_DV)
            o = jnp.dot(sc[hh * C:(hh + 1) * C, :], vc[:, cols], preferred_element_type=F32) + cross[:, cols]
            msq = jnp.mean(o * o, axis=-1, keepdims=True)
            y = o * lax.rsqrt(msq + EPS) * g_ref[:, cols]
            o_ref[rows, cols] = (y * _silu(rg_ref[rows, cols].astype(F32))).astype(BF16)


def _retention(rq, rk, rv, rg, tables, g_ret, batch, seq):
    T = rq.shape[0]
    tm = TOK_TILE
    tps = seq // tm
    row = lambda b, i: (b * tps + i, 0)
    fix = lambda b, i: (0, 0)
    full = lambda a: pl.BlockSpec(a.shape, fix)
    return pl.pallas_call(
        functools.partial(_retention_kernel, tm // RET_CHUNK),
        grid=(batch, tps),
        in_specs=[pl.BlockSpec((tm, 256), row), pl.BlockSpec((tm, 256), row),
                  pl.BlockSpec((tm, 512), row), pl.BlockSpec((tm, 512), row)]
                 + [full(t) for t in tables] + [full(g_ret)],
        out_specs=pl.BlockSpec((tm, 512), row),
        out_shape=jax.ShapeDtypeStruct((T, 512), BF16),
        scratch_shapes=[pltpu.VMEM((RET_HEADS * RET_DK, RET_HEADS * RET_DV), F32)],
        compiler_params=pltpu.CompilerParams(dimension_semantics=("arbitrary", "arbitrary"),
                                             vmem_limit_bytes=VMEM_LIMIT),
        name="retention",
    )(rq, rk, rv, rg, *tables, g_ret)


def _fox_kernel(tq, q_ref, k_ref, v_ref, g_ref, o_ref, m_ref, l_ref, acc_ref):
    i = pl.program_id(2)
    q = q_ref[...]
    m_ref[...] = jnp.full_like(m_ref, NEG)
    l_ref[...] = jnp.zeros_like(l_ref)
    acc_ref[...] = jnp.zeros_like(acc_ref)

    def block(j, masked):
        start = pl.multiple_of(j * tq, tq)
        k = k_ref[pl.ds(start, tq), :]
        v = v_ref[pl.ds(start, tq), :]
        s = lax.dot_general(q, k, (((1,), (1,)), ((), ())), preferred_element_type=F32)
        if masked:
            r = lax.broadcasted_iota(jnp.int32, s.shape, 0)
            c = lax.broadcasted_iota(jnp.int32, s.shape, 1)
            s = jnp.where(c <= r, s, NEG)
        m_prev = m_ref[...]
        m_new = jnp.maximum(m_prev, jnp.max(s, axis=-1, keepdims=True))
        alpha = jnp.exp(m_prev - m_new)
        p = jnp.exp(s - m_new)
        l_ref[...] = alpha * l_ref[...] + jnp.sum(p, axis=-1, keepdims=True)
        acc_ref[...] = alpha * acc_ref[...] + jnp.dot(p.astype(BF16), v, preferred_element_type=F32)
        m_ref[...] = m_new

    def body(j, carry):
        block(j, False)
        return carry

    lax.fori_loop(0, i, body, 0)
    block(i, True)
    o = acc_ref[...] / l_ref[...]
    msq = jnp.sum(o * o, axis=-1, keepdims=True) * (1.0 / FOX_DH)
    o_ref[...] = (o * lax.rsqrt(msq + EPS) * g_ref[...]).astype(BF16)


def _fox(fqa, fka, fv, g_fox, batch, seq):
    T = fqa.shape[0]
    tq = TOK_TILE
    nq = seq // tq
    return pl.pallas_call(
        functools.partial(_fox_kernel, tq),
        grid=(batch, FOX_HEADS, nq),
        in_specs=[pl.BlockSpec((tq, LANE), lambda b, h, i: (b * nq + i, h)),
                  pl.BlockSpec((seq, LANE), lambda b, h, i: (b, h)),
                  pl.BlockSpec((seq, LANE), lambda b, h, i: (b, h)),
                  pl.BlockSpec((1, LANE), lambda b, h, i: (0, h))],
        out_specs=pl.BlockSpec((tq, LANE), lambda b, h, i: (b * nq + i, h)),
        out_shape=jax.ShapeDtypeStruct((T, FOX_HEADS * LANE), BF16),
        scratch_shapes=[pltpu.VMEM((tq, 1), F32), pltpu.VMEM((tq, 1), F32), pltpu.VMEM((tq, LANE), F32)],
        compiler_params=pltpu.CompilerParams(dimension_semantics=("arbitrary", "arbitrary", "arbitrary"),
                                             vmem_limit_bytes=VMEM_LIMIT),
        name="fox",
    )(fqa, fka, fv, g_fox)


def _mem_prep_kernel(mem_ref, g_ref, w_ref, kg_ref, bd_ref, mkt_ref, mvs_ref):
    x = mem_ref[0]
    ms = jnp.mean(x * x, axis=-1, keepdims=True)
    mn = (x * lax.rsqrt(ms + EPS) * g_ref[0]).astype(BF16)
    mkv = jnp.dot(mn, w_ref[0], preferred_element_type=F32)
    width = MEM_HEADS * MEM_DH
    mk = mkv[:, :width]
    mv = mkv[:, width:]
    mk = mk * lax.rsqrt(_group_mean_sq(mk, bd_ref[...]) + EPS) * kg_ref[0]
    mkt = mk.T
    feat = lax.broadcasted_iota(jnp.int32, mkt.shape, 0) // MEM_DH
    lane_head = lax.broadcasted_iota(jnp.int32, mv.shape, 1) // MEM_DH
    M = x.shape[0]
    for hh in range(MEM_HEADS):
        mkt_ref[0, 0, :, hh * M:(hh + 1) * M] = jnp.where(feat == hh, mkt, 0.0).astype(BF16)
        mvs_ref[0, 0, hh * M:(hh + 1) * M, :] = jnp.where(lane_head == hh, mv, 0.0).astype(BF16)


def _mem_prep(mem, mem_norm_g, w_mem_kv, mem_k_g, bd):
    B, M, _ = mem.shape
    depth = w_mem_kv.shape[0]
    width = MEM_HEADS * MEM_DH
    return pl.pallas_call(
        _mem_prep_kernel,
        grid=(depth, B),
        in_specs=[pl.BlockSpec((1, M, D_MODEL), lambda l, b: (b, 0, 0)),
                  pl.BlockSpec((1, 1, D_MODEL), lambda l, b: (l, 0, 0)),
                  pl.BlockSpec((1, D_MODEL, 2 * width), lambda l, b: (l, 0, 0)),
                  pl.BlockSpec((1, 1, width), lambda l, b: (l, 0, 0)),
                  pl.BlockSpec(bd.shape, lambda l, b: (0, 0))],
        out_specs=[pl.BlockSpec((1, 1, width, MEM_HEADS * M), lambda l, b: (l, b, 0, 0)),
                   pl.BlockSpec((1, 1, MEM_HEADS * M, width), lambda l, b: (l, b, 0, 0))],
        out_shape=[jax.ShapeDtypeStruct((depth, B, width, MEM_HEADS * M), BF16),
                   jax.ShapeDtypeStruct((depth, B, MEM_HEADS * M, width), BF16)],
        compiler_params=pltpu.CompilerParams(dimension_semantics=("arbitrary", "arbitrary"),
                                             vmem_limit_bytes=VMEM_LIMIT),
        name="mem_prep",
    )(mem, mem_norm_g, w_mem_kv, mem_k_g, bd)


def _out_proj_body(ret_ref, fox_ref, mq_ref, mkt_ref, mvs_ref, bd_ref, gm_ref, w_ref, x_ref, fg_ref):
    M = mkt_ref.shape[-1] // MEM_HEADS
    s = jnp.dot(mq_ref[...], mkt_ref[0, 0], preferred_element_type=F32)
    mem_o = None
    for hh in range(MEM_HEADS):
        sh = s[:, hh * M:(hh + 1) * M]
        p = jnp.exp(sh - jnp.max(sh, axis=-1, keepdims=True))
        p = (p / jnp.sum(p, axis=-1, keepdims=True)).astype(BF16)
        part = jnp.dot(p, mvs_ref[0, 0, hh * M:(hh + 1) * M, :], preferred_element_type=F32)
        mem_o = part if mem_o is None else mem_o + part
    mem_o = (mem_o * lax.rsqrt(_group_mean_sq(mem_o, bd_ref[...]) + EPS) * gm_ref[...]).astype(BF16)
    y = (jnp.dot(ret_ref[...], w_ref[0:512, :], preferred_element_type=F32)
         + jnp.dot(fox_ref[...], w_ref[512:1024, :], preferred_element_type=F32)
         + jnp.dot(mem_o, w_ref[1024:1280, :], preferred_element_type=F32))
    xm = x_ref[...] + y
    ms = jnp.mean(xm * xm, axis=-1, keepdims=True)
    return xm, xm * lax.rsqrt(ms + EPS) * fg_ref[...]


def _out_proj_dense_kernel(ret_ref, fox_ref, mq_ref, mkt_ref, mvs_ref, bd_ref, gm_ref, w_ref, x_ref, fg_ref,
                           xm_ref, h_ref):
    xm, hn = _out_proj_body(ret_ref, fox_ref, mq_ref, mkt_ref, mvs_ref, bd_ref, gm_ref, w_ref, x_ref, fg_ref)
    xm_ref[...] = xm
    h_ref[...] = hn.astype(BF16)


def _out_proj_moe_kernel(ret_ref, fox_ref, mq_ref, mkt_ref, mvs_ref, bd_ref, gm_ref, w_ref, x_ref, fg_ref,
                         rw_ref, tri_ref, xm_ref, h_ref, route_ref, cnt_ref, carry_ref):
    @pl.when(pl.program_id(0) == 0)
    def _():
        carry_ref[...] = jnp.zeros_like(carry_ref)

    xm, hn = _out_proj_body(ret_ref, fox_ref, mq_ref, mkt_ref, mvs_ref, bd_ref, gm_ref, w_ref, x_ref, fg_ref)
    xm_ref[...] = xm
    h_ref[...] = hn
    logits = jnp.dot(hn, rw_ref[...], preferred_element_type=F32, precision=lax.Precision.HIGHEST)
    lane = lax.broadcasted_iota(jnp.int32, logits.shape, 1)
    lanef = lane.astype(F32)
    lg = jnp.where(lane < N_EXPERTS, logits, NEG)
    m1 = jnp.max(lg, axis=-1, keepdims=True)
    i1 = jnp.min(jnp.where(lg == m1, lanef, float(LANE)), axis=-1, keepdims=True)
    oh1 = lanef == i1
    lg2 = jnp.where(oh1, NEG, lg)
    m2 = jnp.max(lg2, axis=-1, keepdims=True)
    i2 = jnp.min(jnp.where(lg2 == m2, lanef, float(LANE)), axis=-1, keepdims=True)
    oh2 = lanef == i2
    e = jnp.exp(m2 - m1)
    g1 = 1.0 / (1.0 + e)
    g2 = e / (1.0 + e)
    oh = jnp.where(oh1 | oh2, 1.0, 0.0)
    before = carry_ref[0:1, :] + jnp.dot(tri_ref[...], oh.astype(BF16), preferred_element_type=F32)
    r1 = jnp.sum(jnp.where(oh1, before, 0.0), axis=-1, keepdims=True)
    r2 = jnp.sum(jnp.where(oh2, before, 0.0), axis=-1, keepdims=True)
    tm = oh.shape[0]
    total = before[tm - 1:tm, :] + oh[tm - 1:tm, :]
    carry_ref[0:1, :] = total
    cnt_ref[...] = jnp.broadcast_to(total, cnt_ref.shape)
    route = jnp.zeros_like(logits)
    for col, val in enumerate((i1, i2, r1, r2, g1, g2)):
        route = jnp.where(lane == col, val, route)
    route_ref[...] = route


def _out_proj(ret_o, fox_o, mq, mkt, mvs, bd, g_mem, w_out, x, ffn_g, layer, seq, router=None):
    T = x.shape[0]
    tm = TOK_TILE
    tps = seq // tm
    row = lambda i: (i, 0)
    fix = lambda i: (0, 0)
    full = lambda a: pl.BlockSpec(a.shape, fix)
    M4 = mkt.shape[-1]
    in_specs = [pl.BlockSpec((tm, 512), row), pl.BlockSpec((tm, 512), row), pl.BlockSpec((tm, 256), row),
                pl.BlockSpec((1, 1, 256, M4), lambda i: (layer, i // tps, 0, 0)),
                pl.BlockSpec((1, 1, M4, 256), lambda i: (layer, i // tps, 0, 0)),
                full(bd), full(g_mem), full(w_out), pl.BlockSpec((tm, D_MODEL), row), full(ffn_g)]
    args = [ret_o, fox_o, mq, mkt, mvs, bd, g_mem, w_out, x, ffn_g]
    params = pltpu.CompilerParams(dimension_semantics=("arbitrary",), vmem_limit_bytes=VMEM_LIMIT)
    if router is None:
        return pl.pallas_call(
            _out_proj_dense_kernel, grid=(T // tm,), in_specs=in_specs,
            out_specs=[pl.BlockSpec((tm, D_MODEL), row), pl.BlockSpec((tm, D_MODEL), row)],
            out_shape=[jax.ShapeDtypeStruct((T, D_MODEL), F32), jax.ShapeDtypeStruct((T, D_MODEL), BF16)],
            compiler_params=params, name="out_proj_dense",
        )(*args)
    rw, tri = router
    return pl.pallas_call(
        _out_proj_moe_kernel, grid=(T // tm,), in_specs=in_specs + [full(rw), full(tri)],
        out_specs=[pl.BlockSpec((tm, D_MODEL), row), pl.BlockSpec((tm, D_MODEL), row),
                   pl.BlockSpec((tm, LANE), row), pl.BlockSpec((8, LANE), fix)],
        out_shape=[jax.ShapeDtypeStruct((T, D_MODEL), F32), jax.ShapeDtypeStruct((T, D_MODEL), F32),
                   jax.ShapeDtypeStruct((T, LANE), F32), jax.ShapeDtypeStruct((8, LANE), F32)],
        scratch_shapes=[pltpu.VMEM((8, LANE), F32)],
        compiler_params=params, name="out_proj_moe",
    )(*args, rw, tri)


def _dense_ffn_kernel(h_ref, wg_ref, wu_ref, wd_ref, x_ref, o_ref, acc_ref):
    j = pl.program_id(1)

    @pl.when(j == 0)
    def _():
        acc_ref[...] = x_ref[...]

    h = h_ref[...]
    g = jnp.dot(h, wg_ref[0], preferred_element_type=F32)
    u = jnp.dot(h, wu_ref[0], preferred_element_type=F32)
    a = (_silu(g) * u).astype(BF16)
    acc_ref[...] += jnp.dot(a, wd_ref[0], preferred_element_type=F32)

    @pl.when(j == pl.num_programs(1) - 1)
    def _():
        o_ref[...] = acc_ref[...]


def _dense_ffn(h, wg, wu, wd, x, layer):
    T = h.shape[0]
    tm = TOK_TILE
    nf = D_FF_PAD // FF_TILE
    return pl.pallas_call(
        _dense_ffn_kernel,
        grid=(T // tm, nf),
        in_specs=[pl.BlockSpec((tm, D_MODEL), lambda i, j: (i, 0)),
                  pl.BlockSpec((1, D_MODEL, FF_TILE), lambda i, j: (layer, 0, j)),
                  pl.BlockSpec((1, D_MODEL, FF_TILE), lambda i, j: (layer, 0, j)),
                  pl.BlockSpec((1, FF_TILE, D_MODEL), lambda i, j: (layer, j, 0)),
                  pl.BlockSpec((tm, D_MODEL), lambda i, j: (i, 0))],
        out_specs=pl.BlockSpec((tm, D_MODEL), lambda i, j: (i, 0)),
        out_shape=jax.ShapeDtypeStruct((T, D_MODEL), F32),
        scratch_shapes=[pltpu.VMEM((tm, D_MODEL), F32)],
        compiler_params=pltpu.CompilerParams(dimension_semantics=("arbitrary", "arbitrary"),
                                             vmem_limit_bytes=VMEM_LIMIT),
        name="dense_ffn",
    )(h, wg, wu, wd, x)


def _row_copy(src, s, dst, d, sem):
    return pltpu.make_async_copy(src.at[pl.ds(s, 1)], dst.at[pl.ds(d, 1)], sem)


def _scatter_kernel(tc, d1_ref, d2_ref, h_ref, xs_in_ref, xs_ref, sem):
    del xs_in_ref
    base = pl.program_id(0) * tc

    def issue(r, carry):
        t = base + r
        _row_copy(h_ref, t, xs_ref, d1_ref[t], sem).start()
        _row_copy(h_ref, t, xs_ref, d2_ref[t], sem).start()
        return carry

    def drain(r, carry):
        _row_copy(h_ref, 0, xs_ref, 0, sem).wait()
        _row_copy(h_ref, 0, xs_ref, 0, sem).wait()
        return carry

    lax.fori_loop(0, tc, issue, 0)
    lax.fori_loop(0, tc, drain, 0)


def _scatter(h, d1, d2, rows):
    T = h.shape[0]
    tc = 2048
    xs0 = jnp.zeros((rows, D_MODEL), F32)
    any_spec = pl.BlockSpec(memory_space=pl.ANY)
    return pl.pallas_call(
        functools.partial(_scatter_kernel, tc),
        grid_spec=pltpu.PrefetchScalarGridSpec(
            num_scalar_prefetch=2, grid=(T // tc,),
            in_specs=[any_spec, any_spec], out_specs=any_spec,
            scratch_shapes=[pltpu.SemaphoreType.DMA(())]),
        out_shape=jax.ShapeDtypeStruct((rows, D_MODEL), F32),
        input_output_aliases={3: 0},
        compiler_params=pltpu.CompilerParams(dimension_semantics=("arbitrary",), has_side_effects=True),
        name="moe_scatter",
    )(d1, d2, h, xs0)


def _moe_ffn_kernel(te_ref, nu_ref, xs_ref, wg_ref, wu_ref, wd_ref, o_ref, acc_ref):
    i = pl.program_id(0)
    j = pl.program_id(1)

    @pl.when(i < nu_ref[0])
    def _():
        x = xs_ref[...].astype(BF16)
        g = jnp.dot(x, wg_ref[0, 0], preferred_element_type=F32)
        u = jnp.dot(x, wu_ref[0, 0], preferred_element_type=F32)
        a = (_silu(g) * u).astype(BF16)
        part = jnp.dot(a, wd_ref[0, 0], preferred_element_type=F32)

        @pl.when(j == 0)
        def _():
            acc_ref[...] = part

        @pl.when(j > 0)
        def _():
            acc_ref[...] += part

        @pl.when(j == pl.num_programs(1) - 1)
        def _():
            o_ref[...] = acc_ref[...]

    @pl.when(i >= nu_ref[0])
    def _():
        o_ref[...] = jnp.zeros_like(o_ref)


def _moe_ffn(xs, tile_expert, n_used, wg, wu, wd, layer):
    rows = xs.shape[0]
    tm = MOE_TILE
    nf = D_FF_PAD // FF_TILE
    last = nf - 1

    def tile(i, j, te, nu):
        return (jnp.minimum(i, nu[0] - 1), 0)

    def w_cols(i, j, te, nu):
        return (layer, te[i], 0, jnp.where(i < nu[0], j, last))

    def w_rows(i, j, te, nu):
        return (layer, te[i], jnp.where(i < nu[0], j, last), 0)

    return pl.pallas_call(
        _moe_ffn_kernel,
        grid_spec=pltpu.PrefetchScalarGridSpec(
            num_scalar_prefetch=2, grid=(rows // tm, nf),
            in_specs=[pl.BlockSpec((tm, D_MODEL), tile),
                      pl.BlockSpec((1, 1, D_MODEL, FF_TILE), w_cols),
                      pl.BlockSpec((1, 1, D_MODEL, FF_TILE), w_cols),
                      pl.BlockSpec((1, 1, FF_TILE, D_MODEL), w_rows)],
            out_specs=pl.BlockSpec((tm, D_MODEL), lambda i, j, te, nu: (i, 0)),
            scratch_shapes=[pltpu.VMEM((tm, D_MODEL), F32)]),
        out_shape=jax.ShapeDtypeStruct((rows, D_MODEL), F32),
        compiler_params=pltpu.CompilerParams(dimension_semantics=("arbitrary", "arbitrary"),
                                             vmem_limit_bytes=VMEM_LIMIT),
        name="moe_ffn",
    )(tile_expert, n_used, xs, wg, wu, wd)


def _combine_kernel(tc, d1_ref, d2_ref, o_hbm, x_ref, route_ref, out_ref, b1_ref, b2_ref, sem):
    base = pl.program_id(0) * tc

    def issue(r, carry):
        t = base + r
        _row_copy(o_hbm, d1_ref[t], b1_ref, r, sem).start()
        _row_copy(o_hbm, d2_ref[t], b2_ref, r, sem).start()
        return carry

    def drain(r, carry):
        _row_copy(o_hbm, 0, b1_ref, 0, sem).wait()
        _row_copy(o_hbm, 0, b2_ref, 0, sem).wait()
        return carry

    lax.fori_loop(0, tc, issue, 0)
    lax.fori_loop(0, tc, drain, 0)
    route = route_ref[...]
    g1 = route[:, 4:5]
    g2 = route[:, 5:6]
    out_ref[...] = x_ref[...] + g1 * b1_ref[...] + g2 * b2_ref[...]


def _combine(o, d1, d2, x, route):
    T = x.shape[0]
    tc = 256
    return pl.pallas_call(
        functools.partial(_combine_kernel, tc),
        grid_spec=pltpu.PrefetchScalarGridSpec(
            num_scalar_prefetch=2, grid=(T // tc,),
            in_specs=[pl.BlockSpec(memory_space=pl.ANY),
                      pl.BlockSpec((tc, D_MODEL), lambda i, a, b: (i, 0)),
                      pl.BlockSpec((tc, LANE), lambda i, a, b: (i, 0))],
            out_specs=pl.BlockSpec((tc, D_MODEL), lambda i, a, b: (i, 0)),
            scratch_shapes=[pltpu.VMEM((tc, D_MODEL), F32), pltpu.VMEM((tc, D_MODEL), F32),
                            pltpu.SemaphoreType.DMA(())]),
        out_shape=jax.ShapeDtypeStruct((T, D_MODEL), F32),
        compiler_params=pltpu.CompilerParams(dimension_semantics=("arbitrary",),
                                             vmem_limit_bytes=VMEM_LIMIT),
        name="moe_combine",
    )(d1, d2, o, x, route)


def _moe_layer(h, xm, route, counts, wg, wu, wd, layer):
    T = h.shape[0]
    tm = MOE_TILE
    n_tiles = (2 * T) // tm + N_EXPERTS
    e1 = route[:, 0].astype(jnp.int32)
    e2 = route[:, 1].astype(jnp.int32)
    r1 = route[:, 2].astype(jnp.int32)
    r2 = route[:, 3].astype(jnp.int32)
    cnt = counts[0, :N_EXPERTS].astype(jnp.int32)
    tiles = (cnt + tm - 1) // tm
    tile_end = jnp.cumsum(tiles)
    offs = (tile_end - tiles) * tm
    d1 = offs[e1] + r1
    d2 = offs[e2] + r2
    n_used = tile_end[-1:]
    tile_id = jnp.arange(n_tiles, dtype=jnp.int32)
    tile_expert = jnp.sum(tile_id[:, None] >= tile_end[None, :], axis=1).astype(jnp.int32)
    tile_expert = jnp.minimum(tile_expert, tile_expert[jnp.maximum(n_used[0] - 1, 0)])
    xs = _scatter(h, d1, d2, n_tiles * tm)
    o = _moe_ffn(xs, tile_expert, n_used.astype(jnp.int32), wg, wu, wd, layer)
    return _combine(o, d1, d2, xm, route)


def _constants(seq):
    tm = TOK_TILE
    r = np.arange(tm)
    tri_incl = (r[:, None] >= r[None, :]).astype(np.float32)
    tri_strict = (r[:, None] > r[None, :]).astype(np.float32)
    d = np.arange(256)
    bd = (d[:, None] // 64 == d[None, :] // 64).astype(np.float32) / 64.0
    pq = np.zeros((N_PIECE * LANE, FOX_HEADS * LANE), np.float32)
    pk = np.zeros_like(pq)
    oneq = np.zeros((1, LANE), np.float32)
    onek = np.zeros((1, LANE), np.float32)
    for p in range(N_PIECE):
        oneq[0, C_LANE + N_PIECE + p] = 1.0
        onek[0, C_LANE + p] = 1.0
        for h in range(FOX_HEADS):
            pq[p * LANE + h, h * LANE + C_LANE + p] = 1.0
            pk[p * LANE + h, h * LANE + C_LANE + N_PIECE + p] = -1.0
    in_consts = (jnp.asarray(tri_incl, BF16), jnp.asarray(bd, BF16), jnp.asarray(pq, BF16),
                 jnp.asarray(pk, BF16), jnp.asarray(oneq), jnp.asarray(onek))

    half = RET_DK // 2
    pos = jnp.arange(seq, dtype=F32)
    inv_freq = ROPE_BASE ** (-jnp.arange(half, dtype=F32) / half)
    ang = pos[:, None] * inv_freq[None, :]
    cos = jnp.tile(jnp.cos(ang), (1, RET_HEADS))
    sin = jnp.tile(jnp.sin(ang), (1, RET_HEADS))

    C = RET_CHUNK
    hh = jnp.arange(RET_HEADS, dtype=F32)
    log_g = jnp.log(1.0 - 2.0 ** (-5.0 - hh))
    idx = jnp.arange(C, dtype=F32)
    diff = idx[:, None] - idx[None, :]
    dmat = jnp.where(diff >= 0, jnp.exp(log_g[:, None, None] * jnp.maximum(diff, 0.0)), 0.0)
    dstack = dmat.reshape(RET_HEADS * C, C)
    k_head = (np.arange(256) % LANE) // 32
    v_head = np.arange(RET_HEADS * RET_DV) // RET_DV
    qd = jnp.exp(log_g[k_head][None, :] * (idx[:, None] + 1.0))
    kd = jnp.exp(log_g[k_head][None, :] * (C - 1.0 - idx[:, None]))
    cd = jnp.exp(log_g[v_head] * C)[None, :]
    bm = jnp.asarray((k_head[:, None] == v_head[None, :]).astype(np.float32))
    hm = jnp.asarray((np.arange(RET_HEADS)[:, None] == k_head[None, :]).astype(np.float32), BF16)
    ret_tables = (hm, dstack, qd, kd, cd, bm)
    return in_consts, cos, sin, ret_tables, jnp.asarray(tri_strict, BF16)


def _pad_heads(g):
    depth = g.shape[0]
    g = g.reshape(depth, FOX_HEADS, FOX_DH)
    return jnp.pad(g, ((0, 0), (0, 0), (0, LANE - FOX_DH))).reshape(depth, 1, FOX_HEADS * LANE)


def kernel(x, mem, attn_norm_g, w_in, forget_bias, fox_q_norm_g, fox_k_norm_g, mem_q_norm_g, mem_k_norm_g,
           mem_norm_g, w_mem_kv, out_norm_g, w_out, ffn_norm_g, dense_w_gate, dense_w_up, dense_w_down,
           router_w, expert_w_gate, expert_w_up, expert_w_down):
    B, S, D = x.shape
    depth = w_in.shape[0]
    T = B * S
    in_consts, cos, sin, ret_tables, tri_strict = _constants(S)
    bd = in_consts[1]

    cols = _in_proj_columns()
    w_in_z = jnp.concatenate([w_in, jnp.zeros((depth, D, 1), w_in.dtype)], axis=-1)
    w_in_p = jnp.take(w_in_z, jnp.asarray(np.where(cols < 0, w_in.shape[-1], cols)), axis=-1).astype(BF16)
    lane_pad = lambda g: jnp.pad(g, ((0, 0), (0, LANE - g.shape[-1])))[:, None, :]
    fqg = lane_pad(fox_q_norm_g)
    fkg = lane_pad(fox_k_norm_g)
    fb = lane_pad(forget_bias)
    mqg = jnp.tile(mem_q_norm_g, (1, MEM_HEADS))[:, None, :]
    mkg = jnp.tile(mem_k_norm_g, (1, MEM_HEADS))[:, None, :]
    n_ret = RET_HEADS * RET_DV
    n_fox = FOX_HEADS * FOX_DH
    g_ret = out_norm_g[:, None, :n_ret]
    g_fox = _pad_heads(out_norm_g[:, n_ret:n_ret + n_fox])
    g_mem = out_norm_g[:, None, n_ret + n_fox:]
    w_fox = w_out[:, n_ret:n_ret + n_fox].reshape(depth, FOX_HEADS, FOX_DH, D)
    w_fox = jnp.pad(w_fox, ((0, 0), (0, 0), (0, LANE - FOX_DH), (0, 0))).reshape(depth, FOX_HEADS * LANE, D)
    w_out_p = jnp.concatenate([w_out[:, :n_ret], w_fox, w_out[:, n_ret + n_fox:]], axis=1).astype(BF16)
    ffpad = D_FF_PAD - D_FF
    pad_cols = lambda w: jnp.pad(w, [(0, 0)] * (w.ndim - 1) + [(0, ffpad)]).astype(BF16)
    pad_rows = lambda w: jnp.pad(w, [(0, 0)] * (w.ndim - 2) + [(0, ffpad), (0, 0)]).astype(BF16)
    dwg, dwu, dwd = pad_cols(dense_w_gate), pad_cols(dense_w_up), pad_rows(dense_w_down)
    ewg, ewu, ewd = pad_cols(expert_w_gate), pad_cols(expert_w_up), pad_rows(expert_w_down)
    rw = jnp.pad(router_w, ((0, 0), (0, 0), (0, LANE - N_EXPERTS)))

    mkt, mvs = _mem_prep(mem, mem_norm_g[:, None, :], w_mem_kv.astype(BF16), mkg, bd)

    xt = x.reshape(T, D)
    for l in range(depth):
        rq, rk, rv, rg, fqa, fka, fv, mq = _in_proj(
            xt, attn_norm_g[l][None], w_in_p[l], cos, sin, fqg[l], fkg[l], mqg[l], fb[l], in_consts, S)
        ret_o = _retention(rq, rk, rv, rg, ret_tables, g_ret[l], B, S)
        fox_o = _fox(fqa, fka, fv, g_fox[l], B, S)
        j = l // 2
        if l % 2 == 0:
            xm, h2 = _out_proj(ret_o, fox_o, mq, mkt, mvs, bd, g_mem[l], w_out_p[l], xt, ffn_norm_g[l][None], l, S)
            xt = _dense_ffn(h2, dwg, dwu, dwd, xm, j)
        else:
            xm, h2, route, counts = _out_proj(ret_o, fox_o, mq, mkt, mvs, bd, g_mem[l], w_out_p[l], xt,
                                              ffn_norm_g[l][None], l, S, router=(rw[j], tri_strict))
            xt = _moe_layer(h2, xm, route, counts, ewg, ewu, ewd, j)
    return xt.reshape(B, S, D)
```

```python
import functools

import numpy as np
import jax
import jax.numpy as jnp
from jax import lax
from jax.experimental import pallas as pl
from jax.experimental.pallas import tpu as pltpu

F32 = jnp.float32
BF16 = jnp.bfloat16

D_MODEL = 1024
RET_HEADS = 4
RET_DK = 64
RET_DV = 128
FOX_HEADS = 4
FOX_DH = 64
MEM_HEADS = 4
MEM_DH = 64
RET_CHUNK = 128
D_FF = 2752
N_EXPERTS = 8
ROPE_BASE = 10000.0
EPS = 1e-6

LANE = 128
D_FF_PAD = 2816
FF_TILE = D_FF_PAD // 2
TOK_TILE = 512
MOE_TILE = 512
VMEM_LIMIT = 56 * 1024 * 1024
NEG = -1e30

RQ, RK, RV, RG = 0, 256, 512, 1024
FQ, FK, FV = 1536, 2048, 2560
MQ, FF_COL, P_IN = 3072, 3328, 3456
C_LANE = 64
N_PIECE = 3
V_ONE = 64


def _in_proj_columns():
    cols = []
    for base in (0, 256):
        for half in range(2):
            for h in range(RET_HEADS):
                cols += [base + h * RET_DK + half * 32 + i for i in range(32)]
    cols += list(range(512, 1536))
    for base in (1536, 1792, 2048):
        for h in range(FOX_HEADS):
            cols += [base + h * FOX_DH + i for i in range(FOX_DH)] + [-1] * (LANE - FOX_DH)
    cols += list(range(2308, 2564))
    cols += list(range(2304, 2308)) + [-1] * (LANE - FOX_HEADS)
    assert len(cols) == P_IN
    return np.array(cols, np.int32)


def _split3(x):
    hi = x.astype(BF16)
    r1 = x - hi.astype(F32)
    mid = r1.astype(BF16)
    lo = (r1 - mid.astype(F32)).astype(BF16)
    return hi, mid, lo


def _group_mean_sq(x, bd):
    sq = x * x
    hi = sq.astype(BF16)
    lo = (sq - hi.astype(F32)).astype(BF16)
    return (jnp.dot(hi, bd, preferred_element_type=F32)
            + jnp.dot(lo, bd, preferred_element_type=F32))


def _silu(x):
    return x / (1.0 + jnp.exp(-x))


def _in_proj_kernel(tiles_per_seq, x_ref, g_ref, w_ref, cos_ref, sin_ref, fqg_ref, fkg_ref, mqg_ref,
                    fb_ref, tri_ref, bd_ref, pq_ref, pk_ref, oneq_ref, onek_ref,
                    rq_ref, rk_ref, rv_ref, rg_ref, fqt_ref, fka_ref, fvt_ref, mq_ref, carry_ref):
    i = pl.program_id(0)

    @pl.when(i % tiles_per_seq == 0)
    def _():
        carry_ref[...] = jnp.zeros_like(carry_ref)

    x = x_ref[...]
    ms = jnp.mean(x * x, axis=-1, keepdims=True)
    h = (x * lax.rsqrt(ms + EPS) * g_ref[...]).astype(BF16)

    def proj(a, b):
        return jnp.dot(h, w_ref[:, a:b], preferred_element_type=F32)

    cos = cos_ref[...]
    sin = sin_ref[...]
    for base, ref, scale in ((RQ, rq_ref, RET_DK ** -0.5), (RK, rk_ref, 1.0)):
        x1 = proj(base, base + LANE)
        x2 = proj(base + LANE, base + 2 * LANE)
        ref[:, 0:LANE] = ((x1 * cos - x2 * sin) * scale).astype(BF16)
        ref[:, LANE:2 * LANE] = ((x2 * cos + x1 * sin) * scale).astype(BF16)
    rv_ref[...] = proj(RV, RV + 512).astype(BF16)
    rg_ref[...] = proj(RG, RG + 512).astype(BF16)
    vt = proj(FV, FV + 512).T
    vrow = lax.broadcasted_iota(jnp.int32, vt.shape, 0) % LANE
    fvt_ref[...] = jnp.where(vrow == V_ONE, 1.0, vt).astype(BF16)

    mq = proj(MQ, MQ + 256)
    mq = mq * lax.rsqrt(_group_mean_sq(mq, bd_ref[...]) + EPS) * mqg_ref[...] * (MEM_DH ** -0.5)
    mq_ref[...] = mq.astype(BF16)

    z = proj(FF_COL, FF_COL + LANE) + fb_ref[...]
    logf = jnp.minimum(z, 0.0) - jnp.log(1.0 + jnp.exp(-jnp.abs(z)))
    lane = lax.broadcasted_iota(jnp.int32, logf.shape, 1)
    logf = jnp.where(lane < FOX_HEADS, logf, 0.0)
    tri = tri_ref[...]
    c = carry_ref[0:1, :]
    for piece in _split3(logf):
        c = c + jnp.dot(tri, piece, preferred_element_type=F32)
    tm = c.shape[0]
    carry_ref[0:1, :] = c[tm - 1:tm, :]

    cp = jnp.concatenate(_split3(c), axis=1)
    extra_q = jnp.dot(cp, pq_ref[...], preferred_element_type=F32)
    extra_k = jnp.dot(cp, pk_ref[...], preferred_element_type=F32)
    fq = proj(FQ, FQ + 512)
    fk = proj(FK, FK + 512)
    q_tiles = []
    for hh in range(FOX_HEADS):
        sl = slice(hh * LANE, (hh + 1) * LANE)
        tiles = []
        for t, gref, scale, extra, one in ((fq[:, sl], fqg_ref, FOX_DH ** -0.5, extra_q[:, sl], oneq_ref),
                                           (fk[:, sl], fkg_ref, 1.0, extra_k[:, sl], onek_ref)):
            msq = jnp.sum(t * t, axis=-1, keepdims=True) * (1.0 / FOX_DH)
            tn = t * lax.rsqrt(msq + EPS) * gref[...] * scale
            tiles.append(tn + extra + one[...])
        q_tiles.append(tiles[0])
        fka_ref[:, sl] = tiles[1].astype(BF16)
    fqt_ref[...] = jnp.concatenate(q_tiles, axis=1).T.astype(BF16)


def _in_proj(x, g, w, cos, sin, fqg, fkg, mqg, fb, consts, seq):
    T = x.shape[0]
    tm = TOK_TILE
    nt = T // tm
    tps = seq // tm
    row = lambda i: (i, 0)
    fix = lambda i: (0, 0)
    full = lambda a: pl.BlockSpec(a.shape, fix)
    tri, bd, pq, pk, oneq, onek = consts
    widths = (256, 256, 512, 512, -512, 512, -512, 256)
    outs = [jax.ShapeDtypeStruct((T, n) if n > 0 else (-n, T), BF16) for n in widths]
    out_specs = [pl.BlockSpec((tm, n), row) if n > 0 else pl.BlockSpec((-n, tm), lambda i: (0, i))
                 for n in widths]
    return pl.pallas_call(
        functools.partial(_in_proj_kernel, tps),
        grid=(nt,),
        in_specs=[pl.BlockSpec((tm, D_MODEL), row), full(g), full(w),
                  pl.BlockSpec((tm, LANE), lambda i: (i % tps, 0)),
                  pl.BlockSpec((tm, LANE), lambda i: (i % tps, 0)),
                  full(fqg), full(fkg), full(mqg), full(fb), full(tri), full(bd), full(pq), full(pk),
                  full(oneq), full(onek)],
        out_specs=out_specs,
        out_shape=outs,
        scratch_shapes=[pltpu.VMEM((8, LANE), F32)],
        compiler_params=pltpu.CompilerParams(dimension_semantics=("arbitrary",),
                                             vmem_limit_bytes=VMEM_LIMIT),
        name="in_proj",
    )(x, g, w, cos, sin, fqg, fkg, mqg, fb, tri, bd, pq, pk, oneq, onek)


def _retention_kernel(chunks, rq_ref, rk_ref, rv_ref, rg_ref, hm_ref, dst_ref, qd_ref, kd_ref, cd_ref,
                      bm_ref, g_ref, o_ref, state_ref):
    @pl.when(pl.program_id(1) == 0)
    def _():
        state_ref[...] = jnp.zeros_like(state_ref)

    C = RET_CHUNK
    for ci in range(chunks):
        rows = slice(ci * C, (ci + 1) * C)
        qc = rq_ref[rows, :]
        kc = rk_ref[rows, :]
        vc = rv_ref[rows, :]
        qs = jnp.concatenate([qc * hm_ref[hh:hh + 1, :] for hh in range(RET_HEADS)], axis=0)
        sc = lax.dot_general(qs, kc, (((1,), (1,)), ((), ())), preferred_element_type=F32)
        sc = (sc * dst_ref[...]).astype(BF16)
        state = state_ref[...]
        qdec = (qc.astype(F32) * qd_ref[...]).astype(BF16)
        cross = jnp.dot(qdec, state.astype(BF16), preferred_element_type=F32)
        kdec = (kc.astype(F32) * kd_ref[...]).astype(BF16)
        kv = lax.dot_general(kdec, vc, (((0,), (0,)), ((), ())), preferred_element_type=F32)
        state_ref[...] = state * cd_ref[...] + kv * bm_ref[...]
        for hh in range(RET_HEADS):
            cols = slice(hh * RET_DV, (hh + 1) * RET_DV)
            o = jnp.dot(sc[hh * C:(hh + 1) * C, :], vc[:, cols], preferred_element_type=F32) + cross[:, cols]
            msq = jnp.mean(o * o, axis=-1, keepdims=True)
            y = o * lax.rsqrt(msq + EPS) * g_ref[:, cols]
            o_ref[rows, cols] = (y * _silu(rg_ref[rows, cols].astype(F32))).astype(BF16)


def _retention(rq, rk, rv, rg, tables, g_ret, batch, seq):
    T = rq.shape[0]
    tm = TOK_TILE
    tps = seq // tm
    row = lambda b, i: (b * tps + i, 0)
    fix = lambda b, i: (0, 0)
    full = lambda a: pl.BlockSpec(a.shape, fix)
    return pl.pallas_call(
        functools.partial(_retention_kernel, tm // RET_CHUNK),
        grid=(batch, tps),
        in_specs=[pl.BlockSpec((tm, 256), row), pl.BlockSpec((tm, 256), row),
                  pl.BlockSpec((tm, 512), row), pl.BlockSpec((tm, 512), row)]
                 + [full(t) for t in tables] + [full(g_ret)],
        out_specs=pl.BlockSpec((tm, 512), row),
        out_shape=jax.ShapeDtypeStruct((T, 512), BF16),
        scratch_shapes=[pltpu.VMEM((RET_HEADS * RET_DK, RET_HEADS * RET_DV), F32)],
        compiler_params=pltpu.CompilerParams(dimension_semantics=("arbitrary", "arbitrary"),
                                             vmem_limit_bytes=VMEM_LIMIT),
        name="retention",
    )(rq, rk, rv, rg, *tables, g_ret)


def _fox_kernel(tq, qt_ref, k_ref, vt_ref, g_ref, o_ref, m_ref, acc_ref):
    i = pl.program_id(2)
    qt = qt_ref[...]
    m_ref[...] = jnp.full_like(m_ref, NEG)
    acc_ref[...] = jnp.zeros_like(acc_ref)

    def block(start, size, masked):
        k = k_ref[pl.ds(start, size), :]
        vt = vt_ref[:, pl.ds(start, size)]
        st = jnp.dot(k, qt, preferred_element_type=F32)
        if masked:
            key = lax.broadcasted_iota(jnp.int32, st.shape, 0)
            qry = lax.broadcasted_iota(jnp.int32, st.shape, 1)
            st = jnp.where(key <= qry, st, NEG)
        m_prev = m_ref[...]
        m_new = jnp.maximum(m_prev, jnp.max(st, axis=0, keepdims=True))
        alpha = jnp.exp(m_prev - m_new)
        p = jnp.exp(st - m_new).astype(BF16)
        acc_ref[...] = alpha * acc_ref[...] + jnp.dot(vt, p, preferred_element_type=F32)
        m_ref[...] = m_new

    def pair(j, carry):
        block(pl.multiple_of(j * (2 * tq), 2 * tq), 2 * tq, False)
        return carry

    lax.fori_loop(0, i // 2, pair, 0)

    @pl.when(i % 2 == 1)
    def _():
        block(pl.multiple_of((i - 1) * tq, tq), tq, False)

    block(pl.multiple_of(i * tq, tq), tq, True)
    acc = acc_ref[...]
    row = lax.broadcasted_iota(jnp.int32, acc.shape, 0)
    o = jnp.where(row < FOX_DH, acc / acc[V_ONE:V_ONE + 1, :], 0.0)
    msq = jnp.sum(o * o, axis=0, keepdims=True) * (1.0 / FOX_DH)
    o_ref[...] = ((o * lax.rsqrt(msq + EPS)).T * g_ref[...]).astype(BF16)


def _fox(fqt, fka, fvt, g_fox, batch, seq):
    T = fka.shape[0]
    tq = TOK_TILE
    nq = seq // tq
    return pl.pallas_call(
        functools.partial(_fox_kernel, tq),
        grid=(batch, FOX_HEADS, nq),
        in_specs=[pl.BlockSpec((LANE, tq), lambda b, h, i: (h, b * nq + i)),
                  pl.BlockSpec((seq, LANE), lambda b, h, i: (b, h)),
                  pl.BlockSpec((LANE, seq), lambda b, h, i: (h, b)),
                  pl.BlockSpec((1, LANE), lambda b, h, i: (0, h))],
        out_specs=pl.BlockSpec((tq, LANE), lambda b, h, i: (b * nq + i, h)),
        out_shape=jax.ShapeDtypeStruct((T, FOX_HEADS * LANE), BF16),
        scratch_shapes=[pltpu.VMEM((1, tq), F32), pltpu.VMEM((LANE, tq), F32)],
        compiler_params=pltpu.CompilerParams(dimension_semantics=("arbitrary", "arbitrary", "arbitrary"),
                                             vmem_limit_bytes=VMEM_LIMIT),
        name="fox",
    )(fqt, fka, fvt, g_fox)


def _mem_prep_kernel(mem_ref, g_ref, w_ref, kg_ref, bd_ref, mkt_ref, mvs_ref):
    x = mem_ref[0]
    ms = jnp.mean(x * x, axis=-1, keepdims=True)
    mn = (x * lax.rsqrt(ms + EPS) * g_ref[0]).astype(BF16)
    mkv = jnp.dot(mn, w_ref[0], preferred_element_type=F32)
    width = MEM_HEADS * MEM_DH
    mk = mkv[:, :width]
    mv = mkv[:, width:]
    mk = mk * lax.rsqrt(_group_mean_sq(mk, bd_ref[...]) + EPS) * kg_ref[0]
    mkt = mk.T
    feat = lax.broadcasted_iota(jnp.int32, mkt.shape, 0) // MEM_DH
    lane_head = lax.broadcasted_iota(jnp.int32, mv.shape, 1) // MEM_DH
    M = x.shape[0]
    for hh in range(MEM_HEADS):
        mkt_ref[0, 0, :, hh * M:(hh + 1) * M] = jnp.where(feat == hh, mkt, 0.0).astype(BF16)
        mvs_ref[0, 0, hh * M:(hh + 1) * M, :] = jnp.where(lane_head == hh, mv, 0.0).astype(BF16)


def _mem_prep(mem, mem_norm_g, w_mem_kv, mem_k_g, bd):
    B, M, _ = mem.shape
    depth = w_mem_kv.shape[0]
    width = MEM_HEADS * MEM_DH
    return pl.pallas_call(
        _mem_prep_kernel,
        grid=(depth, B),
        in_specs=[pl.BlockSpec((1, M, D_MODEL), lambda l, b: (b, 0, 0)),
                  pl.BlockSpec((1, 1, D_MODEL), lambda l, b: (l, 0, 0)),
                  pl.BlockSpec((1, D_MODEL, 2 * width), lambda l, b: (l, 0, 0)),
                  pl.BlockSpec((1, 1, width), lambda l, b: (l, 0, 0)),
                  pl.BlockSpec(bd.shape, lambda l, b: (0, 0))],
        out_specs=[pl.BlockSpec((1, 1, width, MEM_HEADS * M), lambda l, b: (l, b, 0, 0)),
                   pl.BlockSpec((1, 1, MEM_HEADS * M, width), lambda l, b: (l, b, 0, 0))],
        out_shape=[jax.ShapeDtypeStruct((depth, B, width, MEM_HEADS * M), BF16),
                   jax.ShapeDtypeStruct((depth, B, MEM_HEADS * M, width), BF16)],
        compiler_params=pltpu.CompilerParams(dimension_semantics=("arbitrary", "arbitrary"),
                                             vmem_limit_bytes=VMEM_LIMIT),
        name="mem_prep",
    )(mem, mem_norm_g, w_mem_kv, mem_k_g, bd)


def _out_proj_body(ret_ref, fox_ref, mq_ref, mkt_ref, mvs_ref, bd_ref, gm_ref, w_ref, x_ref, fg_ref):
    M = mkt_ref.shape[-1] // MEM_HEADS
    s = jnp.dot(mq_ref[...], mkt_ref[0, 0], preferred_element_type=F32)
    mem_o = None
    for hh in range(MEM_HEADS):
        sh = s[:, hh * M:(hh + 1) * M]
        p = jnp.exp(sh - jnp.max(sh, axis=-1, keepdims=True))
        p = (p / jnp.sum(p, axis=-1, keepdims=True)).astype(BF16)
        part = jnp.dot(p, mvs_ref[0, 0, hh * M:(hh + 1) * M, :], preferred_element_type=F32)
        mem_o = part if mem_o is None else mem_o + part
    mem_o = (mem_o * lax.rsqrt(_group_mean_sq(mem_o, bd_ref[...]) + EPS) * gm_ref[...]).astype(BF16)
    y = (jnp.dot(ret_ref[...], w_ref[0:512, :], preferred_element_type=F32)
         + jnp.dot(fox_ref[...], w_ref[512:1024, :], preferred_element_type=F32)
         + jnp.dot(mem_o, w_ref[1024:1280, :], preferred_element_type=F32))
    xm = x_ref[...] + y
    ms = jnp.mean(xm * xm, axis=-1, keepdims=True)
    return xm, xm * lax.rsqrt(ms + EPS) * fg_ref[...]


def _out_proj_dense_kernel(ret_ref, fox_ref, mq_ref, mkt_ref, mvs_ref, bd_ref, gm_ref, w_ref, x_ref, fg_ref,
                           xm_ref, h_ref):
    xm, hn = _out_proj_body(ret_ref, fox_ref, mq_ref, mkt_ref, mvs_ref, bd_ref, gm_ref, w_ref, x_ref, fg_ref)
    xm_ref[...] = xm
    h_ref[...] = hn.astype(BF16)


def _out_proj_moe_kernel(ret_ref, fox_ref, mq_ref, mkt_ref, mvs_ref, bd_ref, gm_ref, w_ref, x_ref, fg_ref,
                         rw_ref, tri_ref, xm_ref, h_ref, route_ref, cnt_ref, carry_ref):
    @pl.when(pl.program_id(0) == 0)
    def _():
        carry_ref[...] = jnp.zeros_like(carry_ref)

    xm, hn = _out_proj_body(ret_ref, fox_ref, mq_ref, mkt_ref, mvs_ref, bd_ref, gm_ref, w_ref, x_ref, fg_ref)
    xm_ref[...] = xm
    h_ref[...] = hn
    logits = jnp.dot(hn, rw_ref[...], preferred_element_type=F32, precision=lax.Precision.HIGHEST)
    lane = lax.broadcasted_iota(jnp.int32, logits.shape, 1)
    lanef = lane.astype(F32)
    lg = jnp.where(lane < N_EXPERTS, logits, NEG)
    m1 = jnp.max(lg, axis=-1, keepdims=True)
    i1 = jnp.min(jnp.where(lg == m1, lanef, float(LANE)), axis=-1, keepdims=True)
    oh1 = lanef == i1
    lg2 = jnp.where(oh1, NEG, lg)
    m2 = jnp.max(lg2, axis=-1, keepdims=True)
    i2 = jnp.min(jnp.where(lg2 == m2, lanef, float(LANE)), axis=-1, keepdims=True)
    oh2 = lanef == i2
    e = jnp.exp(m2 - m1)
    g1 = 1.0 / (1.0 + e)
    g2 = e / (1.0 + e)
    oh = jnp.where(oh1 | oh2, 1.0, 0.0)
    before = carry_ref[0:1, :] + jnp.dot(tri_ref[...], oh.astype(BF16), preferred_element_type=F32)
    r1 = jnp.sum(jnp.where(oh1, before, 0.0), axis=-1, keepdims=True)
    r2 = jnp.sum(jnp.where(oh2, before, 0.0), axis=-1, keepdims=True)
    tm = oh.shape[0]
    total = before[tm - 1:tm, :] + oh[tm - 1:tm, :]
    carry_ref[0:1, :] = total
    cnt_ref[...] = jnp.broadcast_to(total, cnt_ref.shape)
    route = jnp.zeros_like(logits)
    for col, val in enumerate((i1, i2, r1, r2, g1, g2)):
        route = jnp.where(lane == col, val, route)
    route_ref[...] = route


def _out_proj(ret_o, fox_o, mq, mkt, mvs, bd, g_mem, w_out, x, ffn_g, layer, seq, router=None):
    T = x.shape[0]
    tm = TOK_TILE
    tps = seq // tm
    row = lambda i: (i, 0)
    fix = lambda i: (0, 0)
    full = lambda a: pl.BlockSpec(a.shape, fix)
    M4 = mkt.shape[-1]
    in_specs = [pl.BlockSpec((tm, 512), row), pl.BlockSpec((tm, 512), row), pl.BlockSpec((tm, 256), row),
                pl.BlockSpec((1, 1, 256, M4), lambda i: (layer, i // tps, 0, 0)),
                pl.BlockSpec((1, 1, M4, 256), lambda i: (layer, i // tps, 0, 0)),
                full(bd), full(g_mem), full(w_out), pl.BlockSpec((tm, D_MODEL), row), full(ffn_g)]
    args = [ret_o, fox_o, mq, mkt, mvs, bd, g_mem, w_out, x, ffn_g]
    params = pltpu.CompilerParams(dimension_semantics=("arbitrary",), vmem_limit_bytes=VMEM_LIMIT)
    if router is None:
        return pl.pallas_call(
            _out_proj_dense_kernel, grid=(T // tm,), in_specs=in_specs,
            out_specs=[pl.BlockSpec((tm, D_MODEL), row), pl.BlockSpec((tm, D_MODEL), row)],
            out_shape=[jax.ShapeDtypeStruct((T, D_MODEL), F32), jax.ShapeDtypeStruct((T, D_MODEL), BF16)],
            compiler_params=params, name="out_proj_dense",
        )(*args)
    rw, tri = router
    return pl.pallas_call(
        _out_proj_moe_kernel, grid=(T // tm,), in_specs=in_specs + [full(rw), full(tri)],
        out_specs=[pl.BlockSpec((tm, D_MODEL), row), pl.BlockSpec((tm, D_MODEL), row),
                   pl.BlockSpec((tm, LANE), row), pl.BlockSpec((8, LANE), fix)],
        out_shape=[jax.ShapeDtypeStruct((T, D_MODEL), F32), jax.ShapeDtypeStruct((T, D_MODEL), F32),
                   jax.ShapeDtypeStruct((T, LANE), F32), jax.ShapeDtypeStruct((8, LANE), F32)],
        scratch_shapes=[pltpu.VMEM((8, LANE), F32)],
        compiler_params=params, name="out_proj_moe",
    )(*args, rw, tri)


def _dense_ffn_kernel(h_ref, wg_ref, wu_ref, wd_ref, x_ref, o_ref, acc_ref):
    j = pl.program_id(1)

    @pl.when(j == 0)
    def _():
        acc_ref[...] = x_ref[...]

    h = h_ref[...]
    g = jnp.dot(h, wg_ref[0], preferred_element_type=F32)
    u = jnp.dot(h, wu_ref[0], preferred_element_type=F32)
    a = (_silu(g) * u).astype(BF16)
    acc_ref[...] += jnp.dot(a, wd_ref[0], preferred_element_type=F32)

    @pl.when(j == pl.num_programs(1) - 1)
    def _():
        o_ref[...] = acc_ref[...]


def _dense_ffn(h, wg, wu, wd, x, layer):
    T = h.shape[0]
    tm = TOK_TILE
    nf = D_FF_PAD // FF_TILE
    return pl.pallas_call(
        _dense_ffn_kernel,
        grid=(T // tm, nf),
        in_specs=[pl.BlockSpec((tm, D_MODEL), lambda i, j: (i, 0)),
                  pl.BlockSpec((1, D_MODEL, FF_TILE), lambda i, j: (layer, 0, j)),
                  pl.BlockSpec((1, D_MODEL, FF_TILE), lambda i, j: (layer, 0, j)),
                  pl.BlockSpec((1, FF_TILE, D_MODEL), lambda i, j: (layer, j, 0)),
                  pl.BlockSpec((tm, D_MODEL), lambda i, j: (i, 0))],
        out_specs=pl.BlockSpec((tm, D_MODEL), lambda i, j: (i, 0)),
        out_shape=jax.ShapeDtypeStruct((T, D_MODEL), F32),
        scratch_shapes=[pltpu.VMEM((tm, D_MODEL), F32)],
        compiler_params=pltpu.CompilerParams(dimension_semantics=("arbitrary", "arbitrary"),
                                             vmem_limit_bytes=VMEM_LIMIT),
        name="dense_ffn",
    )(h, wg, wu, wd, x)


def _row_copy(src, s, dst, d, sem):
    return pltpu.make_async_copy(src.at[pl.ds(s, 1)], dst.at[pl.ds(d, 1)], sem)


def _moe_ffn_kernel(te_ref, nu_ref, src_ref, h_hbm, wg_ref, wu_ref, wd_ref, o_ref, x_ref, acc_ref, sem):
    i = pl.program_id(0)
    j = pl.program_id(1)
    n_used = nu_ref[0]
    tm = x_ref.shape[1]
    slot = i % 2

    def gather(tile, into):
        base = tile * tm

        def issue(r, carry):
            _row_copy(h_hbm, src_ref[base + r], x_ref.at[into], r, sem.at[into]).start()
            return carry

        lax.fori_loop(0, tm, issue, 0, unroll=8)

    @pl.when((i == 0) & (j == 0))
    def _():
        gather(0, 0)

    @pl.when((j == 0) & (i < n_used))
    def _():
        pltpu.make_async_copy(h_hbm.at[pl.ds(0, tm)], x_ref.at[slot], sem.at[slot]).wait()

    @pl.when((j == 0) & (i + 1 < n_used))
    def _():
        gather(i + 1, 1 - slot)

    @pl.when(i < n_used)
    def _():
        x = x_ref[slot].astype(BF16)
        g = jnp.dot(x, wg_ref[0, 0], preferred_element_type=F32)
        u = jnp.dot(x, wu_ref[0, 0], preferred_element_type=F32)
        a = (_silu(g) * u).astype(BF16)
        part = jnp.dot(a, wd_ref[0, 0], preferred_element_type=F32)

        @pl.when(j == 0)
        def _():
            acc_ref[...] = part

        @pl.when(j > 0)
        def _():
            acc_ref[...] += part

        @pl.when(j == pl.num_programs(1) - 1)
        def _():
            o_ref[...] = acc_ref[...]

    @pl.when(i >= n_used)
    def _():
        o_ref[...] = jnp.zeros_like(o_ref)


def _moe_ffn(h, src, tile_expert, n_used, wg, wu, wd, layer):
    rows = src.shape[0]
    tm = MOE_TILE
    nf = D_FF_PAD // FF_TILE
    last = nf - 1

    def w_cols(i, j, te, nu, sr):
        return (layer, te[i], 0, jnp.where(i < nu[0], j, last))

    def w_rows(i, j, te, nu, sr):
        return (layer, te[i], jnp.where(i < nu[0], j, last), 0)

    return pl.pallas_call(
        _moe_ffn_kernel,
        grid_spec=pltpu.PrefetchScalarGridSpec(
            num_scalar_prefetch=3, grid=(rows // tm, nf),
            in_specs=[pl.BlockSpec(memory_space=pl.ANY),
                      pl.BlockSpec((1, 1, D_MODEL, FF_TILE), w_cols),
                      pl.BlockSpec((1, 1, D_MODEL, FF_TILE), w_cols),
                      pl.BlockSpec((1, 1, FF_TILE, D_MODEL), w_rows)],
            out_specs=pl.BlockSpec((tm, D_MODEL), lambda i, j, te, nu, sr: (i, 0)),
            scratch_shapes=[pltpu.VMEM((2, tm, D_MODEL), F32), pltpu.VMEM((tm, D_MODEL), F32),
                            pltpu.SemaphoreType.DMA((2,))]),
        out_shape=jax.ShapeDtypeStruct((rows, D_MODEL), F32),
        compiler_params=pltpu.CompilerParams(dimension_semantics=("arbitrary", "arbitrary"),
                                             vmem_limit_bytes=VMEM_LIMIT),
        name="moe_ffn",
    )(tile_expert, n_used, src, h, wg, wu, wd)


def _combine_kernel(tc, d1_ref, d2_ref, o_hbm, x_ref, route_ref, out_ref, b1_ref, b2_ref, sem):
    base = pl.program_id(0) * tc

    def issue(r, carry):
        t = base + r
        _row_copy(o_hbm, d1_ref[t], b1_ref, r, sem.at[0]).start()
        _row_copy(o_hbm, d2_ref[t], b2_ref, r, sem.at[1]).start()
        return carry

    lax.fori_loop(0, tc, issue, 0, unroll=8)
    pltpu.make_async_copy(o_hbm.at[pl.ds(0, tc)], b1_ref, sem.at[0]).wait()
    pltpu.make_async_copy(o_hbm.at[pl.ds(0, tc)], b2_ref, sem.at[1]).wait()
    route = route_ref[...]
    g1 = route[:, 4:5]
    g2 = route[:, 5:6]
    out_ref[...] = x_ref[...] + g1 * b1_ref[...] + g2 * b2_ref[...]


def _combine(o, d1, d2, x, route):
    T = x.shape[0]
    tc = 256
    return pl.pallas_call(
        functools.partial(_combine_kernel, tc),
        grid_spec=pltpu.PrefetchScalarGridSpec(
            num_scalar_prefetch=2, grid=(T // tc,),
            in_specs=[pl.BlockSpec(memory_space=pl.ANY),
                      pl.BlockSpec((tc, D_MODEL), lambda i, a, b: (i, 0)),
                      pl.BlockSpec((tc, LANE), lambda i, a, b: (i, 0))],
            out_specs=pl.BlockSpec((tc, D_MODEL), lambda i, a, b: (i, 0)),
            scratch_shapes=[pltpu.VMEM((tc, D_MODEL), F32), pltpu.VMEM((tc, D_MODEL), F32),
                            pltpu.SemaphoreType.DMA((2,))]),
        out_shape=jax.ShapeDtypeStruct((T, D_MODEL), F32),
        compiler_params=pltpu.CompilerParams(dimension_semantics=("arbitrary",),
                                             vmem_limit_bytes=VMEM_LIMIT),
        name="moe_combine",
    )(d1, d2, o, x, route)


def _moe_layer(h, xm, route, counts, wg, wu, wd, layer):
    T = h.shape[0]
    tm = MOE_TILE
    n_tiles = (2 * T) // tm + N_EXPERTS
    e1 = route[:, 0].astype(jnp.int32)
    e2 = route[:, 1].astype(jnp.int32)
    r1 = route[:, 2].astype(jnp.int32)
    r2 = route[:, 3].astype(jnp.int32)
    cnt = counts[0, :N_EXPERTS].astype(jnp.int32)
    tiles = (cnt + tm - 1) // tm
    tile_end = jnp.cumsum(tiles)
    offs = (tile_end - tiles) * tm
    d1 = offs[e1] + r1
    d2 = offs[e2] + r2
    n_used = tile_end[-1:]
    tile_id = jnp.arange(n_tiles, dtype=jnp.int32)
    tile_expert = jnp.sum(tile_id[:, None] >= tile_end[None, :], axis=1).astype(jnp.int32)
    tile_expert = jnp.minimum(tile_expert, tile_expert[jnp.maximum(n_used[0] - 1, 0)])
    tok = jnp.arange(T, dtype=jnp.int32)
    src = jnp.zeros((n_tiles * tm,), jnp.int32)
    src = src.at[d1].set(tok, unique_indices=True).at[d2].set(tok, unique_indices=True)
    o = _moe_ffn(h, src, tile_expert, n_used.astype(jnp.int32), wg, wu, wd, layer)
    return _combine(o, d1, d2, xm, route)


def _constants(seq):
    tm = TOK_TILE
    r = np.arange(tm)
    tri_incl = (r[:, None] >= r[None, :]).astype(np.float32)
    tri_strict = (r[:, None] > r[None, :]).astype(np.float32)
    d = np.arange(256)
    bd = (d[:, None] // 64 == d[None, :] // 64).astype(np.float32) / 64.0
    pq = np.zeros((N_PIECE * LANE, FOX_HEADS * LANE), np.float32)
    pk = np.zeros_like(pq)
    oneq = np.zeros((1, LANE), np.float32)
    onek = np.zeros((1, LANE), np.float32)
    for p in range(N_PIECE):
        oneq[0, C_LANE + N_PIECE + p] = 1.0
        onek[0, C_LANE + p] = 1.0
        for h in range(FOX_HEADS):
            pq[p * LANE + h, h * LANE + C_LANE + p] = 1.0
            pk[p * LANE + h, h * LANE + C_LANE + N_PIECE + p] = -1.0
    in_consts = (jnp.asarray(tri_incl, BF16), jnp.asarray(bd, BF16), jnp.asarray(pq, BF16),
                 jnp.asarray(pk, BF16), jnp.asarray(oneq), jnp.asarray(onek))

    half = RET_DK // 2
    pos = jnp.arange(seq, dtype=F32)
    inv_freq = ROPE_BASE ** (-jnp.arange(half, dtype=F32) / half)
    ang = pos[:, None] * inv_freq[None, :]
    cos = jnp.tile(jnp.cos(ang), (1, RET_HEADS))
    sin = jnp.tile(jnp.sin(ang), (1, RET_HEADS))

    C = RET_CHUNK
    hh = jnp.arange(RET_HEADS, dtype=F32)
    log_g = jnp.log(1.0 - 2.0 ** (-5.0 - hh))
    idx = jnp.arange(C, dtype=F32)
    diff = idx[:, None] - idx[None, :]
    dmat = jnp.where(diff >= 0, jnp.exp(log_g[:, None, None] * jnp.maximum(diff, 0.0)), 0.0)
    dstack = dmat.reshape(RET_HEADS * C, C)
    k_head = (np.arange(256) % LANE) // 32
    v_head = np.arange(RET_HEADS * RET_DV) // RET_DV
    qd = jnp.exp(log_g[k_head][None, :] * (idx[:, None] + 1.0))
    kd = jnp.exp(log_g[k_head][None, :] * (C - 1.0 - idx[:, None]))
    cd = jnp.exp(log_g[v_head] * C)[None, :]
    bm = jnp.asarray((k_head[:, None] == v_head[None, :]).astype(np.float32))
    hm = jnp.asarray((np.arange(RET_HEADS)[:, None] == k_head[None, :]).astype(np.float32), BF16)
    ret_tables = (hm, dstack, qd, kd, cd, bm)
    return in_consts, cos, sin, ret_tables, jnp.asarray(tri_strict, BF16)


def _pad_heads(g):
    depth = g.shape[0]
    g = g.reshape(depth, FOX_HEADS, FOX_DH)
    return jnp.pad(g, ((0, 0), (0, 0), (0, LANE - FOX_DH))).reshape(depth, 1, FOX_HEADS * LANE)


def kernel(x, mem, attn_norm_g, w_in, forget_bias, fox_q_norm_g, fox_k_norm_g, mem_q_norm_g, mem_k_norm_g,
           mem_norm_g, w_mem_kv, out_norm_g, w_out, ffn_norm_g, dense_w_gate, dense_w_up, dense_w_down,
           router_w, expert_w_gate, expert_w_up, expert_w_down):
    B, S, D = x.shape
    depth = w_in.shape[0]
    T = B * S
    in_consts, cos, sin, ret_tables, tri_strict = _constants(S)
    bd = in_consts[1]

    cols = _in_proj_columns()
    w_in_z = jnp.concatenate([w_in, jnp.zeros((depth, D, 1), w_in.dtype)], axis=-1)
    w_in_p = jnp.take(w_in_z, jnp.asarray(np.where(cols < 0, w_in.shape[-1], cols)), axis=-1).astype(BF16)
    lane_pad = lambda g: jnp.pad(g, ((0, 0), (0, LANE - g.shape[-1])))[:, None, :]
    fqg = lane_pad(fox_q_norm_g)
    fkg = lane_pad(fox_k_norm_g)
    fb = lane_pad(forget_bias)
    mqg = jnp.tile(mem_q_norm_g, (1, MEM_HEADS))[:, None, :]
    mkg = jnp.tile(mem_k_norm_g, (1, MEM_HEADS))[:, None, :]
    n_ret = RET_HEADS * RET_DV
    n_fox = FOX_HEADS * FOX_DH
    g_ret = out_norm_g[:, None, :n_ret]
    g_fox = _pad_heads(out_norm_g[:, n_ret:n_ret + n_fox])
    g_mem = out_norm_g[:, None, n_ret + n_fox:]
    w_fox = w_out[:, n_ret:n_ret + n_fox].reshape(depth, FOX_HEADS, FOX_DH, D)
    w_fox = jnp.pad(w_fox, ((0, 0), (0, 0), (0, LANE - FOX_DH), (0, 0))).reshape(depth, FOX_HEADS * LANE, D)
    w_out_p = jnp.concatenate([w_out[:, :n_ret], w_fox, w_out[:, n_ret + n_fox:]], axis=1).astype(BF16)
    ffpad = D_FF_PAD - D_FF
    pad_cols = lambda w: jnp.pad(w, [(0, 0)] * (w.ndim - 1) + [(0, ffpad)]).astype(BF16)
    pad_rows = lambda w: jnp.pad(w, [(0, 0)] * (w.ndim - 2) + [(0, ffpad), (0, 0)]).astype(BF16)
    dwg, dwu, dwd = pad_cols(dense_w_gate), pad_cols(dense_w_up), pad_rows(dense_w_down)
    ewg, ewu, ewd = pad_cols(expert_w_gate), pad_cols(expert_w_up), pad_rows(expert_w_down)
    rw = jnp.pad(router_w, ((0, 0), (0, 0), (0, LANE - N_EXPERTS)))

    mkt, mvs = _mem_prep(mem, mem_norm_g[:, None, :], w_mem_kv.astype(BF16), mkg, bd)

    xt = x.reshape(T, D)
    for l in range(depth):
        rq, rk, rv, rg, fqt, fka, fvt, mq = _in_proj(
            xt, attn_norm_g[l][None], w_in_p[l], cos, sin, fqg[l], fkg[l], mqg[l], fb[l], in_consts, S)
        ret_o = _retention(rq, rk, rv, rg, ret_tables, g_ret[l], B, S)
        fox_o = _fox(fqt, fka, fvt, g_fox[l], B, S)
        j = l // 2
        if l % 2 == 0:
            xm, h2 = _out_proj(ret_o, fox_o, mq, mkt, mvs, bd, g_mem[l], w_out_p[l], xt, ffn_norm_g[l][None], l, S)
            xt = _dense_ffn(h2, dwg, dwu, dwd, xm, j)
        else:
            xm, h2, route, counts = _out_proj(ret_o, fox_o, mq, mkt, mvs, bd, g_mem[l], w_out_p[l], xt,
                                              ffn_norm_g[l][None], l, S, router=(rw[j], tri_strict))
            xt = _moe_layer(h2, xm, route, counts, ewg, ewu, ewd, j)
    return xt.reshape(B, S, D)
```

```python
import functools

import numpy as np
import jax
import jax.numpy as jnp
from jax import lax
from jax.experimental import pallas as pl
from jax.experimental.pallas import tpu as pltpu

F32 = jnp.float32
BF16 = jnp.bfloat16

D_MODEL = 1024
RET_HEADS = 4
RET_DK = 64
RET_DV = 128
FOX_HEADS = 4
FOX_DH = 64
MEM_HEADS = 4
MEM_DH = 64
RET_CHUNK = 128
D_FF = 2752
N_EXPERTS = 8
ROPE_BASE = 10000.0
EPS = 1e-6

LANE = 128
D_FF_PAD = 2816
FF_TILE = D_FF_PAD // 2
TOK_TILE = 512
FOX_Q_TILE = 1024
FOX_K_TILE = 512
LOG2E = 1.4426950408889634
MOE_TILE = 512
VMEM_LIMIT = 56 * 1024 * 1024
NEG = -1e30

RQ, RK, RV, RG = 0, 256, 512, 1024
FQ, FK, FV = 1536, 2048, 2560
MQ, FF_COL, P_IN = 3072, 3328, 3456
C_LANE = 64
N_PIECE = 3
V_ONE = 64


def _in_proj_columns():
    cols = []
    for base in (0, 256):
        for half in range(2):
            for h in range(RET_HEADS):
                cols += [base + h * RET_DK + half * 32 + i for i in range(32)]
    cols += list(range(512, 1536))
    for base in (1536, 1792, 2048):
        for h in range(FOX_HEADS):
            cols += [base + h * FOX_DH + i for i in range(FOX_DH)] + [-1] * (LANE - FOX_DH)
    cols += list(range(2308, 2564))
    cols += list(range(2304, 2308)) + [-1] * (LANE - FOX_HEADS)
    assert len(cols) == P_IN
    return np.array(cols, np.int32)


def _split3(x):
    hi = x.astype(BF16)
    r1 = x - hi.astype(F32)
    mid = r1.astype(BF16)
    lo = (r1 - mid.astype(F32)).astype(BF16)
    return hi, mid, lo


def _group_mean_sq(x, bd):
    sq = x * x
    hi = sq.astype(BF16)
    lo = (sq - hi.astype(F32)).astype(BF16)
    return (jnp.dot(hi, bd, preferred_element_type=F32)
            + jnp.dot(lo, bd, preferred_element_type=F32))


def _silu(x):
    return x / (1.0 + jnp.exp(-x))


def _in_proj_kernel(tiles_per_seq, x_ref, g_ref, w_ref, cos_ref, sin_ref, fqg_ref, fkg_ref, mqg_ref,
                    fb_ref, tri_ref, bd_ref, pq_ref, pk_ref, oneq_ref, onek_ref,
                    rq_ref, rk_ref, rv_ref, rg_ref, fqt_ref, fka_ref, fvt_ref, mq_ref, carry_ref):
    i = pl.program_id(0)

    @pl.when(i % tiles_per_seq == 0)
    def _():
        carry_ref[...] = jnp.zeros_like(carry_ref)

    x = x_ref[...]
    ms = jnp.mean(x * x, axis=-1, keepdims=True)
    h = (x * lax.rsqrt(ms + EPS) * g_ref[...]).astype(BF16)

    def proj(a, b):
        return jnp.dot(h, w_ref[:, a:b], preferred_element_type=F32)

    cos = cos_ref[...]
    sin = sin_ref[...]
    for base, ref, scale in ((RQ, rq_ref, RET_DK ** -0.5), (RK, rk_ref, 1.0)):
        x1 = proj(base, base + LANE)
        x2 = proj(base + LANE, base + 2 * LANE)
        ref[:, 0:LANE] = ((x1 * cos - x2 * sin) * scale).astype(BF16)
        ref[:, LANE:2 * LANE] = ((x2 * cos + x1 * sin) * scale).astype(BF16)
    rv_ref[...] = proj(RV, RV + 512).astype(BF16)
    rg_ref[...] = proj(RG, RG + 512).astype(BF16)
    vt = proj(FV, FV + 512).T
    vrow = lax.broadcasted_iota(jnp.int32, vt.shape, 0) % LANE
    fvt_ref[...] = jnp.where(vrow == V_ONE, 1.0, vt).astype(BF16)

    mq = proj(MQ, MQ + 256)
    mq = mq * lax.rsqrt(_group_mean_sq(mq, bd_ref[...]) + EPS) * mqg_ref[...] * (MEM_DH ** -0.5)
    mq_ref[...] = mq.astype(BF16)

    z = proj(FF_COL, FF_COL + LANE) + fb_ref[...]
    logf = jnp.minimum(z, 0.0) - jnp.log(1.0 + jnp.exp(-jnp.abs(z)))
    lane = lax.broadcasted_iota(jnp.int32, logf.shape, 1)
    logf = jnp.where(lane < FOX_HEADS, logf, 0.0)
    tri = tri_ref[...]
    c = carry_ref[0:1, :]
    for piece in _split3(logf):
        c = c + jnp.dot(tri, piece, preferred_element_type=F32)
    tm = c.shape[0]
    carry_ref[0:1, :] = c[tm - 1:tm, :]

    cp = jnp.concatenate(_split3(c * LOG2E), axis=1)
    extra_q = jnp.dot(cp, pq_ref[...], preferred_element_type=F32)
    extra_k = jnp.dot(cp, pk_ref[...], preferred_element_type=F32)
    fq = proj(FQ, FQ + 512)
    fk = proj(FK, FK + 512)
    q_tiles = []
    for hh in range(FOX_HEADS):
        sl = slice(hh * LANE, (hh + 1) * LANE)
        tiles = []
        for t, gref, scale, extra, one in ((fq[:, sl], fqg_ref, FOX_DH ** -0.5 * LOG2E, extra_q[:, sl], oneq_ref),
                                           (fk[:, sl], fkg_ref, 1.0, extra_k[:, sl], onek_ref)):
            msq = jnp.sum(t * t, axis=-1, keepdims=True) * (1.0 / FOX_DH)
            tn = t * lax.rsqrt(msq + EPS) * gref[...] * scale
            tiles.append(tn + extra + one[...])
        q_tiles.append(tiles[0])
        fka_ref[:, sl] = tiles[1].astype(BF16)
    fqt_ref[...] = jnp.concatenate(q_tiles, axis=1).T.astype(BF16)


def _in_proj(x, g, w, cos, sin, fqg, fkg, mqg, fb, consts, seq):
    T = x.shape[0]
    tm = TOK_TILE
    nt = T // tm
    tps = seq // tm
    row = lambda i: (i, 0)
    fix = lambda i: (0, 0)
    full = lambda a: pl.BlockSpec(a.shape, fix)
    tri, bd, pq, pk, oneq, onek = consts
    widths = (256, 256, 512, 512, -512, 512, -512, 256)
    outs = [jax.ShapeDtypeStruct((T, n) if n > 0 else (-n, T), BF16) for n in widths]
    out_specs = [pl.BlockSpec((tm, n), row) if n > 0 else pl.BlockSpec((-n, tm), lambda i: (0, i))
                 for n in widths]
    return pl.pallas_call(
        functools.partial(_in_proj_kernel, tps),
        grid=(nt,),
        in_specs=[pl.BlockSpec((tm, D_MODEL), row), full(g), full(w),
                  pl.BlockSpec((tm, LANE), lambda i: (i % tps, 0)),
                  pl.BlockSpec((tm, LANE), lambda i: (i % tps, 0)),
                  full(fqg), full(fkg), full(mqg), full(fb), full(tri), full(bd), full(pq), full(pk),
                  full(oneq), full(onek)],
        out_specs=out_specs,
        out_shape=outs,
        scratch_shapes=[pltpu.VMEM((8, LANE), F32)],
        compiler_params=pltpu.CompilerParams(dimension_semantics=("arbitrary",),
                                             vmem_limit_bytes=VMEM_LIMIT),
        name="in_proj",
    )(x, g, w, cos, sin, fqg, fkg, mqg, fb, tri, bd, pq, pk, oneq, onek)


def _retention_kernel(chunks, rq_ref, rk_ref, rv_ref, rg_ref, hm_ref, dst_ref, qd_ref, kd_ref, cd_ref,
                      bm_ref, g_ref, o_ref, state_ref):
    @pl.when(pl.program_id(1) == 0)
    def _():
        state_ref[...] = jnp.zeros_like(state_ref)

    C = RET_CHUNK
    for ci in range(chunks):
        rows = slice(ci * C, (ci + 1) * C)
        qc = rq_ref[rows, :]
        kc = rk_ref[rows, :]
        vc = rv_ref[rows, :]
        qs = jnp.concatenate([qc * hm_ref[hh:hh + 1, :] for hh in range(RET_HEADS)], axis=0)
        sc = lax.dot_general(qs, kc, (((1,), (1,)), ((), ())), preferred_element_type=F32)
        sc = (sc * dst_ref[...]).astype(BF16)
        state = state_ref[...]
        qdec = (qc.astype(F32) * qd_ref[...]).astype(BF16)
        cross = jnp.dot(qdec, state.astype(BF16), preferred_element_type=F32)
        kdec = (kc.astype(F32) * kd_ref[...]).astype(BF16)
        kv = lax.dot_general(kdec, vc, (((0,), (0,)), ((), ())), preferred_element_type=F32)
        state_ref[...] = state * cd_ref[...] + kv * bm_ref[...]
        for hh in range(RET_HEADS):
            cols = slice(hh * RET_DV, (hh + 1) * RET_DV)
            o = jnp.dot(sc[hh * C:(hh + 1) * C, :], vc[:, cols], preferred_element_type=F32) + cross[:, cols]
            msq = jnp.mean(o * o, axis=-1, keepdims=True)
            y = o * lax.rsqrt(msq + EPS) * g_ref[:, cols]
            o_ref[rows, cols] = (y * _silu(rg_ref[rows, cols].astype(F32))).astype(BF16)


def _retention(rq, rk, rv, rg, tables, g_ret, batch, seq):
    T = rq.shape[0]
    tm = TOK_TILE
    tps = seq // tm
    row = lambda b, i: (b * tps + i, 0)
    fix = lambda b, i: (0, 0)
    full = lambda a: pl.BlockSpec(a.shape, fix)
    return pl.pallas_call(
        functools.partial(_retention_kernel, tm // RET_CHUNK),
        grid=(batch, tps),
        in_specs=[pl.BlockSpec((tm, 256), row), pl.BlockSpec((tm, 256), row),
                  pl.BlockSpec((tm, 512), row), pl.BlockSpec((tm, 512), row)]
                 + [full(t) for t in tables] + [full(g_ret)],
        out_specs=pl.BlockSpec((tm, 512), row),
        out_shape=jax.ShapeDtypeStruct((T, 512), BF16),
        scratch_shapes=[pltpu.VMEM((RET_HEADS * RET_DK, RET_HEADS * RET_DV), F32)],
        compiler_params=pltpu.CompilerParams(dimension_semantics=("arbitrary", "arbitrary"),
                                             vmem_limit_bytes=VMEM_LIMIT),
        name="retention",
    )(rq, rk, rv, rg, *tables, g_ret)


def _fox_kernel(tq, tk, qt_ref, k_ref, vt_ref, g_ref, o_ref, m_ref, acc_ref, s0_ref, s1_ref):
    i = pl.program_id(2)
    qt = qt_ref[...]
    m_ref[...] = jnp.full_like(m_ref, NEG)
    acc_ref[...] = jnp.zeros_like(acc_ref)
    assert tq == 2 * tk
    s_refs = (s0_ref, s1_ref)

    def scores(j, slot):
        start = pl.multiple_of(j * tk, tk)
        s_refs[slot][...] = jnp.dot(k_ref[pl.ds(start, tk), :], qt, preferred_element_type=F32)

    def update(j, slot, diag):
        start = pl.multiple_of(j * tk, tk)
        st = s_refs[slot][...]
        if diag is not None:
            key = lax.broadcasted_iota(jnp.int32, st.shape, 0) + diag * tk
            qry = lax.broadcasted_iota(jnp.int32, st.shape, 1)
            st = jnp.where(key <= qry, st, NEG)
        m_prev = m_ref[...]
        m_new = jnp.maximum(m_prev, jnp.max(st, axis=0, keepdims=True))
        alpha = jnp.exp2(m_prev - m_new)
        p = jnp.exp2(st - m_new).astype(BF16)
        vt = vt_ref[:, pl.ds(start, tk)]
        acc_ref[...] = alpha * acc_ref[...] + jnp.dot(vt, p, preferred_element_type=F32)
        m_ref[...] = m_new

    scores(0, 0)

    def body(t, carry):
        scores(2 * t + 1, 1)
        update(2 * t, 0, None)
        scores(2 * t + 2, 0)
        update(2 * t + 1, 1, None)
        return carry

    lax.fori_loop(0, i, body, 0)
    scores(2 * i + 1, 1)
    update(2 * i, 0, 0)
    update(2 * i + 1, 1, 1)
    acc = acc_ref[...]
    row = lax.broadcasted_iota(jnp.int32, acc.shape, 0)
    o = jnp.where(row < FOX_DH, acc / acc[V_ONE:V_ONE + 1, :], 0.0)
    msq = jnp.sum(o * o, axis=0, keepdims=True) * (1.0 / FOX_DH)
    o_ref[...] = ((o * lax.rsqrt(msq + EPS)).T * g_ref[...]).astype(BF16)


def _fox(fqt, fka, fvt, g_fox, batch, seq):
    T = fka.shape[0]
    tq = FOX_Q_TILE
    tk = FOX_K_TILE
    nq = seq // tq
    return pl.pallas_call(
        functools.partial(_fox_kernel, tq, tk),
        grid=(batch, FOX_HEADS, nq),
        in_specs=[pl.BlockSpec((LANE, tq), lambda b, h, i: (h, b * nq + i)),
                  pl.BlockSpec((seq, LANE), lambda b, h, i: (b, h)),
                  pl.BlockSpec((LANE, seq), lambda b, h, i: (h, b)),
                  pl.BlockSpec((1, LANE), lambda b, h, i: (0, h))],
        out_specs=pl.BlockSpec((tq, LANE), lambda b, h, i: (b * nq + i, h)),
        out_shape=jax.ShapeDtypeStruct((T, FOX_HEADS * LANE), BF16),
        scratch_shapes=[pltpu.VMEM((1, tq), F32), pltpu.VMEM((LANE, tq), F32),
                        pltpu.VMEM((tk, tq), F32), pltpu.VMEM((tk, tq), F32)],
        compiler_params=pltpu.CompilerParams(dimension_semantics=("arbitrary", "arbitrary", "arbitrary"),
                                             vmem_limit_bytes=VMEM_LIMIT),
        name="fox",
    )(fqt, fka, fvt, g_fox)


def _mem_prep_kernel(mem_ref, g_ref, w_ref, kg_ref, bd_ref, mkt_ref, mvs_ref):
    x = mem_ref[0]
    ms = jnp.mean(x * x, axis=-1, keepdims=True)
    mn = (x * lax.rsqrt(ms + EPS) * g_ref[0]).astype(BF16)
    mkv = jnp.dot(mn, w_ref[0], preferred_element_type=F32)
    width = MEM_HEADS * MEM_DH
    mk = mkv[:, :width]
    mv = mkv[:, width:]
    mk = mk * lax.rsqrt(_group_mean_sq(mk, bd_ref[...]) + EPS) * kg_ref[0]
    mkt = mk.T
    feat = lax.broadcasted_iota(jnp.int32, mkt.shape, 0) // MEM_DH
    lane_head = lax.broadcasted_iota(jnp.int32, mv.shape, 1) // MEM_DH
    M = x.shape[0]
    for hh in range(MEM_HEADS):
        mkt_ref[0, 0, :, hh * M:(hh + 1) * M] = jnp.where(feat == hh, mkt, 0.0).astype(BF16)
        mvs_ref[0, 0, hh * M:(hh + 1) * M, :] = jnp.where(lane_head == hh, mv, 0.0).astype(BF16)


def _mem_prep(mem, mem_norm_g, w_mem_kv, mem_k_g, bd):
    B, M, _ = mem.shape
    depth = w_mem_kv.shape[0]
    width = MEM_HEADS * MEM_DH
    return pl.pallas_call(
        _mem_prep_kernel,
        grid=(depth, B),
        in_specs=[pl.BlockSpec((1, M, D_MODEL), lambda l, b: (b, 0, 0)),
                  pl.BlockSpec((1, 1, D_MODEL), lambda l, b: (l, 0, 0)),
                  pl.BlockSpec((1, D_MODEL, 2 * width), lambda l, b: (l, 0, 0)),
                  pl.BlockSpec((1, 1, width), lambda l, b: (l, 0, 0)),
                  pl.BlockSpec(bd.shape, lambda l, b: (0, 0))],
        out_specs=[pl.BlockSpec((1, 1, width, MEM_HEADS * M), lambda l, b: (l, b, 0, 0)),
                   pl.BlockSpec((1, 1, MEM_HEADS * M, width), lambda l, b: (l, b, 0, 0))],
        out_shape=[jax.ShapeDtypeStruct((depth, B, width, MEM_HEADS * M), BF16),
                   jax.ShapeDtypeStruct((depth, B, MEM_HEADS * M, width), BF16)],
        compiler_params=pltpu.CompilerParams(dimension_semantics=("arbitrary", "arbitrary"),
                                             vmem_limit_bytes=VMEM_LIMIT),
        name="mem_prep",
    )(mem, mem_norm_g, w_mem_kv, mem_k_g, bd)


def _out_proj_body(ret_ref, fox_ref, mq_ref, mkt_ref, mvs_ref, bd_ref, gm_ref, w_ref, x_ref, fg_ref):
    M = mkt_ref.shape[-1] // MEM_HEADS
    s = jnp.dot(mq_ref[...], mkt_ref[0, 0], preferred_element_type=F32)
    mem_o = None
    for hh in range(MEM_HEADS):
        sh = s[:, hh * M:(hh + 1) * M]
        p = jnp.exp(sh - jnp.max(sh, axis=-1, keepdims=True))
        p = (p / jnp.sum(p, axis=-1, keepdims=True)).astype(BF16)
        part = jnp.dot(p, mvs_ref[0, 0, hh * M:(hh + 1) * M, :], preferred_element_type=F32)
        mem_o = part if mem_o is None else mem_o + part
    mem_o = (mem_o * lax.rsqrt(_group_mean_sq(mem_o, bd_ref[...]) + EPS) * gm_ref[...]).astype(BF16)
    y = (jnp.dot(ret_ref[...], w_ref[0:512, :], preferred_element_type=F32)
         + jnp.dot(fox_ref[...], w_ref[512:1024, :], preferred_element_type=F32)
         + jnp.dot(mem_o, w_ref[1024:1280, :], preferred_element_type=F32))
    xm = x_ref[...] + y
    ms = jnp.mean(xm * xm, axis=-1, keepdims=True)
    return xm, xm * lax.rsqrt(ms + EPS) * fg_ref[...]


def _out_proj_dense_kernel(ret_ref, fox_ref, mq_ref, mkt_ref, mvs_ref, bd_ref, gm_ref, w_ref, x_ref, fg_ref,
                           xm_ref, h_ref):
    xm, hn = _out_proj_body(ret_ref, fox_ref, mq_ref, mkt_ref, mvs_ref, bd_ref, gm_ref, w_ref, x_ref, fg_ref)
    xm_ref[...] = xm
    h_ref[...] = hn.astype(BF16)


def _out_proj_moe_kernel(ret_ref, fox_ref, mq_ref, mkt_ref, mvs_ref, bd_ref, gm_ref, w_ref, x_ref, fg_ref,
                         rw_ref, tri_ref, xm_ref, h_ref, route_ref, cnt_ref, carry_ref):
    @pl.when(pl.program_id(0) == 0)
    def _():
        carry_ref[...] = jnp.zeros_like(carry_ref)

    xm, hn = _out_proj_body(ret_ref, fox_ref, mq_ref, mkt_ref, mvs_ref, bd_ref, gm_ref, w_ref, x_ref, fg_ref)
    xm_ref[...] = xm
    h_ref[...] = hn
    rw = rw_ref[...]
    h_hi = hn.astype(BF16)
    h_lo = (hn - h_hi.astype(F32)).astype(BF16)
    w_hi = rw.astype(BF16)
    w_lo = (rw - w_hi.astype(F32)).astype(BF16)
    logits = (jnp.dot(h_hi, w_hi, preferred_element_type=F32) + jnp.dot(h_lo, w_hi, preferred_element_type=F32)
              + jnp.dot(h_hi, w_lo, preferred_element_type=F32))
    lane = lax.broadcasted_iota(jnp.int32, logits.shape, 1)
    lanef = lane.astype(F32)
    lg = jnp.where(lane < N_EXPERTS, logits, NEG)
    m1 = jnp.max(lg, axis=-1, keepdims=True)
    i1 = jnp.min(jnp.where(lg == m1, lanef, float(LANE)), axis=-1, keepdims=True)
    oh1 = lanef == i1
    lg2 = jnp.where(oh1, NEG, lg)
    m2 = jnp.max(lg2, axis=-1, keepdims=True)
    i2 = jnp.min(jnp.where(lg2 == m2, lanef, float(LANE)), axis=-1, keepdims=True)
    oh2 = lanef == i2
    e = jnp.exp(m2 - m1)
    g1 = 1.0 / (1.0 + e)
    g2 = e / (1.0 + e)
    oh = jnp.where(oh1 | oh2, 1.0, 0.0)
    before = carry_ref[0:1, :] + jnp.dot(tri_ref[...], oh.astype(BF16), preferred_element_type=F32)
    r1 = jnp.sum(jnp.where(oh1, before, 0.0), axis=-1, keepdims=True)
    r2 = jnp.sum(jnp.where(oh2, before, 0.0), axis=-1, keepdims=True)
    tm = oh.shape[0]
    total = before[tm - 1:tm, :] + oh[tm - 1:tm, :]
    carry_ref[0:1, :] = total
    cnt_ref[...] = jnp.broadcast_to(total, cnt_ref.shape)
    route = jnp.zeros_like(logits)
    for col, val in enumerate((i1, i2, r1, r2, g1, g2)):
        route = jnp.where(lane == col, val, route)
    route_ref[...] = route


def _out_proj(ret_o, fox_o, mq, mkt, mvs, bd, g_mem, w_out, x, ffn_g, layer, seq, router=None):
    T = x.shape[0]
    tm = TOK_TILE
    tps = seq // tm
    row = lambda i: (i, 0)
    fix = lambda i: (0, 0)
    full = lambda a: pl.BlockSpec(a.shape, fix)
    M4 = mkt.shape[-1]
    in_specs = [pl.BlockSpec((tm, 512), row), pl.BlockSpec((tm, 512), row), pl.BlockSpec((tm, 256), row),
                pl.BlockSpec((1, 1, 256, M4), lambda i: (layer, i // tps, 0, 0)),
                pl.BlockSpec((1, 1, M4, 256), lambda i: (layer, i // tps, 0, 0)),
                full(bd), full(g_mem), full(w_out), pl.BlockSpec((tm, D_MODEL), row), full(ffn_g)]
    args = [ret_o, fox_o, mq, mkt, mvs, bd, g_mem, w_out, x, ffn_g]
    params = pltpu.CompilerParams(dimension_semantics=("arbitrary",), vmem_limit_bytes=VMEM_LIMIT)
    if router is None:
        return pl.pallas_call(
            _out_proj_dense_kernel, grid=(T // tm,), in_specs=in_specs,
            out_specs=[pl.BlockSpec((tm, D_MODEL), row), pl.BlockSpec((tm, D_MODEL), row)],
            out_shape=[jax.ShapeDtypeStruct((T, D_MODEL), F32), jax.ShapeDtypeStruct((T, D_MODEL), BF16)],
            compiler_params=params, name="out_proj_dense",
        )(*args)
    rw, tri = router
    return pl.pallas_call(
        _out_proj_moe_kernel, grid=(T // tm,), in_specs=in_specs + [full(rw), full(tri)],
        out_specs=[pl.BlockSpec((tm, D_MODEL), row), pl.BlockSpec((tm, D_MODEL), row),
                   pl.BlockSpec((tm, LANE), row), pl.BlockSpec((8, LANE), fix)],
        out_shape=[jax.ShapeDtypeStruct((T, D_MODEL), F32), jax.ShapeDtypeStruct((T, D_MODEL), F32),
                   jax.ShapeDtypeStruct((T, LANE), F32), jax.ShapeDtypeStruct((8, LANE), F32)],
        scratch_shapes=[pltpu.VMEM((8, LANE), F32)],
        compiler_params=params, name="out_proj_moe",
    )(*args, rw, tri)


def _dense_ffn_kernel(h_ref, wg_ref, wu_ref, wd_ref, x_ref, o_ref, acc_ref):
    j = pl.program_id(1)

    @pl.when(j == 0)
    def _():
        acc_ref[...] = x_ref[...]

    h = h_ref[...]
    g = jnp.dot(h, wg_ref[0], preferred_element_type=F32)
    u = jnp.dot(h, wu_ref[0], preferred_element_type=F32)
    a = (_silu(g) * u).astype(BF16)
    acc_ref[...] += jnp.dot(a, wd_ref[0], preferred_element_type=F32)

    @pl.when(j == pl.num_programs(1) - 1)
    def _():
        o_ref[...] = acc_ref[...]


def _dense_ffn(h, wg, wu, wd, x, layer):
    T = h.shape[0]
    tm = TOK_TILE
    nf = D_FF_PAD // FF_TILE
    return pl.pallas_call(
        _dense_ffn_kernel,
        grid=(T // tm, nf),
        in_specs=[pl.BlockSpec((tm, D_MODEL), lambda i, j: (i, 0)),
                  pl.BlockSpec((1, D_MODEL, FF_TILE), lambda i, j: (layer, 0, j)),
                  pl.BlockSpec((1, D_MODEL, FF_TILE), lambda i, j: (layer, 0, j)),
                  pl.BlockSpec((1, FF_TILE, D_MODEL), lambda i, j: (layer, j, 0)),
                  pl.BlockSpec((tm, D_MODEL), lambda i, j: (i, 0))],
        out_specs=pl.BlockSpec((tm, D_MODEL), lambda i, j: (i, 0)),
        out_shape=jax.ShapeDtypeStruct((T, D_MODEL), F32),
        scratch_shapes=[pltpu.VMEM((tm, D_MODEL), F32)],
        compiler_params=pltpu.CompilerParams(dimension_semantics=("arbitrary", "arbitrary"),
                                             vmem_limit_bytes=VMEM_LIMIT),
        name="dense_ffn",
    )(h, wg, wu, wd, x)


def _row_copy(src, s, dst, d, sem):
    return pltpu.make_async_copy(src.at[pl.ds(s, 1)], dst.at[pl.ds(d, 1)], sem)


def _moe_ffn_kernel(te_ref, nu_ref, src_ref, h_hbm, wg_ref, wu_ref, wd_ref, o_ref, x_ref, acc_ref, sem):
    i = pl.program_id(0)
    j = pl.program_id(1)
    n_used = nu_ref[0]
    tm = x_ref.shape[1]
    slot = i % 2

    def gather(tile, into):
        base = tile * tm

        def issue(r, carry):
            _row_copy(h_hbm, src_ref[base + r], x_ref.at[into], r, sem.at[into]).start()
            return carry

        lax.fori_loop(0, tm, issue, 0, unroll=8)

    @pl.when((i == 0) & (j == 0))
    def _():
        gather(0, 0)

    @pl.when((j == 0) & (i < n_used))
    def _():
        pltpu.make_async_copy(h_hbm.at[pl.ds(0, tm)], x_ref.at[slot], sem.at[slot]).wait()

    @pl.when((j == 0) & (i + 1 < n_used))
    def _():
        gather(i + 1, 1 - slot)

    @pl.when(i < n_used)
    def _():
        x = x_ref[slot].astype(BF16)
        g = jnp.dot(x, wg_ref[0, 0], preferred_element_type=F32)
        u = jnp.dot(x, wu_ref[0, 0], preferred_element_type=F32)
        a = (_silu(g) * u).astype(BF16)
        part = jnp.dot(a, wd_ref[0, 0], preferred_element_type=F32)

        @pl.when(j == 0)
        def _():
            acc_ref[...] = part

        @pl.when(j > 0)
        def _():
            acc_ref[...] += part

        @pl.when(j == pl.num_programs(1) - 1)
        def _():
            o_ref[...] = acc_ref[...]

    @pl.when(i >= n_used)
    def _():
        o_ref[...] = jnp.zeros_like(o_ref)


def _moe_ffn(h, src, tile_expert, n_used, wg, wu, wd, layer):
    rows = src.shape[0]
    tm = MOE_TILE
    nf = D_FF_PAD // FF_TILE
    last = nf - 1

    def w_cols(i, j, te, nu, sr):
        return (layer, te[i], 0, jnp.where(i < nu[0], j, last))

    def w_rows(i, j, te, nu, sr):
        return (layer, te[i], jnp.where(i < nu[0], j, last), 0)

    return pl.pallas_call(
        _moe_ffn_kernel,
        grid_spec=pltpu.PrefetchScalarGridSpec(
            num_scalar_prefetch=3, grid=(rows // tm, nf),
            in_specs=[pl.BlockSpec(memory_space=pl.ANY),
                      pl.BlockSpec((1, 1, D_MODEL, FF_TILE), w_cols),
                      pl.BlockSpec((1, 1, D_MODEL, FF_TILE), w_cols),
                      pl.BlockSpec((1, 1, FF_TILE, D_MODEL), w_rows)],
            out_specs=pl.BlockSpec((tm, D_MODEL), lambda i, j, te, nu, sr: (i, 0)),
            scratch_shapes=[pltpu.VMEM((2, tm, D_MODEL), F32), pltpu.VMEM((tm, D_MODEL), F32),
                            pltpu.SemaphoreType.DMA((2,))]),
        out_shape=jax.ShapeDtypeStruct((rows, D_MODEL), F32),
        compiler_params=pltpu.CompilerParams(dimension_semantics=("arbitrary", "arbitrary"),
                                             vmem_limit_bytes=VMEM_LIMIT),
        name="moe_ffn",
    )(tile_expert, n_used, src, h, wg, wu, wd)


def _combine_kernel(tc, d1_ref, d2_ref, o_hbm, x_ref, route_ref, out_ref, b1_ref, b2_ref, sem):
    base = pl.program_id(0) * tc

    def issue(r, carry):
        t = base + r
        _row_copy(o_hbm, d1_ref[t], b1_ref, r, sem.at[0]).start()
        _row_copy(o_hbm, d2_ref[t], b2_ref, r, sem.at[1]).start()
        return carry

    lax.fori_loop(0, tc, issue, 0, unroll=8)
    pltpu.make_async_copy(o_hbm.at[pl.ds(0, tc)], b1_ref, sem.at[0]).wait()
    pltpu.make_async_copy(o_hbm.at[pl.ds(0, tc)], b2_ref, sem.at[1]).wait()
    route = route_ref[...]
    g1 = route[:, 4:5]
    g2 = route[:, 5:6]
    out_ref[...] = x_ref[...] + g1 * b1_ref[...] + g2 * b2_ref[...]


def _combine(o, d1, d2, x, route):
    T = x.shape[0]
    tc = 256
    return pl.pallas_call(
        functools.partial(_combine_kernel, tc),
        grid_spec=pltpu.PrefetchScalarGridSpec(
            num_scalar_prefetch=2, grid=(T // tc,),
            in_specs=[pl.BlockSpec(memory_space=pl.ANY),
                      pl.BlockSpec((tc, D_MODEL), lambda i, a, b: (i, 0)),
                      pl.BlockSpec((tc, LANE), lambda i, a, b: (i, 0))],
            out_specs=pl.BlockSpec((tc, D_MODEL), lambda i, a, b: (i, 0)),
            scratch_shapes=[pltpu.VMEM((tc, D_MODEL), F32), pltpu.VMEM((tc, D_MODEL), F32),
                            pltpu.SemaphoreType.DMA((2,))]),
        out_shape=jax.ShapeDtypeStruct((T, D_MODEL), F32),
        compiler_params=pltpu.CompilerParams(dimension_semantics=("arbitrary",),
                                             vmem_limit_bytes=VMEM_LIMIT),
        name="moe_combine",
    )(d1, d2, o, x, route)


def _moe_layer(h, xm, route, counts, wg, wu, wd, layer):
    T = h.shape[0]
    tm = MOE_TILE
    n_tiles = (2 * T) // tm + N_EXPERTS
    e1 = route[:, 0].astype(jnp.int32)
    e2 = route[:, 1].astype(jnp.int32)
    r1 = route[:, 2].astype(jnp.int32)
    r2 = route[:, 3].astype(jnp.int32)
    cnt = counts[0, :N_EXPERTS].astype(jnp.int32)
    tiles = (cnt + tm - 1) // tm
    tile_end = jnp.cumsum(tiles)
    offs = (tile_end - tiles) * tm
    d1 = offs[e1] + r1
    d2 = offs[e2] + r2
    n_used = tile_end[-1:]
    tile_id = jnp.arange(n_tiles, dtype=jnp.int32)
    tile_expert = jnp.sum(tile_id[:, None] >= tile_end[None, :], axis=1).astype(jnp.int32)
    tile_expert = jnp.minimum(tile_expert, tile_expert[jnp.maximum(n_used[0] - 1, 0)])
    tok = jnp.arange(T, dtype=jnp.int32)
    src = jnp.zeros((n_tiles * tm,), jnp.int32)
    src = src.at[jnp.concatenate([d1, d2])].set(jnp.concatenate([tok, tok]), unique_indices=True)
    o = _moe_ffn(h, src, tile_expert, n_used.astype(jnp.int32), wg, wu, wd, layer)
    return _combine(o, d1, d2, xm, route)


def _constants(seq):
    tm = TOK_TILE
    r = np.arange(tm)
    tri_incl = (r[:, None] >= r[None, :]).astype(np.float32)
    tri_strict = (r[:, None] > r[None, :]).astype(np.float32)
    d = np.arange(256)
    bd = (d[:, None] // 64 == d[None, :] // 64).astype(np.float32) / 64.0
    pq = np.zeros((N_PIECE * LANE, FOX_HEADS * LANE), np.float32)
    pk = np.zeros_like(pq)
    oneq = np.zeros((1, LANE), np.float32)
    onek = np.zeros((1, LANE), np.float32)
    for p in range(N_PIECE):
        oneq[0, C_LANE + N_PIECE + p] = 1.0
        onek[0, C_LANE + p] = 1.0
        for h in range(FOX_HEADS):
            pq[p * LANE + h, h * LANE + C_LANE + p] = 1.0
            pk[p * LANE + h, h * LANE + C_LANE + N_PIECE + p] = -1.0
    in_consts = (jnp.asarray(tri_incl, BF16), jnp.asarray(bd, BF16), jnp.asarray(pq, BF16),
                 jnp.asarray(pk, BF16), jnp.asarray(oneq), jnp.asarray(onek))

    half = RET_DK // 2
    pos = jnp.arange(seq, dtype=F32)
    inv_freq = ROPE_BASE ** (-jnp.arange(half, dtype=F32) / half)
    ang = pos[:, None] * inv_freq[None, :]
    cos = jnp.tile(jnp.cos(ang), (1, RET_HEADS))
    sin = jnp.tile(jnp.sin(ang), (1, RET_HEADS))

    C = RET_CHUNK
    hh = jnp.arange(RET_HEADS, dtype=F32)
    log_g = jnp.log(1.0 - 2.0 ** (-5.0 - hh))
    idx = jnp.arange(C, dtype=F32)
    diff = idx[:, None] - idx[None, :]
    dmat = jnp.where(diff >= 0, jnp.exp(log_g[:, None, None] * jnp.maximum(diff, 0.0)), 0.0)
    dstack = dmat.reshape(RET_HEADS * C, C)
    k_head = (np.arange(256) % LANE) // 32
    v_head = np.arange(RET_HEADS * RET_DV) // RET_DV
    qd = jnp.exp(log_g[k_head][None, :] * (idx[:, None] + 1.0))
    kd = jnp.exp(log_g[k_head][None, :] * (C - 1.0 - idx[:, None]))
    cd = jnp.exp(log_g[v_head] * C)[None, :]
    bm = jnp.asarray((k_head[:, None] == v_head[None, :]).astype(np.float32))
    hm = jnp.asarray((np.arange(RET_HEADS)[:, None] == k_head[None, :]).astype(np.float32), BF16)
    ret_tables = (hm, dstack, qd, kd, cd, bm)
    return in_consts, cos, sin, ret_tables, jnp.asarray(tri_strict, BF16)


def _pad_heads(g):
    depth = g.shape[0]
    g = g.reshape(depth, FOX_HEADS, FOX_DH)
    return jnp.pad(g, ((0, 0), (0, 0), (0, LANE - FOX_DH))).reshape(depth, 1, FOX_HEADS * LANE)


def kernel(x, mem, attn_norm_g, w_in, forget_bias, fox_q_norm_g, fox_k_norm_g, mem_q_norm_g, mem_k_norm_g,
           mem_norm_g, w_mem_kv, out_norm_g, w_out, ffn_norm_g, dense_w_gate, dense_w_up, dense_w_down,
           router_w, expert_w_gate, expert_w_up, expert_w_down):
    B, S, D = x.shape
    depth = w_in.shape[0]
    T = B * S
    in_consts, cos, sin, ret_tables, tri_strict = _constants(S)
    bd = in_consts[1]

    cols = _in_proj_columns()
    w_in_z = jnp.concatenate([w_in, jnp.zeros((depth, D, 1), w_in.dtype)], axis=-1)
    w_in_p = jnp.take(w_in_z, jnp.asarray(np.where(cols < 0, w_in.shape[-1], cols)), axis=-1).astype(BF16)
    lane_pad = lambda g: jnp.pad(g, ((0, 0), (0, LANE - g.shape[-1])))[:, None, :]
    fqg = lane_pad(fox_q_norm_g)
    fkg = lane_pad(fox_k_norm_g)
    fb = lane_pad(forget_bias)
    mqg = jnp.tile(mem_q_norm_g, (1, MEM_HEADS))[:, None, :]
    mkg = jnp.tile(mem_k_norm_g, (1, MEM_HEADS))[:, None, :]
    n_ret = RET_HEADS * RET_DV
    n_fox = FOX_HEADS * FOX_DH
    g_ret = out_norm_g[:, None, :n_ret]
    g_fox = _pad_heads(out_norm_g[:, n_ret:n_ret + n_fox])
    g_mem = out_norm_g[:, None, n_ret + n_fox:]
    w_fox = w_out[:, n_ret:n_ret + n_fox].reshape(depth, FOX_HEADS, FOX_DH, D)
    w_fox = jnp.pad(w_fox, ((0, 0), (0, 0), (0, LANE - FOX_DH), (0, 0))).reshape(depth, FOX_HEADS * LANE, D)
    w_out_p = jnp.concatenate([w_out[:, :n_ret], w_fox, w_out[:, n_ret + n_fox:]], axis=1).astype(BF16)
    ffpad = D_FF_PAD - D_FF
    pad_cols = lambda w: jnp.pad(w, [(0, 0)] * (w.ndim - 1) + [(0, ffpad)]).astype(BF16)
    pad_rows = lambda w: jnp.pad(w, [(0, 0)] * (w.ndim - 2) + [(0, ffpad), (0, 0)]).astype(BF16)
    dwg, dwu, dwd = pad_cols(dense_w_gate), pad_cols(dense_w_up), pad_rows(dense_w_down)
    ewg, ewu, ewd = pad_cols(expert_w_gate), pad_cols(expert_w_up), pad_rows(expert_w_down)
    rw = jnp.pad(router_w, ((0, 0), (0, 0), (0, LANE - N_EXPERTS)))

    mkt, mvs = _mem_prep(mem, mem_norm_g[:, None, :], w_mem_kv.astype(BF16), mkg, bd)

    xt = x.reshape(T, D)
    for l in range(depth):
        rq, rk, rv, rg, fqt, fka, fvt, mq = _in_proj(
            xt, attn_norm_g[l][None], w_in_p[l], cos, sin, fqg[l], fkg[l], mqg[l], fb[l], in_consts, S)
        ret_o = _retention(rq, rk, rv, rg, ret_tables, g_ret[l], B, S)
        fox_o = _fox(fqt, fka, fvt, g_fox[l], B, S)
        j = l // 2
        if l % 2 == 0:
            xm, h2 = _out_proj(ret_o, fox_o, mq, mkt, mvs, bd, g_mem[l], w_out_p[l], xt, ffn_norm_g[l][None], l, S)
            xt = _dense_ffn(h2, dwg, dwu, dwd, xm, j)
        else:
            xm, h2, route, counts = _out_proj(ret_o, fox_o, mq, mkt, mvs, bd, g_mem[l], w_out_p[l], xt,
                                              ffn_norm_g[l][None], l, S, router=(rw[j], tri_strict))
            xt = _moe_layer(h2, xm, route, counts, ewg, ewu, ewd, j)
    return xt.reshape(B, S, D)
```

```python
import functools

import numpy as np
import jax
import jax.numpy as jnp
from jax import lax
from jax.experimental import pallas as pl
from jax.experimental.pallas import tpu as pltpu

F32 = jnp.float32
BF16 = jnp.bfloat16

D_MODEL = 1024
RET_HEADS = 4
RET_DK = 64
RET_DV = 128
FOX_HEADS = 4
FOX_DH = 64
MEM_HEADS = 4
MEM_DH = 64
RET_CHUNK = 128
D_FF = 2752
N_EXPERTS = 8
ROPE_BASE = 10000.0
EPS = 1e-6

LANE = 128
D_FF_PAD = 2816
FF_TILE = D_FF_PAD // 2
TOK_TILE = 512
FOX_Q_TILE = 1024
FOX_K_TILE = 512
LOG2E = 1.4426950408889634
MOE_TILE = 512
VMEM_LIMIT = 56 * 1024 * 1024
NEG = -1e30

RQ, RK, RV, RG = 0, 256, 512, 1024
FQ, FK, FV = 1536, 2048, 2560
MQ, FF_COL, P_IN = 3072, 3328, 3456
C_LANE = 64
N_PIECE = 3
V_ONE = 64


def _reorder_in_proj(w_in):
    depth, d, _ = w_in.shape
    parts = []
    for base in (0, 256):
        w = w_in[:, :, base:base + 256].reshape(depth, d, RET_HEADS, 2, 32)
        parts.append(w.transpose(0, 1, 3, 2, 4).reshape(depth, d, 256))
    parts.append(w_in[:, :, 512:1536])
    for base in (1536, 1792, 2048):
        w = w_in[:, :, base:base + 256].reshape(depth, d, FOX_HEADS, FOX_DH)
        parts.append(jnp.pad(w, ((0, 0), (0, 0), (0, 0), (0, LANE - FOX_DH))).reshape(depth, d, FOX_HEADS * LANE))
    parts.append(w_in[:, :, 2308:2564])
    parts.append(jnp.pad(w_in[:, :, 2304:2308], ((0, 0), (0, 0), (0, LANE - FOX_HEADS))))
    out = jnp.concatenate(parts, axis=-1).astype(BF16)
    assert out.shape[-1] == P_IN
    return out


def _split3(x):
    hi = x.astype(BF16)
    r1 = x - hi.astype(F32)
    mid = r1.astype(BF16)
    lo = (r1 - mid.astype(F32)).astype(BF16)
    return hi, mid, lo


def _group_mean_sq(x, bd):
    sq = x * x
    hi = sq.astype(BF16)
    lo = (sq - hi.astype(F32)).astype(BF16)
    return (jnp.dot(hi, bd, preferred_element_type=F32)
            + jnp.dot(lo, bd, preferred_element_type=F32))


def _silu(x):
    return x / (1.0 + jnp.exp(-x))


def _in_proj_kernel(tiles_per_seq, x_ref, g_ref, w_ref, cos_ref, sin_ref, fqg_ref, fkg_ref, mqg_ref,
                    fb_ref, tri_ref, bd_ref, pq_ref, pk_ref, oneq_ref, onek_ref,
                    rq_ref, rk_ref, rv_ref, rg_ref, fqt_ref, fka_ref, fvt_ref, mq_ref, carry_ref):
    i = pl.program_id(0)

    @pl.when(i % tiles_per_seq == 0)
    def _():
        carry_ref[...] = jnp.zeros_like(carry_ref)

    x = x_ref[...]
    ms = jnp.mean(x * x, axis=-1, keepdims=True)
    h = (x * lax.rsqrt(ms + EPS) * g_ref[...]).astype(BF16)

    def proj(a, b):
        return jnp.dot(h, w_ref[:, a:b], preferred_element_type=F32)

    cos = cos_ref[...]
    sin = sin_ref[...]
    for base, ref, scale in ((RQ, rq_ref, RET_DK ** -0.5), (RK, rk_ref, 1.0)):
        x1 = proj(base, base + LANE)
        x2 = proj(base + LANE, base + 2 * LANE)
        ref[:, 0:LANE] = ((x1 * cos - x2 * sin) * scale).astype(BF16)
        ref[:, LANE:2 * LANE] = ((x2 * cos + x1 * sin) * scale).astype(BF16)
    rv_ref[...] = proj(RV, RV + 512).astype(BF16)
    rg_ref[...] = proj(RG, RG + 512).astype(BF16)
    vt = proj(FV, FV + 512).T
    vrow = lax.broadcasted_iota(jnp.int32, vt.shape, 0) % LANE
    fvt_ref[...] = jnp.where(vrow == V_ONE, 1.0, vt).astype(BF16)

    mq = proj(MQ, MQ + 256)
    mq = mq * lax.rsqrt(_group_mean_sq(mq, bd_ref[...]) + EPS) * mqg_ref[...] * (MEM_DH ** -0.5)
    mq_ref[...] = mq.astype(BF16)

    z = proj(FF_COL, FF_COL + LANE) + fb_ref[...]
    logf = jnp.minimum(z, 0.0) - jnp.log(1.0 + jnp.exp(-jnp.abs(z)))
    lane = lax.broadcasted_iota(jnp.int32, logf.shape, 1)
    logf = jnp.where(lane < FOX_HEADS, logf, 0.0)
    tri = tri_ref[...]
    c = carry_ref[0:1, :]
    for piece in _split3(logf):
        c = c + jnp.dot(tri, piece, preferred_element_type=F32)
    tm = c.shape[0]
    carry_ref[0:1, :] = c[tm - 1:tm, :]

    cp = jnp.concatenate(_split3(c * LOG2E), axis=1)
    extra_q = jnp.dot(cp, pq_ref[...], preferred_element_type=F32)
    extra_k = jnp.dot(cp, pk_ref[...], preferred_element_type=F32)
    fq = proj(FQ, FQ + 512)
    fk = proj(FK, FK + 512)
    q_tiles = []
    for hh in range(FOX_HEADS):
        sl = slice(hh * LANE, (hh + 1) * LANE)
        tiles = []
        for t, gref, scale, extra, one in ((fq[:, sl], fqg_ref, FOX_DH ** -0.5 * LOG2E, extra_q[:, sl], oneq_ref),
                                           (fk[:, sl], fkg_ref, 1.0, extra_k[:, sl], onek_ref)):
            msq = jnp.sum(t * t, axis=-1, keepdims=True) * (1.0 / FOX_DH)
            tn = t * lax.rsqrt(msq + EPS) * gref[...] * scale
            tiles.append(tn + extra + one[...])
        q_tiles.append(tiles[0])
        fka_ref[:, sl] = tiles[1].astype(BF16)
    fqt_ref[...] = jnp.concatenate(q_tiles, axis=1).T.astype(BF16)


def _in_proj(x, g, w, cos, sin, fqg, fkg, mqg, fb, consts, seq):
    T = x.shape[0]
    tm = TOK_TILE
    nt = T // tm
    tps = seq // tm
    row = lambda i: (i, 0)
    fix = lambda i: (0, 0)
    full = lambda a: pl.BlockSpec(a.shape, fix)
    tri, bd, pq, pk, oneq, onek = consts
    widths = (256, 256, 512, 512, -512, 512, -512, 256)
    outs = [jax.ShapeDtypeStruct((T, n) if n > 0 else (-n, T), BF16) for n in widths]
    out_specs = [pl.BlockSpec((tm, n), row) if n > 0 else pl.BlockSpec((-n, tm), lambda i: (0, i))
                 for n in widths]
    return pl.pallas_call(
        functools.partial(_in_proj_kernel, tps),
        grid=(nt,),
        in_specs=[pl.BlockSpec((tm, D_MODEL), row), full(g), full(w),
                  pl.BlockSpec((tm, LANE), lambda i: (i % tps, 0)),
                  pl.BlockSpec((tm, LANE), lambda i: (i % tps, 0)),
                  full(fqg), full(fkg), full(mqg), full(fb), full(tri), full(bd), full(pq), full(pk),
                  full(oneq), full(onek)],
        out_specs=out_specs,
        out_shape=outs,
        scratch_shapes=[pltpu.VMEM((8, LANE), F32)],
        compiler_params=pltpu.CompilerParams(dimension_semantics=("arbitrary",),
                                             vmem_limit_bytes=VMEM_LIMIT),
        name="in_proj",
    )(x, g, w, cos, sin, fqg, fkg, mqg, fb, tri, bd, pq, pk, oneq, onek)


def _retention_kernel(chunks, rq_ref, rk_ref, rv_ref, rg_ref, hm_ref, dst_ref, qd_ref, kd_ref, cd_ref,
                      bm_ref, g_ref, o_ref, state_ref):
    @pl.when(pl.program_id(1) == 0)
    def _():
        state_ref[...] = jnp.zeros_like(state_ref)

    C = RET_CHUNK
    for ci in range(chunks):
        rows = slice(ci * C, (ci + 1) * C)
        qc = rq_ref[rows, :]
        kc = rk_ref[rows, :]
        vc = rv_ref[rows, :]
        qs = jnp.concatenate([qc * hm_ref[hh:hh + 1, :] for hh in range(RET_HEADS)], axis=0)
        sc = lax.dot_general(qs, kc, (((1,), (1,)), ((), ())), preferred_element_type=F32)
        sc = (sc * dst_ref[...]).astype(BF16)
        state = state_ref[...]
        qdec = (qc.astype(F32) * qd_ref[...]).astype(BF16)
        cross = jnp.dot(qdec, state.astype(BF16), preferred_element_type=F32)
        kdec = (kc.astype(F32) * kd_ref[...]).astype(BF16)
        kv = lax.dot_general(kdec, vc, (((0,), (0,)), ((), ())), preferred_element_type=F32)
        state_ref[...] = state * cd_ref[...] + kv * bm_ref[...]
        for hh in range(RET_HEADS):
            cols = slice(hh * RET_DV, (hh + 1) * RET_DV)
            o = jnp.dot(sc[hh * C:(hh + 1) * C, :], vc[:, cols], preferred_element_type=F32) + cross[:, cols]
            msq = jnp.mean(o * o, axis=-1, keepdims=True)
            y = o * lax.rsqrt(msq + EPS) * g_ref[:, cols]
            o_ref[rows, cols] = (y * _silu(rg_ref[rows, cols].astype(F32))).astype(BF16)


def _retention(rq, rk, rv, rg, tables, g_ret, batch, seq):
    T = rq.shape[0]
    tm = TOK_TILE
    tps = seq // tm
    row = lambda b, i: (b * tps + i, 0)
    fix = lambda b, i: (0, 0)
    full = lambda a: pl.BlockSpec(a.shape, fix)
    return pl.pallas_call(
        functools.partial(_retention_kernel, tm // RET_CHUNK),
        grid=(batch, tps),
        in_specs=[pl.BlockSpec((tm, 256), row), pl.BlockSpec((tm, 256), row),
                  pl.BlockSpec((tm, 512), row), pl.BlockSpec((tm, 512), row)]
                 + [full(t) for t in tables] + [full(g_ret)],
        out_specs=pl.BlockSpec((tm, 512), row),
        out_shape=jax.ShapeDtypeStruct((T, 512), BF16),
        scratch_shapes=[pltpu.VMEM((RET_HEADS * RET_DK, RET_HEADS * RET_DV), F32)],
        compiler_params=pltpu.CompilerParams(dimension_semantics=("arbitrary", "arbitrary"),
                                             vmem_limit_bytes=VMEM_LIMIT),
        name="retention",
    )(rq, rk, rv, rg, *tables, g_ret)


def _fox_kernel(tq, tk, qt_ref, k_ref, vt_ref, g_ref, o_ref, m_ref, acc_ref, s0_ref, s1_ref):
    i = pl.program_id(2)
    qt = qt_ref[...]
    m_ref[...] = jnp.full_like(m_ref, NEG)
    acc_ref[...] = jnp.zeros_like(acc_ref)
    assert tq == 2 * tk
    s_refs = (s0_ref, s1_ref)

    def scores(j, slot):
        start = pl.multiple_of(j * tk, tk)
        s_refs[slot][...] = jnp.dot(k_ref[pl.ds(start, tk), :], qt, preferred_element_type=F32)

    def update(j, slot, diag):
        start = pl.multiple_of(j * tk, tk)
        st = s_refs[slot][...]
        if diag is not None:
            key = lax.broadcasted_iota(jnp.int32, st.shape, 0) + diag * tk
            qry = lax.broadcasted_iota(jnp.int32, st.shape, 1)
            st = jnp.where(key <= qry, st, NEG)
        m_prev = m_ref[...]
        m_new = jnp.maximum(m_prev, jnp.max(st, axis=0, keepdims=True))
        alpha = jnp.exp2(m_prev - m_new)
        p = jnp.exp2(st - m_new).astype(BF16)
        vt = vt_ref[:, pl.ds(start, tk)]
        acc_ref[...] = alpha * acc_ref[...] + jnp.dot(vt, p, preferred_element_type=F32)
        m_ref[...] = m_new

    scores(0, 0)

    def body(t, carry):
        scores(2 * t + 1, 1)
        update(2 * t, 0, None)
        scores(2 * t + 2, 0)
        update(2 * t + 1, 1, None)
        return carry

    lax.fori_loop(0, i, body, 0)
    scores(2 * i + 1, 1)
    update(2 * i, 0, 0)
    update(2 * i + 1, 1, 1)
    acc = acc_ref[...]
    row = lax.broadcasted_iota(jnp.int32, acc.shape, 0)
    o = jnp.where(row < FOX_DH, acc / acc[V_ONE:V_ONE + 1, :], 0.0)
    msq = jnp.sum(o * o, axis=0, keepdims=True) * (1.0 / FOX_DH)
    o_ref[...] = ((o * lax.rsqrt(msq + EPS)).T * g_ref[...]).astype(BF16)


def _fox(fqt, fka, fvt, g_fox, batch, seq):
    T = fka.shape[0]
    tq = FOX_Q_TILE
    tk = FOX_K_TILE
    nq = seq // tq
    return pl.pallas_call(
        functools.partial(_fox_kernel, tq, tk),
        grid=(batch, FOX_HEADS, nq),
        in_specs=[pl.BlockSpec((LANE, tq), lambda b, h, i: (h, b * nq + i)),
                  pl.BlockSpec((seq, LANE), lambda b, h, i: (b, h)),
                  pl.BlockSpec((LANE, seq), lambda b, h, i: (h, b)),
                  pl.BlockSpec((1, LANE), lambda b, h, i: (0, h))],
        out_specs=pl.BlockSpec((tq, LANE), lambda b, h, i: (b * nq + i, h)),
        out_shape=jax.ShapeDtypeStruct((T, FOX_HEADS * LANE), BF16),
        scratch_shapes=[pltpu.VMEM((1, tq), F32), pltpu.VMEM((LANE, tq), F32),
                        pltpu.VMEM((tk, tq), F32), pltpu.VMEM((tk, tq), F32)],
        compiler_params=pltpu.CompilerParams(dimension_semantics=("arbitrary", "arbitrary", "arbitrary"),
                                             vmem_limit_bytes=VMEM_LIMIT),
        name="fox",
    )(fqt, fka, fvt, g_fox)


def _mem_prep_kernel(mem_ref, g_ref, w_ref, kg_ref, bd_ref, mkt_ref, mvs_ref):
    x = mem_ref[0]
    ms = jnp.mean(x * x, axis=-1, keepdims=True)
    mn = (x * lax.rsqrt(ms + EPS) * g_ref[0]).astype(BF16)
    mkv = jnp.dot(mn, w_ref[0], preferred_element_type=F32)
    width = MEM_HEADS * MEM_DH
    mk = mkv[:, :width]
    mv = mkv[:, width:]
    mk = mk * lax.rsqrt(_group_mean_sq(mk, bd_ref[...]) + EPS) * kg_ref[0]
    mkt = mk.T
    feat = lax.broadcasted_iota(jnp.int32, mkt.shape, 0) // MEM_DH
    lane_head = lax.broadcasted_iota(jnp.int32, mv.shape, 1) // MEM_DH
    M = x.shape[0]
    for hh in range(MEM_HEADS):
        mkt_ref[0, 0, :, hh * M:(hh + 1) * M] = jnp.where(feat == hh, mkt, 0.0).astype(BF16)
        mvs_ref[0, 0, hh * M:(hh + 1) * M, :] = jnp.where(lane_head == hh, mv, 0.0).astype(BF16)


def _mem_prep(mem, mem_norm_g, w_mem_kv, mem_k_g, bd):
    B, M, _ = mem.shape
    depth = w_mem_kv.shape[0]
    width = MEM_HEADS * MEM_DH
    return pl.pallas_call(
        _mem_prep_kernel,
        grid=(depth, B),
        in_specs=[pl.BlockSpec((1, M, D_MODEL), lambda l, b: (b, 0, 0)),
                  pl.BlockSpec((1, 1, D_MODEL), lambda l, b: (l, 0, 0)),
                  pl.BlockSpec((1, D_MODEL, 2 * width), lambda l, b: (l, 0, 0)),
                  pl.BlockSpec((1, 1, width), lambda l, b: (l, 0, 0)),
                  pl.BlockSpec(bd.shape, lambda l, b: (0, 0))],
        out_specs=[pl.BlockSpec((1, 1, width, MEM_HEADS * M), lambda l, b: (l, b, 0, 0)),
                   pl.BlockSpec((1, 1, MEM_HEADS * M, width), lambda l, b: (l, b, 0, 0))],
        out_shape=[jax.ShapeDtypeStruct((depth, B, width, MEM_HEADS * M), BF16),
                   jax.ShapeDtypeStruct((depth, B, MEM_HEADS * M, width), BF16)],
        compiler_params=pltpu.CompilerParams(dimension_semantics=("arbitrary", "arbitrary"),
                                             vmem_limit_bytes=VMEM_LIMIT),
        name="mem_prep",
    )(mem, mem_norm_g, w_mem_kv, mem_k_g, bd)


def _out_proj_body(ret_ref, fox_ref, mq_ref, mkt_ref, mvs_ref, bd_ref, gm_ref, w_ref, x_ref, fg_ref):
    M = mkt_ref.shape[-1] // MEM_HEADS
    s = jnp.dot(mq_ref[...], mkt_ref[0, 0], preferred_element_type=F32)
    mem_o = None
    for hh in range(MEM_HEADS):
        sh = s[:, hh * M:(hh + 1) * M]
        p = jnp.exp(sh - jnp.max(sh, axis=-1, keepdims=True))
        p = (p / jnp.sum(p, axis=-1, keepdims=True)).astype(BF16)
        part = jnp.dot(p, mvs_ref[0, 0, hh * M:(hh + 1) * M, :], preferred_element_type=F32)
        mem_o = part if mem_o is None else mem_o + part
    mem_o = (mem_o * lax.rsqrt(_group_mean_sq(mem_o, bd_ref[...]) + EPS) * gm_ref[...]).astype(BF16)
    y = (jnp.dot(ret_ref[...], w_ref[0:512, :], preferred_element_type=F32)
         + jnp.dot(fox_ref[...], w_ref[512:1024, :], preferred_element_type=F32)
         + jnp.dot(mem_o, w_ref[1024:1280, :], preferred_element_type=F32))
    xm = x_ref[...] + y
    ms = jnp.mean(xm * xm, axis=-1, keepdims=True)
    return xm, xm * lax.rsqrt(ms + EPS) * fg_ref[...]


def _out_proj_dense_kernel(ret_ref, fox_ref, mq_ref, mkt_ref, mvs_ref, bd_ref, gm_ref, w_ref, x_ref, fg_ref,
                           xm_ref, h_ref):
    xm, hn = _out_proj_body(ret_ref, fox_ref, mq_ref, mkt_ref, mvs_ref, bd_ref, gm_ref, w_ref, x_ref, fg_ref)
    xm_ref[...] = xm
    h_ref[...] = hn.astype(BF16)


def _out_proj_moe_kernel(ret_ref, fox_ref, mq_ref, mkt_ref, mvs_ref, bd_ref, gm_ref, w_ref, x_ref, fg_ref,
                         rw_ref, tri_ref, xm_ref, h_ref, route_ref, cnt_ref, carry_ref):
    @pl.when(pl.program_id(0) == 0)
    def _():
        carry_ref[...] = jnp.zeros_like(carry_ref)

    xm, hn = _out_proj_body(ret_ref, fox_ref, mq_ref, mkt_ref, mvs_ref, bd_ref, gm_ref, w_ref, x_ref, fg_ref)
    xm_ref[...] = xm
    h_ref[...] = hn
    rw = rw_ref[...]
    h_hi = hn.astype(BF16)
    h_lo = (hn - h_hi.astype(F32)).astype(BF16)
    w_hi = rw.astype(BF16)
    w_lo = (rw - w_hi.astype(F32)).astype(BF16)
    logits = (jnp.dot(h_hi, w_hi, preferred_element_type=F32) + jnp.dot(h_lo, w_hi, preferred_element_type=F32)
              + jnp.dot(h_hi, w_lo, preferred_element_type=F32))
    lane = lax.broadcasted_iota(jnp.int32, logits.shape, 1)
    lanef = lane.astype(F32)
    lg = jnp.where(lane < N_EXPERTS, logits, NEG)
    m1 = jnp.max(lg, axis=-1, keepdims=True)
    i1 = jnp.min(jnp.where(lg == m1, lanef, float(LANE)), axis=-1, keepdims=True)
    oh1 = lanef == i1
    lg2 = jnp.where(oh1, NEG, lg)
    m2 = jnp.max(lg2, axis=-1, keepdims=True)
    i2 = jnp.min(jnp.where(lg2 == m2, lanef, float(LANE)), axis=-1, keepdims=True)
    oh2 = lanef == i2
    e = jnp.exp(m2 - m1)
    g1 = 1.0 / (1.0 + e)
    g2 = e / (1.0 + e)
    oh = jnp.where(oh1 | oh2, 1.0, 0.0)
    before = carry_ref[0:1, :] + jnp.dot(tri_ref[...], oh.astype(BF16), preferred_element_type=F32)
    r1 = jnp.sum(jnp.where(oh1, before, 0.0), axis=-1, keepdims=True)
    r2 = jnp.sum(jnp.where(oh2, before, 0.0), axis=-1, keepdims=True)
    tm = oh.shape[0]
    total = before[tm - 1:tm, :] + oh[tm - 1:tm, :]
    carry_ref[0:1, :] = total
    cnt_ref[...] = jnp.broadcast_to(total, cnt_ref.shape)
    route = jnp.zeros_like(logits)
    for col, val in enumerate((i1, i2, r1, r2, g1, g2)):
        route = jnp.where(lane == col, val, route)
    route_ref[...] = route


def _out_proj(ret_o, fox_o, mq, mkt, mvs, bd, g_mem, w_out, x, ffn_g, layer, seq, router=None):
    T = x.shape[0]
    tm = TOK_TILE
    tps = seq // tm
    row = lambda i: (i, 0)
    fix = lambda i: (0, 0)
    full = lambda a: pl.BlockSpec(a.shape, fix)
    M4 = mkt.shape[-1]
    in_specs = [pl.BlockSpec((tm, 512), row), pl.BlockSpec((tm, 512), row), pl.BlockSpec((tm, 256), row),
                pl.BlockSpec((1, 1, 256, M4), lambda i: (layer, i // tps, 0, 0)),
                pl.BlockSpec((1, 1, M4, 256), lambda i: (layer, i // tps, 0, 0)),
                full(bd), full(g_mem), full(w_out), pl.BlockSpec((tm, D_MODEL), row), full(ffn_g)]
    args = [ret_o, fox_o, mq, mkt, mvs, bd, g_mem, w_out, x, ffn_g]
    params = pltpu.CompilerParams(dimension_semantics=("arbitrary",), vmem_limit_bytes=VMEM_LIMIT)
    if router is None:
        return pl.pallas_call(
            _out_proj_dense_kernel, grid=(T // tm,), in_specs=in_specs,
            out_specs=[pl.BlockSpec((tm, D_MODEL), row), pl.BlockSpec((tm, D_MODEL), row)],
            out_shape=[jax.ShapeDtypeStruct((T, D_MODEL), F32), jax.ShapeDtypeStruct((T, D_MODEL), BF16)],
            compiler_params=params, name="out_proj_dense",
        )(*args)
    rw, tri = router
    return pl.pallas_call(
        _out_proj_moe_kernel, grid=(T // tm,), in_specs=in_specs + [full(rw), full(tri)],
        out_specs=[pl.BlockSpec((tm, D_MODEL), row), pl.BlockSpec((tm, D_MODEL), row),
                   pl.BlockSpec((tm, LANE), row), pl.BlockSpec((8, LANE), fix)],
        out_shape=[jax.ShapeDtypeStruct((T, D_MODEL), F32), jax.ShapeDtypeStruct((T, D_MODEL), F32),
                   jax.ShapeDtypeStruct((T, LANE), F32), jax.ShapeDtypeStruct((8, LANE), F32)],
        scratch_shapes=[pltpu.VMEM((8, LANE), F32)],
        compiler_params=params, name="out_proj_moe",
    )(*args, rw, tri)


def _swiglu(x, wg_ref, wu_ref, wd_ref, y):
    for c in range(D_FF_PAD // FF_TILE):
        cols = slice(c * FF_TILE, (c + 1) * FF_TILE)
        g = jnp.dot(x, wg_ref[:, cols], preferred_element_type=F32)
        u = jnp.dot(x, wu_ref[:, cols], preferred_element_type=F32)
        a = (_silu(g) * u).astype(BF16)
        part = jnp.dot(a, wd_ref[cols, :], preferred_element_type=F32)
        y = part if y is None else y + part
    return y


def _dense_ffn_kernel(h_ref, wg_ref, wu_ref, wd_ref, x_ref, o_ref):
    o_ref[...] = _swiglu(h_ref[...], wg_ref.at[0], wu_ref.at[0], wd_ref.at[0], x_ref[...])


def _dense_ffn(h, wg, wu, wd, x, layer):
    T = h.shape[0]
    tm = TOK_TILE
    once = pl.Buffered(1)
    return pl.pallas_call(
        _dense_ffn_kernel,
        grid=(T // tm,),
        in_specs=[pl.BlockSpec((tm, D_MODEL), lambda i: (i, 0)),
                  pl.BlockSpec((1, D_MODEL, D_FF_PAD), lambda i: (layer, 0, 0), pipeline_mode=once),
                  pl.BlockSpec((1, D_MODEL, D_FF_PAD), lambda i: (layer, 0, 0), pipeline_mode=once),
                  pl.BlockSpec((1, D_FF_PAD, D_MODEL), lambda i: (layer, 0, 0), pipeline_mode=once),
                  pl.BlockSpec((tm, D_MODEL), lambda i: (i, 0))],
        out_specs=pl.BlockSpec((tm, D_MODEL), lambda i: (i, 0)),
        out_shape=jax.ShapeDtypeStruct((T, D_MODEL), F32),
        compiler_params=pltpu.CompilerParams(dimension_semantics=("arbitrary",),
                                             vmem_limit_bytes=VMEM_LIMIT),
        name="dense_ffn",
    )(h, wg, wu, wd, x)


def _row_copy(src, s, dst, d, sem):
    return pltpu.make_async_copy(src.at[pl.ds(s, 1)], dst.at[pl.ds(d, 1)], sem)


def _moe_ffn_kernel(te_ref, nu_ref, src_ref, h_hbm, wg_ref, wu_ref, wd_ref, o_ref, x_ref, sem):
    i = pl.program_id(0)
    n_used = nu_ref[0]
    tm = x_ref.shape[1]
    slot = i % 2
    nxt = 1 - slot

    def tile_wait(s):
        pltpu.make_async_copy(h_hbm.at[pl.ds(0, tm)], x_ref.at[s], sem.at[s]).wait()

    @pl.when(i == 0)
    def _():
        def issue(r, carry):
            _row_copy(h_hbm, src_ref[r], x_ref.at[0], r, sem.at[0]).start()
            return carry

        lax.fori_loop(0, tm, issue, 0, unroll=8)

    @pl.when(i < n_used)
    def _():
        tile_wait(slot)
        x = x_ref[slot].astype(BF16)
        base = (i + 1) * tm
        for r in range(tm):
            _row_copy(h_hbm, src_ref[base + r], x_ref.at[nxt], r, sem.at[nxt]).start()
        o_ref[...] = _swiglu(x, wg_ref.at[0, 0], wu_ref.at[0, 0], wd_ref.at[0, 0], None)

        @pl.when(i + 1 == n_used)
        def _():
            tile_wait(nxt)

    @pl.when(i >= n_used)
    def _():
        o_ref[...] = jnp.zeros_like(o_ref)


def _moe_ffn(h, src, tile_expert, n_used, wg, wu, wd, layer):
    tm = MOE_TILE
    n_tiles = tile_expert.shape[0]
    assert src.shape[0] == (n_tiles + 1) * tm
    w_block = lambda i, te, nu, sr: (layer, te[i], 0, 0)
    return pl.pallas_call(
        _moe_ffn_kernel,
        grid_spec=pltpu.PrefetchScalarGridSpec(
            num_scalar_prefetch=3, grid=(n_tiles,),
            in_specs=[pl.BlockSpec(memory_space=pl.ANY),
                      pl.BlockSpec((1, 1, D_MODEL, D_FF_PAD), w_block),
                      pl.BlockSpec((1, 1, D_MODEL, D_FF_PAD), w_block),
                      pl.BlockSpec((1, 1, D_FF_PAD, D_MODEL), w_block)],
            out_specs=pl.BlockSpec((tm, D_MODEL), lambda i, te, nu, sr: (i, 0)),
            scratch_shapes=[pltpu.VMEM((2, tm, D_MODEL), F32), pltpu.SemaphoreType.DMA((2,))]),
        out_shape=jax.ShapeDtypeStruct((n_tiles * tm, D_MODEL), F32),
        compiler_params=pltpu.CompilerParams(dimension_semantics=("arbitrary",),
                                             vmem_limit_bytes=VMEM_LIMIT),
        name="moe_ffn",
    )(tile_expert, n_used, src, h, wg, wu, wd)


def _combine_kernel(tc, d1_ref, d2_ref, o_hbm, x_ref, route_ref, out_ref, b1_ref, b2_ref, sem):
    base = pl.program_id(0) * tc

    def issue(r, carry):
        t = base + r
        _row_copy(o_hbm, d1_ref[t], b1_ref, r, sem.at[0]).start()
        _row_copy(o_hbm, d2_ref[t], b2_ref, r, sem.at[1]).start()
        return carry

    lax.fori_loop(0, tc, issue, 0, unroll=8)
    pltpu.make_async_copy(o_hbm.at[pl.ds(0, tc)], b1_ref, sem.at[0]).wait()
    pltpu.make_async_copy(o_hbm.at[pl.ds(0, tc)], b2_ref, sem.at[1]).wait()
    route = route_ref[...]
    g1 = route[:, 4:5]
    g2 = route[:, 5:6]
    out_ref[...] = x_ref[...] + g1 * b1_ref[...] + g2 * b2_ref[...]


def _combine(o, d1, d2, x, route):
    T = x.shape[0]
    tc = 256
    return pl.pallas_call(
        functools.partial(_combine_kernel, tc),
        grid_spec=pltpu.PrefetchScalarGridSpec(
            num_scalar_prefetch=2, grid=(T // tc,),
            in_specs=[pl.BlockSpec(memory_space=pl.ANY),
                      pl.BlockSpec((tc, D_MODEL), lambda i, a, b: (i, 0)),
                      pl.BlockSpec((tc, LANE), lambda i, a, b: (i, 0))],
            out_specs=pl.BlockSpec((tc, D_MODEL), lambda i, a, b: (i, 0)),
            scratch_shapes=[pltpu.VMEM((tc, D_MODEL), F32), pltpu.VMEM((tc, D_MODEL), F32),
                            pltpu.SemaphoreType.DMA((2,))]),
        out_shape=jax.ShapeDtypeStruct((T, D_MODEL), F32),
        compiler_params=pltpu.CompilerParams(dimension_semantics=("arbitrary",),
                                             vmem_limit_bytes=VMEM_LIMIT),
        name="moe_combine",
    )(d1, d2, o, x, route)


def _moe_layer(h, xm, route, counts, wg, wu, wd, layer):
    T = h.shape[0]
    tm = MOE_TILE
    n_tiles = (2 * T) // tm + N_EXPERTS
    e1 = route[:, 0].astype(jnp.int32)
    e2 = route[:, 1].astype(jnp.int32)
    r1 = route[:, 2].astype(jnp.int32)
    r2 = route[:, 3].astype(jnp.int32)
    cnt = counts[0, :N_EXPERTS].astype(jnp.int32)
    tiles = (cnt + tm - 1) // tm
    tile_end = jnp.cumsum(tiles)
    offs = (tile_end - tiles) * tm
    d1 = offs[e1] + r1
    d2 = offs[e2] + r2
    n_used = tile_end[-1:]
    tile_id = jnp.arange(n_tiles, dtype=jnp.int32)
    tile_expert = jnp.sum(tile_id[:, None] >= tile_end[None, :], axis=1).astype(jnp.int32)
    tile_expert = jnp.minimum(tile_expert, tile_expert[jnp.maximum(n_used[0] - 1, 0)])
    tok = jnp.arange(T, dtype=jnp.int32)
    src = jnp.zeros(((n_tiles + 1) * tm,), jnp.int32)
    src = src.at[jnp.concatenate([d1, d2])].set(jnp.concatenate([tok, tok]), unique_indices=True)
    o = _moe_ffn(h, src, tile_expert, n_used.astype(jnp.int32), wg, wu, wd, layer)
    return _combine(o, d1, d2, xm, route)


def _constants(seq):
    tm = TOK_TILE
    r = np.arange(tm)
    tri_incl = (r[:, None] >= r[None, :]).astype(np.float32)
    tri_strict = (r[:, None] > r[None, :]).astype(np.float32)
    d = np.arange(256)
    bd = (d[:, None] // 64 == d[None, :] // 64).astype(np.float32) / 64.0
    pq = np.zeros((N_PIECE * LANE, FOX_HEADS * LANE), np.float32)
    pk = np.zeros_like(pq)
    oneq = np.zeros((1, LANE), np.float32)
    onek = np.zeros((1, LANE), np.float32)
    for p in range(N_PIECE):
        oneq[0, C_LANE + N_PIECE + p] = 1.0
        onek[0, C_LANE + p] = 1.0
        for h in range(FOX_HEADS):
            pq[p * LANE + h, h * LANE + C_LANE + p] = 1.0
            pk[p * LANE + h, h * LANE + C_LANE + N_PIECE + p] = -1.0
    in_consts = (jnp.asarray(tri_incl, BF16), jnp.asarray(bd, BF16), jnp.asarray(pq, BF16),
                 jnp.asarray(pk, BF16), jnp.asarray(oneq), jnp.asarray(onek))

    half = RET_DK // 2
    pos = jnp.arange(seq, dtype=F32)
    inv_freq = ROPE_BASE ** (-jnp.arange(half, dtype=F32) / half)
    ang = pos[:, None] * inv_freq[None, :]
    cos = jnp.tile(jnp.cos(ang), (1, RET_HEADS))
    sin = jnp.tile(jnp.sin(ang), (1, RET_HEADS))

    C = RET_CHUNK
    hh = jnp.arange(RET_HEADS, dtype=F32)
    log_g = jnp.log(1.0 - 2.0 ** (-5.0 - hh))
    idx = jnp.arange(C, dtype=F32)
    diff = idx[:, None] - idx[None, :]
    dmat = jnp.where(diff >= 0, jnp.exp(log_g[:, None, None] * jnp.maximum(diff, 0.0)), 0.0)
    dstack = dmat.reshape(RET_HEADS * C, C)
    k_head = (np.arange(256) % LANE) // 32
    v_head = np.arange(RET_HEADS * RET_DV) // RET_DV
    qd = jnp.exp(log_g[k_head][None, :] * (idx[:, None] + 1.0))
    kd = jnp.exp(log_g[k_head][None, :] * (C - 1.0 - idx[:, None]))
    cd = jnp.exp(log_g[v_head] * C)[None, :]
    bm = jnp.asarray((k_head[:, None] == v_head[None, :]).astype(np.float32))
    hm = jnp.asarray((np.arange(RET_HEADS)[:, None] == k_head[None, :]).astype(np.float32), BF16)
    ret_tables = (hm, dstack, qd, kd, cd, bm)
    return in_consts, cos, sin, ret_tables, jnp.asarray(tri_strict, BF16)


def _pad_heads(g):
    depth = g.shape[0]
    g = g.reshape(depth, FOX_HEADS, FOX_DH)
    return jnp.pad(g, ((0, 0), (0, 0), (0, LANE - FOX_DH))).reshape(depth, 1, FOX_HEADS * LANE)


def kernel(x, mem, attn_norm_g, w_in, forget_bias, fox_q_norm_g, fox_k_norm_g, mem_q_norm_g, mem_k_norm_g,
           mem_norm_g, w_mem_kv, out_norm_g, w_out, ffn_norm_g, dense_w_gate, dense_w_up, dense_w_down,
           router_w, expert_w_gate, expert_w_up, expert_w_down):
    B, S, D = x.shape
    depth = w_in.shape[0]
    T = B * S
    in_consts, cos, sin, ret_tables, tri_strict = _constants(S)
    bd = in_consts[1]

    w_in_p = _reorder_in_proj(w_in)
    lane_pad = lambda g: jnp.pad(g, ((0, 0), (0, LANE - g.shape[-1])))[:, None, :]
    fqg = lane_pad(fox_q_norm_g)
    fkg = lane_pad(fox_k_norm_g)
    fb = lane_pad(forget_bias)
    mqg = jnp.tile(mem_q_norm_g, (1, MEM_HEADS))[:, None, :]
    mkg = jnp.tile(mem_k_norm_g, (1, MEM_HEADS))[:, None, :]
    n_ret = RET_HEADS * RET_DV
    n_fox = FOX_HEADS * FOX_DH
    g_ret = out_norm_g[:, None, :n_ret]
    g_fox = _pad_heads(out_norm_g[:, n_ret:n_ret + n_fox])
    g_mem = out_norm_g[:, None, n_ret + n_fox:]
    w_fox = w_out[:, n_ret:n_ret + n_fox].reshape(depth, FOX_HEADS, FOX_DH, D)
    w_fox = jnp.pad(w_fox, ((0, 0), (0, 0), (0, LANE - FOX_DH), (0, 0))).reshape(depth, FOX_HEADS * LANE, D)
    w_out_p = jnp.concatenate([w_out[:, :n_ret], w_fox, w_out[:, n_ret + n_fox:]], axis=1).astype(BF16)
    ffpad = D_FF_PAD - D_FF
    pad_cols = lambda w: jnp.pad(w, [(0, 0)] * (w.ndim - 1) + [(0, ffpad)]).astype(BF16)
    pad_rows = lambda w: jnp.pad(w, [(0, 0)] * (w.ndim - 2) + [(0, ffpad), (0, 0)]).astype(BF16)
    dwg, dwu, dwd = pad_cols(dense_w_gate), pad_cols(dense_w_up), pad_rows(dense_w_down)
    ewg, ewu, ewd = pad_cols(expert_w_gate), pad_cols(expert_w_up), pad_rows(expert_w_down)
    rw = jnp.pad(router_w, ((0, 0), (0, 0), (0, LANE - N_EXPERTS)))

    mkt, mvs = _mem_prep(mem, mem_norm_g[:, None, :], w_mem_kv.astype(BF16), mkg, bd)

    xt = x.reshape(T, D)
    for l in range(depth):
        rq, rk, rv, rg, fqt, fka, fvt, mq = _in_proj(
            xt, attn_norm_g[l][None], w_in_p[l], cos, sin, fqg[l], fkg[l], mqg[l], fb[l], in_consts, S)
        ret_o = _retention(rq, rk, rv, rg, ret_tables, g_ret[l], B, S)
        fox_o = _fox(fqt, fka, fvt, g_fox[l], B, S)
        j = l // 2
        if l % 2 == 0:
            xm, h2 = _out_proj(ret_o, fox_o, mq, mkt, mvs, bd, g_mem[l], w_out_p[l], xt, ffn_norm_g[l][None], l, S)
            xt = _dense_ffn(h2, dwg, dwu, dwd, xm, j)
        else:
            xm, h2, route, counts = _out_proj(ret_o, fox_o, mq, mkt, mvs, bd, g_mem[l], w_out_p[l], xt,
                                              ffn_norm_g[l][None], l, S, router=(rw[j], tri_strict))
            xt = _moe_layer(h2, xm, route, counts, ewg, ewu, ewd, j)
    return xt.reshape(B, S, D)
```

```python
import functools

import numpy as np
import jax
import jax.numpy as jnp
from jax import lax
from jax.experimental import pallas as pl
from jax.experimental.pallas import tpu as pltpu

F32 = jnp.float32
BF16 = jnp.bfloat16

D_MODEL = 1024
RET_HEADS = 4
RET_DK = 64
RET_DV = 128
FOX_HEADS = 4
FOX_DH = 64
MEM_HEADS = 4
MEM_DH = 64
RET_CHUNK = 128
D_FF = 2752
N_EXPERTS = 8
ROPE_BASE = 10000.0
EPS = 1e-6

LANE = 128
D_FF_PAD = 2816
FF_TILE = D_FF_PAD // 2
TOK_TILE = 512
FOX_Q_TILE = 1024
FOX_K_TILE = 512
LOG2E = 1.4426950408889634
MOE_TILE = 512
VMEM_LIMIT = 56 * 1024 * 1024
NEG = -1e30

RQ, RK, RV, RG = 0, 256, 512, 1024
FQ, FK, FV = 1536, 1792, 2048
FF_COL = 2304
MQ_OFF = FOX_HEADS
MQ_SLAB = 384
P_IN = FF_COL + MQ_SLAB
N_PIECE = 3
V_ONE = 64


def _split3(x):
    hi = x.astype(BF16)
    r1 = x - hi.astype(F32)
    mid = r1.astype(BF16)
    lo = (r1 - mid.astype(F32)).astype(BF16)
    return hi, mid, lo


def _group_mean_sq(x, bd):
    sq = x * x
    hi = sq.astype(BF16)
    lo = (sq - hi.astype(F32)).astype(BF16)
    return (jnp.dot(hi, bd, preferred_element_type=F32)
            + jnp.dot(lo, bd, preferred_element_type=F32))


def _silu(x):
    return x / (1.0 + jnp.exp(-x))


def _in_proj_kernel(tiles_per_seq, x_ref, g_ref, w_ref, cos_ref, sin_ref, fqg_ref, fkg_ref, mqg_ref,
                    fb_ref, tri_ref, bd_ref, bdm_ref,
                    rq_ref, rk_ref, rv_ref, rg_ref, fqt_ref, fka_ref, fvt_ref, mq_ref, carry_ref):
    i = pl.program_id(0)

    @pl.when(i % tiles_per_seq == 0)
    def _():
        carry_ref[...] = jnp.zeros_like(carry_ref)

    x = x_ref[...]
    ms = jnp.mean(x * x, axis=-1, keepdims=True)
    h = (x * lax.rsqrt(ms + EPS) * g_ref[...]).astype(BF16)

    def proj(a, b):
        return jnp.dot(h, w_ref[:, a:b], preferred_element_type=F32)

    tm = x.shape[0]
    cos = cos_ref[...]
    sin = sin_ref[...]
    first = (lax.broadcasted_iota(jnp.int32, (tm, LANE), 1) % RET_DK) < RET_DK // 2
    for base, ref, scale in ((RQ, rq_ref, RET_DK ** -0.5), (RK, rk_ref, 1.0)):
        for a in (0, LANE):
            t = proj(base + a, base + a + LANE)
            partner = jnp.where(first, pltpu.roll(t, LANE - RET_DK // 2, 1), pltpu.roll(t, RET_DK // 2, 1))
            ref[:, a:a + LANE] = ((t * cos + partner * sin) * scale).astype(BF16)
    rv_ref[...] = proj(RV, RV + 512).astype(BF16)
    rg_ref[...] = proj(RG, RG + 512).astype(BF16)

    zm = proj(FF_COL, FF_COL + MQ_SLAB)
    mq = zm * lax.rsqrt(_group_mean_sq(zm, bdm_ref[...]) + EPS) * mqg_ref[...] * (MEM_DH ** -0.5)
    mq_ref[...] = mq.astype(BF16)

    z = zm[:, 0:LANE].T[0:8, :] + fb_ref[...]
    logf = jnp.minimum(z, 0.0) - jnp.log(1.0 + jnp.exp(-jnp.abs(z)))
    hrow = lax.broadcasted_iota(jnp.int32, logf.shape, 0)
    logf = jnp.where(hrow < FOX_HEADS, logf, 0.0)
    tri = tri_ref[...]
    c = carry_ref[:, 0:1]
    for piece in _split3(logf):
        c = c + jnp.dot(piece, tri, preferred_element_type=F32)
    carry_ref[...] = jnp.broadcast_to(c[:, tm - 1:tm], carry_ref.shape)

    pieces = [p.astype(F32) for p in _split3(c * LOG2E)]
    bd = bd_ref[...]
    fq = proj(FQ, FQ + 256)
    fk = proj(FK, FK + 256)
    qt = (fq * lax.rsqrt(_group_mean_sq(fq, bd) + EPS) * fqg_ref[...] * (FOX_DH ** -0.5 * LOG2E)).T
    kt = (fk * lax.rsqrt(_group_mean_sq(fk, bd) + EPS) * fkg_ref[...]).T
    vt = proj(FV, FV + 256).T
    erow = lax.broadcasted_iota(jnp.int32, (8, tm), 0)
    zeros = jnp.zeros((LANE - FOX_DH - 8, tm), F32)
    ones_row = jnp.where(erow == V_ONE - FOX_DH, 1.0, 0.0)
    for hh in range(FOX_HEADS):
        feat = slice(hh * FOX_DH, (hh + 1) * FOX_DH)
        rows = slice(hh * LANE, (hh + 1) * LANE)
        eq = jnp.where(erow < 2 * N_PIECE, 1.0, 0.0)
        ek = eq
        for p in range(N_PIECE):
            c_row = pieces[p][hh:hh + 1, :]
            eq = jnp.where(erow == p, c_row, eq)
            ek = jnp.where(erow == N_PIECE + p, -c_row, ek)
        fqt_ref[rows, :] = jnp.concatenate([qt[feat], eq, zeros], axis=0).astype(BF16)
        fka_ref[:, rows] = jnp.concatenate([kt[feat], ek, zeros], axis=0).T.astype(BF16)
        fvt_ref[rows, :] = jnp.concatenate([vt[feat], ones_row, zeros], axis=0).astype(BF16)


def _in_proj(x, g, w, cos, sin, fqg, fkg, mqg, fb, consts, seq):
    T = x.shape[0]
    tm = TOK_TILE
    nt = T // tm
    tps = seq // tm
    row = lambda i: (i, 0)
    fix = lambda i: (0, 0)
    full = lambda a: pl.BlockSpec(a.shape, fix)
    tri, bd, bdm = consts
    widths = (256, 256, 512, 512, -512, 512, -512, MQ_SLAB)
    outs = [jax.ShapeDtypeStruct((T, n) if n > 0 else (-n, T), BF16) for n in widths]
    out_specs = [pl.BlockSpec((tm, n), row) if n > 0 else pl.BlockSpec((-n, tm), lambda i: (0, i))
                 for n in widths]
    return pl.pallas_call(
        functools.partial(_in_proj_kernel, tps),
        grid=(nt,),
        in_specs=[pl.BlockSpec((tm, D_MODEL), row), full(g), full(w),
                  pl.BlockSpec((tm, LANE), lambda i: (i % tps, 0)),
                  pl.BlockSpec((tm, LANE), lambda i: (i % tps, 0)),
                  full(fqg), full(fkg), full(mqg), full(fb), full(tri), full(bd), full(bdm)],
        out_specs=out_specs,
        out_shape=outs,
        scratch_shapes=[pltpu.VMEM((8, LANE), F32)],
        compiler_params=pltpu.CompilerParams(dimension_semantics=("arbitrary",),
                                             vmem_limit_bytes=VMEM_LIMIT),
        name="in_proj",
    )(x, g, w, cos, sin, fqg, fkg, mqg, fb, tri, bd, bdm)


def _retention_kernel(chunks, rq_ref, rk_ref, rv_ref, rg_ref, hm_ref, dst_ref, qd_ref, kd_ref, cd_ref,
                      bm_ref, g_ref, o_ref, state_ref):
    @pl.when(pl.program_id(1) == 0)
    def _():
        state_ref[...] = jnp.zeros_like(state_ref)

    C = RET_CHUNK
    for ci in range(chunks):
        rows = slice(ci * C, (ci + 1) * C)
        qc = rq_ref[rows, :]
        kc = rk_ref[rows, :]
        vc = rv_ref[rows, :]
        qs = jnp.concatenate([qc * hm_ref[hh:hh + 1, :] for hh in range(RET_HEADS)], axis=0)
        sc = lax.dot_general(qs, kc, (((1,), (1,)), ((), ())), preferred_element_type=F32)
        sc = (sc * dst_ref[...]).astype(BF16)
        state = state_ref[...]
        qdec = (qc.astype(F32) * qd_ref[...]).astype(BF16)
        cross = jnp.dot(qdec, state.astype(BF16), preferred_element_type=F32)
        kdec = (kc.astype(F32) * kd_ref[...]).astype(BF16)
        kv = lax.dot_general(kdec, vc, (((0,), (0,)), ((), ())), preferred_element_type=F32)
        state_ref[...] = state * cd_ref[...] + kv * bm_ref[...]
        for hh in range(RET_HEADS):
            cols = slice(hh * RET_DV, (hh + 1) * RET_DV)
            o = jnp.dot(sc[hh * C:(hh + 1) * C, :], vc[:, cols], preferred_element_type=F32) + cross[:, cols]
            msq = jnp.mean(o * o, axis=-1, keepdims=True)
            y = o * lax.rsqrt(msq + EPS) * g_ref[:, cols]
            o_ref[rows, cols] = (y * _silu(rg_ref[rows, cols].astype(F32))).astype(BF16)


def _retention(rq, rk, rv, rg, tables, g_ret, batch, seq):
    T = rq.shape[0]
    tm = TOK_TILE
    tps = seq // tm
    row = lambda b, i: (b * tps + i, 0)
    fix = lambda b, i: (0, 0)
    full = lambda a: pl.BlockSpec(a.shape, fix)
    return pl.pallas_call(
        functools.partial(_retention_kernel, tm // RET_CHUNK),
        grid=(batch, tps),
        in_specs=[pl.BlockSpec((tm, 256), row), pl.BlockSpec((tm, 256), row),
                  pl.BlockSpec((tm, 512), row), pl.BlockSpec((tm, 512), row)]
                 + [full(t) for t in tables] + [full(g_ret)],
        out_specs=pl.BlockSpec((tm, 512), row),
        out_shape=jax.ShapeDtypeStruct((T, 512), BF16),
        scratch_shapes=[pltpu.VMEM((RET_HEADS * RET_DK, RET_HEADS * RET_DV), F32)],
        compiler_params=pltpu.CompilerParams(dimension_semantics=("arbitrary", "arbitrary"),
                                             vmem_limit_bytes=VMEM_LIMIT),
        name="retention",
    )(rq, rk, rv, rg, *tables, g_ret)


def _fox_kernel(tq, tk, qt_ref, k_ref, vt_ref, g_ref, o_ref, m_ref, acc_ref, s0_ref, s1_ref):
    i = pl.program_id(2)
    qt = qt_ref[...]
    m_ref[...] = jnp.full_like(m_ref, NEG)
    acc_ref[...] = jnp.zeros_like(acc_ref)
    assert tq == 2 * tk
    s_refs = (s0_ref, s1_ref)

    def scores(j, slot):
        start = pl.multiple_of(j * tk, tk)
        s_refs[slot][...] = jnp.dot(k_ref[pl.ds(start, tk), :], qt, preferred_element_type=F32)

    def update(j, slot, diag):
        start = pl.multiple_of(j * tk, tk)
        st = s_refs[slot][...]
        if diag is not None:
            key = lax.broadcasted_iota(jnp.int32, st.shape, 0) + diag * tk
            qry = lax.broadcasted_iota(jnp.int32, st.shape, 1)
            st = jnp.where(key <= qry, st, NEG)
        m_prev = m_ref[...]
        m_new = jnp.maximum(m_prev, jnp.max(st, axis=0, keepdims=True))
        alpha = jnp.exp2(m_prev - m_new)
        p = jnp.exp2(st - m_new).astype(BF16)
        vt = vt_ref[:, pl.ds(start, tk)]
        acc_ref[...] = alpha * acc_ref[...] + jnp.dot(vt, p, preferred_element_type=F32)
        m_ref[...] = m_new

    scores(0, 0)

    def body(t, carry):
        scores(2 * t + 1, 1)
        update(2 * t, 0, None)
        scores(2 * t + 2, 0)
        update(2 * t + 1, 1, None)
        return carry

    lax.fori_loop(0, i, body, 0)
    scores(2 * i + 1, 1)
    update(2 * i, 0, 0)
    update(2 * i + 1, 1, 1)
    acc = acc_ref[...]
    row = lax.broadcasted_iota(jnp.int32, acc.shape, 0)
    o = jnp.where(row < FOX_DH, acc / acc[V_ONE:V_ONE + 1, :], 0.0)
    msq = jnp.sum(o * o, axis=0, keepdims=True) * (1.0 / FOX_DH)
    o_ref[...] = ((o * lax.rsqrt(msq + EPS)).T * g_ref[...]).astype(BF16)


def _fox(fqt, fka, fvt, g_fox, batch, seq):
    T = fka.shape[0]
    tq = FOX_Q_TILE
    tk = FOX_K_TILE
    nq = seq // tq
    return pl.pallas_call(
        functools.partial(_fox_kernel, tq, tk),
        grid=(batch, FOX_HEADS, nq),
        in_specs=[pl.BlockSpec((LANE, tq), lambda b, h, i: (h, b * nq + i)),
                  pl.BlockSpec((seq, LANE), lambda b, h, i: (b, h)),
                  pl.BlockSpec((LANE, seq), lambda b, h, i: (h, b)),
                  pl.BlockSpec((1, LANE), lambda b, h, i: (0, h))],
        out_specs=pl.BlockSpec((tq, LANE), lambda b, h, i: (b * nq + i, h)),
        out_shape=jax.ShapeDtypeStruct((T, FOX_HEADS * LANE), BF16),
        scratch_shapes=[pltpu.VMEM((1, tq), F32), pltpu.VMEM((LANE, tq), F32),
                        pltpu.VMEM((tk, tq), F32), pltpu.VMEM((tk, tq), F32)],
        compiler_params=pltpu.CompilerParams(dimension_semantics=("arbitrary", "arbitrary", "arbitrary"),
                                             vmem_limit_bytes=VMEM_LIMIT),
        name="fox",
    )(fqt, fka, fvt, g_fox)


def _mem_prep_kernel(mem_ref, g_ref, w_ref, kg_ref, bd_ref, shift_ref, mkt_ref, mvs_ref):
    x = mem_ref[0]
    ms = jnp.mean(x * x, axis=-1, keepdims=True)
    mn = (x * lax.rsqrt(ms + EPS) * g_ref[0]).astype(BF16)
    mkv = jnp.dot(mn, w_ref[0], preferred_element_type=F32)
    width = MEM_HEADS * MEM_DH
    mk = mkv[:, :width]
    mv = mkv[:, width:]
    mk = mk * lax.rsqrt(_group_mean_sq(mk, bd_ref[...]) + EPS) * kg_ref[0]
    mk = jnp.dot(mk.astype(BF16), shift_ref[...], preferred_element_type=F32)
    mkt = mk.T
    feat = (lax.broadcasted_iota(jnp.int32, mkt.shape, 0) - MQ_OFF) // MEM_DH
    lane_head = lax.broadcasted_iota(jnp.int32, mv.shape, 1) // MEM_DH
    M = x.shape[0]
    for hh in range(MEM_HEADS):
        mkt_ref[0, 0, :, hh * M:(hh + 1) * M] = jnp.where(feat == hh, mkt, 0.0).astype(BF16)
        mvs_ref[0, 0, hh * M:(hh + 1) * M, :] = jnp.where(lane_head == hh, mv, 0.0).astype(BF16)


def _mem_prep(mem, mem_norm_g, w_mem_kv, mem_k_g, bd, shift):
    B, M, _ = mem.shape
    depth = w_mem_kv.shape[0]
    width = MEM_HEADS * MEM_DH
    return pl.pallas_call(
        _mem_prep_kernel,
        grid=(depth, B),
        in_specs=[pl.BlockSpec((1, M, D_MODEL), lambda l, b: (b, 0, 0)),
                  pl.BlockSpec((1, 1, D_MODEL), lambda l, b: (l, 0, 0)),
                  pl.BlockSpec((1, D_MODEL, 2 * width), lambda l, b: (l, 0, 0)),
                  pl.BlockSpec((1, 1, width), lambda l, b: (l, 0, 0)),
                  pl.BlockSpec(bd.shape, lambda l, b: (0, 0)),
                  pl.BlockSpec(shift.shape, lambda l, b: (0, 0))],
        out_specs=[pl.BlockSpec((1, 1, MQ_SLAB, MEM_HEADS * M), lambda l, b: (l, b, 0, 0)),
                   pl.BlockSpec((1, 1, MEM_HEADS * M, width), lambda l, b: (l, b, 0, 0))],
        out_shape=[jax.ShapeDtypeStruct((depth, B, MQ_SLAB, MEM_HEADS * M), BF16),
                   jax.ShapeDtypeStruct((depth, B, MEM_HEADS * M, width), BF16)],
        compiler_params=pltpu.CompilerParams(dimension_semantics=("arbitrary", "arbitrary"),
                                             vmem_limit_bytes=VMEM_LIMIT),
        name="mem_prep",
    )(mem, mem_norm_g, w_mem_kv, mem_k_g, bd, shift)


def _out_proj_body(ret_ref, fox_ref, mq_ref, mkt_ref, mvs_ref, bd_ref, gm_ref, w_ref, x_ref, fg_ref):
    M = mkt_ref.shape[-1] // MEM_HEADS
    s = jnp.dot(mq_ref[...], mkt_ref[0, 0], preferred_element_type=F32)
    mem_o = None
    for hh in range(MEM_HEADS):
        sh = s[:, hh * M:(hh + 1) * M]
        p = jnp.exp(sh - jnp.max(sh, axis=-1, keepdims=True))
        p = (p / jnp.sum(p, axis=-1, keepdims=True)).astype(BF16)
        part = jnp.dot(p, mvs_ref[0, 0, hh * M:(hh + 1) * M, :], preferred_element_type=F32)
        mem_o = part if mem_o is None else mem_o + part
    mem_o = (mem_o * lax.rsqrt(_group_mean_sq(mem_o, bd_ref[...]) + EPS) * gm_ref[...]).astype(BF16)
    y = (jnp.dot(ret_ref[...], w_ref[0:512, :], preferred_element_type=F32)
         + jnp.dot(fox_ref[...], w_ref[512:1024, :], preferred_element_type=F32)
         + jnp.dot(mem_o, w_ref[1024:1280, :], preferred_element_type=F32))
    xm = x_ref[...] + y
    ms = jnp.mean(xm * xm, axis=-1, keepdims=True)
    return xm, xm * lax.rsqrt(ms + EPS) * fg_ref[...]


def _out_proj_dense_kernel(ret_ref, fox_ref, mq_ref, mkt_ref, mvs_ref, bd_ref, gm_ref, w_ref, x_ref, fg_ref,
                           xm_ref, h_ref):
    xm, hn = _out_proj_body(ret_ref, fox_ref, mq_ref, mkt_ref, mvs_ref, bd_ref, gm_ref, w_ref, x_ref, fg_ref)
    xm_ref[...] = xm
    h_ref[...] = hn.astype(BF16)


def _out_proj_moe_kernel(ret_ref, fox_ref, mq_ref, mkt_ref, mvs_ref, bd_ref, gm_ref, w_ref, x_ref, fg_ref,
                         rw_ref, tri_ref, xm_ref, h_ref, route_ref, cnt_ref, carry_ref):
    @pl.when(pl.program_id(0) == 0)
    def _():
        carry_ref[...] = jnp.zeros_like(carry_ref)

    xm, hn = _out_proj_body(ret_ref, fox_ref, mq_ref, mkt_ref, mvs_ref, bd_ref, gm_ref, w_ref, x_ref, fg_ref)
    xm_ref[...] = xm
    h_ref[...] = hn
    rw = rw_ref[...]
    h_hi = hn.astype(BF16)
    h_lo = (hn - h_hi.astype(F32)).astype(BF16)
    w_hi = rw.astype(BF16)
    w_lo = (rw - w_hi.astype(F32)).astype(BF16)
    logits = (jnp.dot(h_hi, w_hi, preferred_element_type=F32) + jnp.dot(h_lo, w_hi, preferred_element_type=F32)
              + jnp.dot(h_hi, w_lo, preferred_element_type=F32))
    lane = lax.broadcasted_iota(jnp.int32, logits.shape, 1)
    lanef = lane.astype(F32)
    lg = jnp.where(lane < N_EXPERTS, logits, NEG)
    m1 = jnp.max(lg, axis=-1, keepdims=True)
    i1 = jnp.min(jnp.where(lg == m1, lanef, float(LANE)), axis=-1, keepdims=True)
    oh1 = lanef == i1
    lg2 = jnp.where(oh1, NEG, lg)
    m2 = jnp.max(lg2, axis=-1, keepdims=True)
    i2 = jnp.min(jnp.where(lg2 == m2, lanef, float(LANE)), axis=-1, keepdims=True)
    oh2 = lanef == i2
    e = jnp.exp(m2 - m1)
    g1 = 1.0 / (1.0 + e)
    g2 = e / (1.0 + e)
    oh = jnp.where(oh1 | oh2, 1.0, 0.0)
    before = carry_ref[0:1, :] + jnp.dot(tri_ref[...], oh.astype(BF16), preferred_element_type=F32)
    r1 = jnp.sum(jnp.where(oh1, before, 0.0), axis=-1, keepdims=True)
    r2 = jnp.sum(jnp.where(oh2, before, 0.0), axis=-1, keepdims=True)
    tm = oh.shape[0]
    total = before[tm - 1:tm, :] + oh[tm - 1:tm, :]
    carry_ref[0:1, :] = total
    cnt_ref[...] = jnp.broadcast_to(total, cnt_ref.shape)
    route = jnp.zeros_like(logits)
    for col, val in enumerate((i1, i2, r1, r2, g1, g2)):
        route = jnp.where(lane == col, val, route)
    route_ref[...] = route


def _out_proj(ret_o, fox_o, mq, mkt, mvs, bd, g_mem, w_out, x, ffn_g, layer, seq, router=None):
    T = x.shape[0]
    tm = TOK_TILE
    tps = seq // tm
    row = lambda i: (i, 0)
    fix = lambda i: (0, 0)
    full = lambda a: pl.BlockSpec(a.shape, fix)
    M4 = mkt.shape[-1]
    in_specs = [pl.BlockSpec((tm, 512), row), pl.BlockSpec((tm, 512), row), pl.BlockSpec((tm, MQ_SLAB), row),
                pl.BlockSpec((1, 1, MQ_SLAB, M4), lambda i: (layer, i // tps, 0, 0)),
                pl.BlockSpec((1, 1, M4, 256), lambda i: (layer, i // tps, 0, 0)),
                full(bd), full(g_mem), full(w_out), pl.BlockSpec((tm, D_MODEL), row), full(ffn_g)]
    args = [ret_o, fox_o, mq, mkt, mvs, bd, g_mem, w_out, x, ffn_g]
    params = pltpu.CompilerParams(dimension_semantics=("arbitrary",), vmem_limit_bytes=VMEM_LIMIT)
    if router is None:
        return pl.pallas_call(
            _out_proj_dense_kernel, grid=(T // tm,), in_specs=in_specs,
            out_specs=[pl.BlockSpec((tm, D_MODEL), row), pl.BlockSpec((tm, D_MODEL), row)],
            out_shape=[jax.ShapeDtypeStruct((T, D_MODEL), F32), jax.ShapeDtypeStruct((T, D_MODEL), BF16)],
            compiler_params=params, name="out_proj_dense",
        )(*args)
    rw, tri = router
    return pl.pallas_call(
        _out_proj_moe_kernel, grid=(T // tm,), in_specs=in_specs + [full(rw), full(tri)],
        out_specs=[pl.BlockSpec((tm, D_MODEL), row), pl.BlockSpec((tm, D_MODEL), row),
                   pl.BlockSpec((tm, LANE), row), pl.BlockSpec((8, LANE), fix)],
        out_shape=[jax.ShapeDtypeStruct((T, D_MODEL), F32), jax.ShapeDtypeStruct((T, D_MODEL), F32),
                   jax.ShapeDtypeStruct((T, LANE), F32), jax.ShapeDtypeStruct((8, LANE), F32)],
        scratch_shapes=[pltpu.VMEM((8, LANE), F32)],
        compiler_params=params, name="out_proj_moe",
    )(*args, rw, tri)


def _swiglu(x, wg_ref, wu_ref, wd_ref, y):
    for c in range(D_FF_PAD // FF_TILE):
        cols = slice(c * FF_TILE, (c + 1) * FF_TILE)
        g = jnp.dot(x, wg_ref[:, cols], preferred_element_type=F32)
        u = jnp.dot(x, wu_ref[:, cols], preferred_element_type=F32)
        a = (_silu(g) * u).astype(BF16)
        part = jnp.dot(a, wd_ref[cols, :], preferred_element_type=F32)
        y = part if y is None else y + part
    return y


def _dense_ffn_kernel(h_ref, wg_ref, wu_ref, wd_ref, x_ref, o_ref):
    o_ref[...] = _swiglu(h_ref[...], wg_ref.at[0], wu_ref.at[0], wd_ref.at[0], x_ref[...])


def _dense_ffn(h, wg, wu, wd, x, layer):
    T = h.shape[0]
    tm = TOK_TILE
    once = pl.Buffered(1)
    return pl.pallas_call(
        _dense_ffn_kernel,
        grid=(T // tm,),
        in_specs=[pl.BlockSpec((tm, D_MODEL), lambda i: (i, 0)),
                  pl.BlockSpec((1, D_MODEL, D_FF_PAD), lambda i: (layer, 0, 0), pipeline_mode=once),
                  pl.BlockSpec((1, D_MODEL, D_FF_PAD), lambda i: (layer, 0, 0), pipeline_mode=once),
                  pl.BlockSpec((1, D_FF_PAD, D_MODEL), lambda i: (layer, 0, 0), pipeline_mode=once),
                  pl.BlockSpec((tm, D_MODEL), lambda i: (i, 0))],
        out_specs=pl.BlockSpec((tm, D_MODEL), lambda i: (i, 0)),
        out_shape=jax.ShapeDtypeStruct((T, D_MODEL), F32),
        compiler_params=pltpu.CompilerParams(dimension_semantics=("arbitrary",),
                                             vmem_limit_bytes=VMEM_LIMIT),
        name="dense_ffn",
    )(h, wg, wu, wd, x)


def _row_copy(src, s, dst, d, sem):
    return pltpu.make_async_copy(src.at[pl.ds(s, 1)], dst.at[pl.ds(d, 1)], sem)


def _moe_ffn_kernel(te_ref, nu_ref, src_ref, h_hbm, wg_ref, wu_ref, wd_ref, o_ref, x_ref, sem):
    i = pl.program_id(0)
    n_used = nu_ref[0]
    tm = x_ref.shape[1]
    slot = i % 2
    nxt = 1 - slot

    def tile_wait(s):
        pltpu.make_async_copy(h_hbm.at[pl.ds(0, tm)], x_ref.at[s], sem.at[s]).wait()

    @pl.when(i == 0)
    def _():
        def issue(r, carry):
            _row_copy(h_hbm, src_ref[r], x_ref.at[0], r, sem.at[0]).start()
            return carry

        lax.fori_loop(0, tm, issue, 0, unroll=8)

    @pl.when(i < n_used)
    def _():
        tile_wait(slot)
        x = x_ref[slot].astype(BF16)
        base = (i + 1) * tm
        for r in range(tm):
            _row_copy(h_hbm, src_ref[base + r], x_ref.at[nxt], r, sem.at[nxt]).start(priority=r % 2)
        o_ref[...] = _swiglu(x, wg_ref.at[0, 0], wu_ref.at[0, 0], wd_ref.at[0, 0], None)

        @pl.when(i + 1 == n_used)
        def _():
            tile_wait(nxt)

    @pl.when(i >= n_used)
    def _():
        o_ref[...] = jnp.zeros_like(o_ref)


def _moe_ffn(h, src, tile_expert, n_used, wg, wu, wd, layer):
    tm = MOE_TILE
    n_tiles = tile_expert.shape[0]
    assert src.shape[0] == (n_tiles + 1) * tm
    w_block = lambda i, te, nu, sr: (layer, te[i], 0, 0)
    return pl.pallas_call(
        _moe_ffn_kernel,
        grid_spec=pltpu.PrefetchScalarGridSpec(
            num_scalar_prefetch=3, grid=(n_tiles,),
            in_specs=[pl.BlockSpec(memory_space=pl.ANY),
                      pl.BlockSpec((1, 1, D_MODEL, D_FF_PAD), w_block),
                      pl.BlockSpec((1, 1, D_MODEL, D_FF_PAD), w_block),
                      pl.BlockSpec((1, 1, D_FF_PAD, D_MODEL), w_block)],
            out_specs=pl.BlockSpec((tm, D_MODEL), lambda i, te, nu, sr: (i, 0)),
            scratch_shapes=[pltpu.VMEM((2, tm, D_MODEL), F32), pltpu.SemaphoreType.DMA((2,))]),
        out_shape=jax.ShapeDtypeStruct((n_tiles * tm, D_MODEL), F32),
        compiler_params=pltpu.CompilerParams(dimension_semantics=("arbitrary",),
                                             vmem_limit_bytes=VMEM_LIMIT),
        name="moe_ffn",
    )(tile_expert, n_used, src, h, wg, wu, wd)


def _combine_kernel(tc, d1_ref, d2_ref, o_hbm, x_ref, route_ref, out_ref, b1_ref, b2_ref, sem):
    base = pl.program_id(0) * tc

    def issue(r, carry):
        t = base + r
        _row_copy(o_hbm, d1_ref[t], b1_ref, r, sem.at[0]).start(priority=0)
        _row_copy(o_hbm, d2_ref[t], b2_ref, r, sem.at[1]).start(priority=1)
        return carry

    lax.fori_loop(0, tc, issue, 0, unroll=8)
    pltpu.make_async_copy(o_hbm.at[pl.ds(0, tc)], b1_ref, sem.at[0]).wait()
    pltpu.make_async_copy(o_hbm.at[pl.ds(0, tc)], b2_ref, sem.at[1]).wait()
    route = route_ref[...]
    g1 = route[:, 4:5]
    g2 = route[:, 5:6]
    out_ref[...] = x_ref[...] + g1 * b1_ref[...] + g2 * b2_ref[...]


def _combine(o, d1, d2, x, route):
    T = x.shape[0]
    tc = 256
    return pl.pallas_call(
        functools.partial(_combine_kernel, tc),
        grid_spec=pltpu.PrefetchScalarGridSpec(
            num_scalar_prefetch=2, grid=(T // tc,),
            in_specs=[pl.BlockSpec(memory_space=pl.ANY),
                      pl.BlockSpec((tc, D_MODEL), lambda i, a, b: (i, 0)),
                      pl.BlockSpec((tc, LANE), lambda i, a, b: (i, 0))],
            out_specs=pl.BlockSpec((tc, D_MODEL), lambda i, a, b: (i, 0)),
            scratch_shapes=[pltpu.VMEM((tc, D_MODEL), F32), pltpu.VMEM((tc, D_MODEL), F32),
                            pltpu.SemaphoreType.DMA((2,))]),
        out_shape=jax.ShapeDtypeStruct((T, D_MODEL), F32),
        compiler_params=pltpu.CompilerParams(dimension_semantics=("arbitrary",),
                                             vmem_limit_bytes=VMEM_LIMIT),
        name="moe_combine",
    )(d1, d2, o, x, route)


def _moe_layer(h, xm, route, counts, wg, wu, wd, layer):
    T = h.shape[0]
    tm = MOE_TILE
    n_tiles = (2 * T) // tm + N_EXPERTS
    e1 = route[:, 0].astype(jnp.int32)
    e2 = route[:, 1].astype(jnp.int32)
    r1 = route[:, 2].astype(jnp.int32)
    r2 = route[:, 3].astype(jnp.int32)
    cnt = counts[0, :N_EXPERTS].astype(jnp.int32)
    tiles = (cnt + tm - 1) // tm
    tile_end = jnp.cumsum(tiles)
    offs = (tile_end - tiles) * tm
    d1 = offs[e1] + r1
    d2 = offs[e2] + r2
    n_used = tile_end[-1:]
    tile_id = jnp.arange(n_tiles, dtype=jnp.int32)
    tile_expert = jnp.sum(tile_id[:, None] >= tile_end[None, :], axis=1).astype(jnp.int32)
    tile_expert = jnp.minimum(tile_expert, tile_expert[jnp.maximum(n_used[0] - 1, 0)])
    tok = jnp.arange(T, dtype=jnp.int32)
    src = jnp.zeros(((n_tiles + 1) * tm,), jnp.int32)
    src = src.at[jnp.concatenate([d1, d2])].set(jnp.concatenate([tok, tok]), unique_indices=True)
    o = _moe_ffn(h, src, tile_expert, n_used.astype(jnp.int32), wg, wu, wd, layer)
    return _combine(o, d1, d2, xm, route)


def _constants(seq):
    tm = TOK_TILE
    r = np.arange(tm)
    tri_upper = (r[:, None] <= r[None, :]).astype(np.float32)
    tri_strict = (r[:, None] > r[None, :]).astype(np.float32)
    d = np.arange(256)
    bd = (d[:, None] // 64 == d[None, :] // 64).astype(np.float32) / 64.0
    m = np.arange(MQ_SLAB) - MQ_OFF
    valid = (m >= 0) & (m < MEM_HEADS * MEM_DH)
    bdm = ((m[:, None] // MEM_DH == m[None, :] // MEM_DH) & valid[:, None] & valid[None, :]).astype(np.float32) / MEM_DH
    shift = (d[:, None] + MQ_OFF == np.arange(MQ_SLAB)[None, :]).astype(np.float32)
    in_consts = (jnp.asarray(tri_upper, BF16), jnp.asarray(bd, BF16), jnp.asarray(bdm, BF16))

    half = RET_DK // 2
    pos = jnp.arange(seq, dtype=F32)
    inv_freq = ROPE_BASE ** (-jnp.arange(half, dtype=F32) / half)
    ang = pos[:, None] * inv_freq[None, :]
    cos = jnp.tile(jnp.cos(ang), (1, LANE // half))
    sin = jnp.tile(jnp.concatenate([-jnp.sin(ang), jnp.sin(ang)], axis=1), (1, LANE // RET_DK))

    C = RET_CHUNK
    hh = jnp.arange(RET_HEADS, dtype=F32)
    log_g = jnp.log(1.0 - 2.0 ** (-5.0 - hh))
    idx = jnp.arange(C, dtype=F32)
    diff = idx[:, None] - idx[None, :]
    dmat = jnp.where(diff >= 0, jnp.exp(log_g[:, None, None] * jnp.maximum(diff, 0.0)), 0.0)
    dstack = dmat.reshape(RET_HEADS * C, C)
    k_head = np.arange(RET_HEADS * RET_DK) // RET_DK
    v_head = np.arange(RET_HEADS * RET_DV) // RET_DV
    qd = jnp.exp(log_g[k_head][None, :] * (idx[:, None] + 1.0))
    kd = jnp.exp(log_g[k_head][None, :] * (C - 1.0 - idx[:, None]))
    cd = jnp.exp(log_g[v_head] * C)[None, :]
    bm = jnp.asarray((k_head[:, None] == v_head[None, :]).astype(np.float32))
    hm = jnp.asarray((np.arange(RET_HEADS)[:, None] == k_head[None, :]).astype(np.float32), BF16)
    ret_tables = (hm, dstack, qd, kd, cd, bm)
    return in_consts, cos, sin, ret_tables, jnp.asarray(tri_strict, BF16), jnp.asarray(shift, BF16)


def _pad_heads(g):
    depth = g.shape[0]
    g = g.reshape(depth, FOX_HEADS, FOX_DH)
    return jnp.pad(g, ((0, 0), (0, 0), (0, LANE - FOX_DH))).reshape(depth, 1, FOX_HEADS * LANE)


def kernel(x, mem, attn_norm_g, w_in, forget_bias, fox_q_norm_g, fox_k_norm_g, mem_q_norm_g, mem_k_norm_g,
           mem_norm_g, w_mem_kv, out_norm_g, w_out, ffn_norm_g, dense_w_gate, dense_w_up, dense_w_down,
           router_w, expert_w_gate, expert_w_up, expert_w_down):
    B, S, D = x.shape
    depth = w_in.shape[0]
    T = B * S
    in_consts, cos, sin, ret_tables, tri_strict, shift = _constants(S)
    bd = in_consts[1]

    w_in_p = jnp.pad(w_in, ((0, 0), (0, 0), (0, P_IN - w_in.shape[-1]))).astype(BF16)
    fqg = jnp.tile(fox_q_norm_g, (1, FOX_HEADS))[:, None, :]
    fkg = jnp.tile(fox_k_norm_g, (1, FOX_HEADS))[:, None, :]
    fb = jnp.pad(forget_bias, ((0, 0), (0, 8 - FOX_HEADS)))[:, :, None]
    mqg = jnp.tile(mem_q_norm_g, (1, MEM_HEADS))
    mqg = jnp.pad(mqg, ((0, 0), (MQ_OFF, MQ_SLAB - MQ_OFF - mqg.shape[-1])))[:, None, :]
    mkg = jnp.tile(mem_k_norm_g, (1, MEM_HEADS))[:, None, :]
    n_ret = RET_HEADS * RET_DV
    n_fox = FOX_HEADS * FOX_DH
    g_ret = out_norm_g[:, None, :n_ret]
    g_fox = _pad_heads(out_norm_g[:, n_ret:n_ret + n_fox])
    g_mem = out_norm_g[:, None, n_ret + n_fox:]
    w_fox = w_out[:, n_ret:n_ret + n_fox].reshape(depth, FOX_HEADS, FOX_DH, D)
    w_fox = jnp.pad(w_fox, ((0, 0), (0, 0), (0, LANE - FOX_DH), (0, 0))).reshape(depth, FOX_HEADS * LANE, D)
    w_out_p = jnp.concatenate([w_out[:, :n_ret], w_fox, w_out[:, n_ret + n_fox:]], axis=1).astype(BF16)
    ffpad = D_FF_PAD - D_FF
    pad_cols = lambda w: jnp.pad(w, [(0, 0)] * (w.ndim - 1) + [(0, ffpad)]).astype(BF16)
    pad_rows = lambda w: jnp.pad(w, [(0, 0)] * (w.ndim - 2) + [(0, ffpad), (0, 0)]).astype(BF16)
    dwg, dwu, dwd = pad_cols(dense_w_gate), pad_cols(dense_w_up), pad_rows(dense_w_down)
    ewg, ewu, ewd = pad_cols(expert_w_gate), pad_cols(expert_w_up), pad_rows(expert_w_down)
    rw = jnp.pad(router_w, ((0, 0), (0, 0), (0, LANE - N_EXPERTS)))

    mkt, mvs = _mem_prep(mem, mem_norm_g[:, None, :], w_mem_kv.astype(BF16), mkg, bd, shift)

    xt = x.reshape(T, D)
    for l in range(depth):
        rq, rk, rv, rg, fqt, fka, fvt, mq = _in_proj(
            xt, attn_norm_g[l][None], w_in_p[l], cos, sin, fqg[l], fkg[l], mqg[l], fb[l], in_consts, S)
        ret_o = _retention(rq, rk, rv, rg, ret_tables, g_ret[l], B, S)
        fox_o = _fox(fqt, fka, fvt, g_fox[l], B, S)
        j = l // 2
        if l % 2 == 0:
            xm, h2 = _out_proj(ret_o, fox_o, mq, mkt, mvs, bd, g_mem[l], w_out_p[l], xt, ffn_norm_g[l][None], l, S)
            xt = _dense_ffn(h2, dwg, dwu, dwd, xm, j)
        else:
            xm, h2, route, counts = _out_proj(ret_o, fox_o, mq, mkt, mvs, bd, g_mem[l], w_out_p[l], xt,
                                              ffn_norm_g[l][None], l, S, router=(rw[j], tri_strict))
            xt = _moe_layer(h2, xm, route, counts, ewg, ewu, ewd, j)
    return xt.reshape(B, S, D)
```

```python
import functools

import numpy as np
import jax
import jax.numpy as jnp
from jax import lax
from jax.experimental import pallas as pl
from jax.experimental.pallas import tpu as pltpu

F32 = jnp.float32
BF16 = jnp.bfloat16

D_MODEL = 1024
RET_HEADS = 4
RET_DK = 64
RET_DV = 128
FOX_HEADS = 4
FOX_DH = 64
MEM_HEADS = 4
MEM_DH = 64
RET_CHUNK = 128
D_FF = 2752
N_EXPERTS = 8
ROPE_BASE = 10000.0
EPS = 1e-6

LANE = 128
D_FF_PAD = 2816
FF_TILE = D_FF_PAD // 2
TOK_TILE = 512
FOX_Q_TILE = 1024
FOX_K_TILE = 512
LOG2E = 1.4426950408889634
MOE_TILE = 512
VMEM_LIMIT = 56 * 1024 * 1024
NEG = -1e30

RQ, RK, RV, RG = 0, 256, 512, 1024
FQ, FK, FV = 1536, 1792, 2048
FF_COL = 2304
MQ_OFF = FOX_HEADS
MQ_SLAB = 384
P_IN = FF_COL + MQ_SLAB
N_PIECE = 3
V_ONE = 64


def _split3(x):
    hi = x.astype(BF16)
    r1 = x - hi.astype(F32)
    mid = r1.astype(BF16)
    lo = (r1 - mid.astype(F32)).astype(BF16)
    return hi, mid, lo


def _group_mean_sq(x, bd):
    sq = x * x
    hi = sq.astype(BF16)
    lo = (sq - hi.astype(F32)).astype(BF16)
    return (jnp.dot(hi, bd, preferred_element_type=F32)
            + jnp.dot(lo, bd, preferred_element_type=F32))


def _silu(x):
    return x / (1.0 + jnp.exp(-x))


def _in_proj_kernel(tiles_per_seq, x_ref, g_ref, w_ref, cos_ref, sin_ref, fqg_ref, fkg_ref, mqg_ref,
                    fb_ref, tri_ref, bd_ref, bdm_ref,
                    rq_ref, rk_ref, rv_ref, rg_ref, fqt_ref, fka_ref, fvt_ref, mq_ref, carry_ref):
    i = pl.program_id(0)

    @pl.when(i % tiles_per_seq == 0)
    def _():
        carry_ref[...] = jnp.zeros_like(carry_ref)

    x = x_ref[...]
    ms = jnp.mean(x * x, axis=-1, keepdims=True)
    h = (x * lax.rsqrt(ms + EPS) * g_ref[...]).astype(BF16)

    def proj(a, b):
        return jnp.dot(h, w_ref[:, a:b], preferred_element_type=F32)

    tm = x.shape[0]
    cos = cos_ref[...]
    sin = sin_ref[...]
    first = (lax.broadcasted_iota(jnp.int32, (tm, LANE), 1) % RET_DK) < RET_DK // 2
    for base, ref, scale in ((RQ, rq_ref, RET_DK ** -0.5), (RK, rk_ref, 1.0)):
        for a in (0, LANE):
            t = proj(base + a, base + a + LANE)
            partner = jnp.where(first, pltpu.roll(t, LANE - RET_DK // 2, 1), pltpu.roll(t, RET_DK // 2, 1))
            ref[:, a:a + LANE] = ((t * cos + partner * sin) * scale).astype(BF16)
    rv_ref[...] = proj(RV, RV + 512).astype(BF16)
    rg_ref[...] = proj(RG, RG + 512).astype(BF16)

    zm = proj(FF_COL, FF_COL + MQ_SLAB)
    mq = zm * lax.rsqrt(_group_mean_sq(zm, bdm_ref[...]) + EPS) * mqg_ref[...] * (MEM_DH ** -0.5)
    mq_ref[...] = mq.astype(BF16)

    z = zm[:, 0:LANE].T[0:8, :] + fb_ref[...]
    logf = jnp.minimum(z, 0.0) - jnp.log(1.0 + jnp.exp(-jnp.abs(z)))
    hrow = lax.broadcasted_iota(jnp.int32, logf.shape, 0)
    logf = jnp.where(hrow < FOX_HEADS, logf, 0.0)
    tri = tri_ref[...]
    c = carry_ref[:, 0:1]
    for piece in _split3(logf):
        c = c + jnp.dot(piece, tri, preferred_element_type=F32)
    carry_ref[...] = jnp.broadcast_to(c[:, tm - 1:tm], carry_ref.shape)

    pieces = [p.astype(F32) for p in _split3(c * LOG2E)]
    bd = bd_ref[...]
    fq = proj(FQ, FQ + 256)
    fk = proj(FK, FK + 256)
    qt = (fq * lax.rsqrt(_group_mean_sq(fq, bd) + EPS) * fqg_ref[...] * (FOX_DH ** -0.5 * LOG2E)).T
    kt = (fk * lax.rsqrt(_group_mean_sq(fk, bd) + EPS) * fkg_ref[...]).T
    vt = proj(FV, FV + 256).T
    erow = lax.broadcasted_iota(jnp.int32, (8, tm), 0)
    zeros = jnp.zeros((LANE - FOX_DH - 8, tm), F32)
    ones_row = jnp.where(erow == V_ONE - FOX_DH, 1.0, 0.0)
    for hh in range(FOX_HEADS):
        feat = slice(hh * FOX_DH, (hh + 1) * FOX_DH)
        rows = slice(hh * LANE, (hh + 1) * LANE)
        eq = jnp.where(erow < 2 * N_PIECE, 1.0, 0.0)
        ek = eq
        for p in range(N_PIECE):
            c_row = pieces[p][hh:hh + 1, :]
            eq = jnp.where(erow == p, c_row, eq)
            ek = jnp.where(erow == N_PIECE + p, -c_row, ek)
        fqt_ref[rows, :] = jnp.concatenate([qt[feat], eq, zeros], axis=0).astype(BF16)
        fka_ref[:, rows] = jnp.concatenate([kt[feat], ek, zeros], axis=0).T.astype(BF16)
        fvt_ref[rows, :] = jnp.concatenate([vt[feat], ones_row, zeros], axis=0).astype(BF16)


def _in_proj(x, g, w, cos, sin, fqg, fkg, mqg, fb, consts, seq):
    T = x.shape[0]
    tm = TOK_TILE
    nt = T // tm
    tps = seq // tm
    row = lambda i: (i, 0)
    fix = lambda i: (0, 0)
    full = lambda a: pl.BlockSpec(a.shape, fix)
    tri, bd, bdm = consts
    widths = (256, 256, 512, 512, -512, 512, -512, MQ_SLAB)
    outs = [jax.ShapeDtypeStruct((T, n) if n > 0 else (-n, T), BF16) for n in widths]
    out_specs = [pl.BlockSpec((tm, n), row) if n > 0 else pl.BlockSpec((-n, tm), lambda i: (0, i))
                 for n in widths]
    return pl.pallas_call(
        functools.partial(_in_proj_kernel, tps),
        grid=(nt,),
        in_specs=[pl.BlockSpec((tm, D_MODEL), row), full(g), full(w),
                  pl.BlockSpec((tm, LANE), lambda i: (i % tps, 0)),
                  pl.BlockSpec((tm, LANE), lambda i: (i % tps, 0)),
                  full(fqg), full(fkg), full(mqg), full(fb), full(tri), full(bd), full(bdm)],
        out_specs=out_specs,
        out_shape=outs,
        scratch_shapes=[pltpu.VMEM((8, LANE), F32)],
        compiler_params=pltpu.CompilerParams(dimension_semantics=("arbitrary",),
                                             vmem_limit_bytes=VMEM_LIMIT),
        name="in_proj",
    )(x, g, w, cos, sin, fqg, fkg, mqg, fb, tri, bd, bdm)


def _retention_kernel(chunks, rq_ref, rk_ref, rv_ref, rg_ref, hm_ref, dst_ref, qd_ref, kd_ref, cd_ref,
                      bm_ref, g_ref, o_ref, state_ref):
    @pl.when(pl.program_id(1) == 0)
    def _():
        state_ref[...] = jnp.zeros_like(state_ref)

    C = RET_CHUNK
    for ci in range(chunks):
        rows = slice(ci * C, (ci + 1) * C)
        qc = rq_ref[rows, :]
        kc = rk_ref[rows, :]
        vc = rv_ref[rows, :]
        qs = jnp.concatenate([qc * hm_ref[hh:hh + 1, :] for hh in range(RET_HEADS)], axis=0)
        sc = lax.dot_general(qs, kc, (((1,), (1,)), ((), ())), preferred_element_type=F32)
        sc = (sc * dst_ref[...]).astype(BF16)
        state = state_ref[...]
        qdec = (qc.astype(F32) * qd_ref[...]).astype(BF16)
        cross = jnp.dot(qdec, state.astype(BF16), preferred_element_type=F32)
        kdec = (kc.astype(F32) * kd_ref[...]).astype(BF16)
        kv = lax.dot_general(kdec, vc, (((0,), (0,)), ((), ())), preferred_element_type=F32)
        state_ref[...] = state * cd_ref[...] + kv * bm_ref[...]
        for hh in range(RET_HEADS):
            cols = slice(hh * RET_DV, (hh + 1) * RET_DV)
            o = jnp.dot(sc[hh * C:(hh + 1) * C, :], vc[:, cols], preferred_element_type=F32) + cross[:, cols]
            msq = jnp.mean(o * o, axis=-1, keepdims=True)
            y = o * lax.rsqrt(msq + EPS) * g_ref[:, cols]
            o_ref[rows, cols] = (y * _silu(rg_ref[rows, cols].astype(F32))).astype(BF16)


def _retention(rq, rk, rv, rg, tables, g_ret, batch, seq):
    T = rq.shape[0]
    tm = TOK_TILE
    tps = seq // tm
    row = lambda b, i: (b * tps + i, 0)
    fix = lambda b, i: (0, 0)
    full = lambda a: pl.BlockSpec(a.shape, fix)
    return pl.pallas_call(
        functools.partial(_retention_kernel, tm // RET_CHUNK),
        grid=(batch, tps),
        in_specs=[pl.BlockSpec((tm, 256), row), pl.BlockSpec((tm, 256), row),
                  pl.BlockSpec((tm, 512), row), pl.BlockSpec((tm, 512), row)]
                 + [full(t) for t in tables] + [full(g_ret)],
        out_specs=pl.BlockSpec((tm, 512), row),
        out_shape=jax.ShapeDtypeStruct((T, 512), BF16),
        scratch_shapes=[pltpu.VMEM((RET_HEADS * RET_DK, RET_HEADS * RET_DV), F32)],
        compiler_params=pltpu.CompilerParams(dimension_semantics=("arbitrary", "arbitrary"),
                                             vmem_limit_bytes=VMEM_LIMIT),
        name="retention",
    )(rq, rk, rv, rg, *tables, g_ret)


def _fox_kernel(tq, tk, qt_ref, k_ref, vt_ref, g_ref, o_ref, m_ref, acc_ref, s0_ref, s1_ref):
    i = pl.program_id(2)
    qt = qt_ref[...]
    m_ref[...] = jnp.full_like(m_ref, NEG)
    acc_ref[...] = jnp.zeros_like(acc_ref)
    assert tq == 2 * tk
    s_refs = (s0_ref, s1_ref)

    def scores(j, slot):
        start = pl.multiple_of(j * tk, tk)
        s_refs[slot][...] = jnp.dot(k_ref[pl.ds(start, tk), :], qt, preferred_element_type=F32)

    def update(j, slot, diag):
        start = pl.multiple_of(j * tk, tk)
        st = s_refs[slot][...]
        if diag is not None:
            key = lax.broadcasted_iota(jnp.int32, st.shape, 0) + diag * tk
            qry = lax.broadcasted_iota(jnp.int32, st.shape, 1)
            st = jnp.where(key <= qry, st, NEG)
        m_prev = m_ref[...]
        m_new = jnp.maximum(m_prev, jnp.max(st, axis=0, keepdims=True))
        alpha = jnp.exp2(m_prev - m_new)
        p = jnp.exp2(st - m_new).astype(BF16)
        vt = vt_ref[:, pl.ds(start, tk)]
        acc_ref[...] = alpha * acc_ref[...] + jnp.dot(vt, p, preferred_element_type=F32)
        m_ref[...] = m_new

    scores(0, 0)

    def body(t, carry):
        scores(2 * t + 1, 1)
        update(2 * t, 0, None)
        scores(2 * t + 2, 0)
        update(2 * t + 1, 1, None)
        return carry

    lax.fori_loop(0, i, body, 0)
    scores(2 * i + 1, 1)
    update(2 * i, 0, 0)
    update(2 * i + 1, 1, 1)
    acc = acc_ref[...]
    row = lax.broadcasted_iota(jnp.int32, acc.shape, 0)
    o = jnp.where(row < FOX_DH, acc / acc[V_ONE:V_ONE + 1, :], 0.0)
    msq = jnp.sum(o * o, axis=0, keepdims=True) * (1.0 / FOX_DH)
    o_ref[...] = ((o * lax.rsqrt(msq + EPS)).T * g_ref[...]).astype(BF16)


def _fox(fqt, fka, fvt, g_fox, batch, seq):
    T = fka.shape[0]
    tq = FOX_Q_TILE
    tk = FOX_K_TILE
    nq = seq // tq
    return pl.pallas_call(
        functools.partial(_fox_kernel, tq, tk),
        grid=(batch, FOX_HEADS, nq),
        in_specs=[pl.BlockSpec((LANE, tq), lambda b, h, i: (h, b * nq + i)),
                  pl.BlockSpec((seq, LANE), lambda b, h, i: (b, h)),
                  pl.BlockSpec((LANE, seq), lambda b, h, i: (h, b)),
                  pl.BlockSpec((1, LANE), lambda b, h, i: (0, h))],
        out_specs=pl.BlockSpec((tq, LANE), lambda b, h, i: (b * nq + i, h)),
        out_shape=jax.ShapeDtypeStruct((T, FOX_HEADS * LANE), BF16),
        scratch_shapes=[pltpu.VMEM((1, tq), F32), pltpu.VMEM((LANE, tq), F32),
                        pltpu.VMEM((tk, tq), F32), pltpu.VMEM((tk, tq), F32)],
        compiler_params=pltpu.CompilerParams(dimension_semantics=("arbitrary", "arbitrary", "arbitrary"),
                                             vmem_limit_bytes=VMEM_LIMIT),
        name="fox",
    )(fqt, fka, fvt, g_fox)


def _mem_prep_kernel(mem_ref, g_ref, w_ref, kg_ref, bd_ref, shift_ref, mkt_ref, mvs_ref):
    x = mem_ref[0]
    ms = jnp.mean(x * x, axis=-1, keepdims=True)
    mn = (x * lax.rsqrt(ms + EPS) * g_ref[0]).astype(BF16)
    mkv = jnp.dot(mn, w_ref[0], preferred_element_type=F32)
    width = MEM_HEADS * MEM_DH
    mk = mkv[:, :width]
    mv = mkv[:, width:]
    mk = mk * lax.rsqrt(_group_mean_sq(mk, bd_ref[...]) + EPS) * kg_ref[0]
    mk = jnp.dot(mk.astype(BF16), shift_ref[...], preferred_element_type=F32)
    mkt = mk.T
    feat = (lax.broadcasted_iota(jnp.int32, mkt.shape, 0) - MQ_OFF) // MEM_DH
    lane_head = lax.broadcasted_iota(jnp.int32, mv.shape, 1) // MEM_DH
    M = x.shape[0]
    for hh in range(MEM_HEADS):
        mkt_ref[0, 0, :, hh * M:(hh + 1) * M] = jnp.where(feat == hh, mkt, 0.0).astype(BF16)
        mvs_ref[0, 0, hh * M:(hh + 1) * M, :] = jnp.where(lane_head == hh, mv, 0.0).astype(BF16)


def _mem_prep(mem, mem_norm_g, w_mem_kv, mem_k_g, bd, shift):
    B, M, _ = mem.shape
    depth = w_mem_kv.shape[0]
    width = MEM_HEADS * MEM_DH
    return pl.pallas_call(
        _mem_prep_kernel,
        grid=(depth, B),
        in_specs=[pl.BlockSpec((1, M, D_MODEL), lambda l, b: (b, 0, 0)),
                  pl.BlockSpec((1, 1, D_MODEL), lambda l, b: (l, 0, 0)),
                  pl.BlockSpec((1, D_MODEL, 2 * width), lambda l, b: (l, 0, 0)),
                  pl.BlockSpec((1, 1, width), lambda l, b: (l, 0, 0)),
                  pl.BlockSpec(bd.shape, lambda l, b: (0, 0)),
                  pl.BlockSpec(shift.shape, lambda l, b: (0, 0))],
        out_specs=[pl.BlockSpec((1, 1, MQ_SLAB, MEM_HEADS * M), lambda l, b: (l, b, 0, 0)),
                   pl.BlockSpec((1, 1, MEM_HEADS * M, width), lambda l, b: (l, b, 0, 0))],
        out_shape=[jax.ShapeDtypeStruct((depth, B, MQ_SLAB, MEM_HEADS * M), BF16),
                   jax.ShapeDtypeStruct((depth, B, MEM_HEADS * M, width), BF16)],
        compiler_params=pltpu.CompilerParams(dimension_semantics=("arbitrary", "arbitrary"),
                                             vmem_limit_bytes=VMEM_LIMIT),
        name="mem_prep",
    )(mem, mem_norm_g, w_mem_kv, mem_k_g, bd, shift)


def _out_proj_body(ret_ref, fox_ref, mq_ref, mkt_ref, mvs_ref, bd_ref, gm_ref, w_ref, x_ref, fg_ref):
    M = mkt_ref.shape[-1] // MEM_HEADS
    s = jnp.dot(mq_ref[...], mkt_ref[0, 0], preferred_element_type=F32)
    mem_o = None
    for hh in range(MEM_HEADS):
        sh = s[:, hh * M:(hh + 1) * M]
        p = jnp.exp(sh - jnp.max(sh, axis=-1, keepdims=True))
        p = (p / jnp.sum(p, axis=-1, keepdims=True)).astype(BF16)
        part = jnp.dot(p, mvs_ref[0, 0, hh * M:(hh + 1) * M, :], preferred_element_type=F32)
        mem_o = part if mem_o is None else mem_o + part
    mem_o = (mem_o * lax.rsqrt(_group_mean_sq(mem_o, bd_ref[...]) + EPS) * gm_ref[...]).astype(BF16)
    y = (jnp.dot(ret_ref[...], w_ref[0:512, :], preferred_element_type=F32)
         + jnp.dot(fox_ref[...], w_ref[512:1024, :], preferred_element_type=F32)
         + jnp.dot(mem_o, w_ref[1024:1280, :], preferred_element_type=F32))
    xm = x_ref[...] + y
    ms = jnp.mean(xm * xm, axis=-1, keepdims=True)
    return xm, xm * lax.rsqrt(ms + EPS) * fg_ref[...]


def _out_proj_dense_kernel(ret_ref, fox_ref, mq_ref, mkt_ref, mvs_ref, bd_ref, gm_ref, w_ref, x_ref, fg_ref,
                           xm_ref, h_ref):
    xm, hn = _out_proj_body(ret_ref, fox_ref, mq_ref, mkt_ref, mvs_ref, bd_ref, gm_ref, w_ref, x_ref, fg_ref)
    xm_ref[...] = xm
    h_ref[...] = hn.astype(BF16)


def _out_proj_moe_kernel(ret_ref, fox_ref, mq_ref, mkt_ref, mvs_ref, bd_ref, gm_ref, w_ref, x_ref, fg_ref,
                         rwt_ref, tri_ref, xm_ref, h_ref, routet_ref, route_ref, cnt_ref, carry_ref):
    @pl.when(pl.program_id(0) == 0)
    def _():
        carry_ref[...] = jnp.zeros_like(carry_ref)

    xm, hn = _out_proj_body(ret_ref, fox_ref, mq_ref, mkt_ref, mvs_ref, bd_ref, gm_ref, w_ref, x_ref, fg_ref)
    xm_ref[...] = xm
    h_ref[...] = hn
    tm = hn.shape[0]
    h_hi = hn.astype(BF16)
    h_lo = (hn - h_hi.astype(F32)).astype(BF16)
    rwt = rwt_ref[...]
    nt = (((1,), (1,)), ((), ()))
    a = lax.dot_general(rwt, h_hi, nt, preferred_element_type=F32)
    b = lax.dot_general(rwt[0:16], h_lo, nt, preferred_element_type=F32)
    lg = a[0:N_EXPERTS] + a[16:16 + N_EXPERTS] + b[0:N_EXPERTS]
    row = lax.broadcasted_iota(jnp.int32, lg.shape, 0).astype(F32)
    m1 = jnp.max(lg, axis=0, keepdims=True)
    i1 = jnp.min(jnp.where(lg == m1, row, float(N_EXPERTS)), axis=0, keepdims=True)
    oh1 = row == i1
    lg2 = jnp.where(oh1, NEG, lg)
    m2 = jnp.max(lg2, axis=0, keepdims=True)
    i2 = jnp.min(jnp.where(lg2 == m2, row, float(N_EXPERTS)), axis=0, keepdims=True)
    oh2 = row == i2
    e = jnp.exp(m2 - m1)
    g1 = 1.0 / (1.0 + e)
    g2 = e / (1.0 + e)
    oh = jnp.where(oh1 | oh2, 1.0, 0.0)
    oh16 = jnp.concatenate([oh, jnp.zeros_like(oh)], axis=0).astype(BF16)
    before = carry_ref[:, 0:1] + jnp.dot(oh16, tri_ref[...], preferred_element_type=F32)[0:N_EXPERTS]
    r1 = jnp.sum(jnp.where(oh1, before, 0.0), axis=0, keepdims=True)
    r2 = jnp.sum(jnp.where(oh2, before, 0.0), axis=0, keepdims=True)
    total = before[:, tm - 1:tm] + oh[:, tm - 1:tm]
    carry_ref[...] = jnp.broadcast_to(total, carry_ref.shape)
    cnt_ref[...] = jnp.broadcast_to(total, cnt_ref.shape)
    rt = jnp.zeros_like(lg)
    for r, val in enumerate((i1, i2, r1, r2, g1, g2)):
        rt = jnp.where(row == r, val, rt)
    routet_ref[...] = rt
    route_ref[...] = jnp.concatenate([rt, jnp.zeros((LANE - N_EXPERTS, tm), F32)], axis=0).T


def _out_proj(ret_o, fox_o, mq, mkt, mvs, bd, g_mem, w_out, x, ffn_g, layer, seq, router=None):
    T = x.shape[0]
    tm = TOK_TILE
    tps = seq // tm
    row = lambda i: (i, 0)
    fix = lambda i: (0, 0)
    full = lambda a: pl.BlockSpec(a.shape, fix)
    M4 = mkt.shape[-1]
    in_specs = [pl.BlockSpec((tm, 512), row), pl.BlockSpec((tm, 512), row), pl.BlockSpec((tm, MQ_SLAB), row),
                pl.BlockSpec((1, 1, MQ_SLAB, M4), lambda i: (layer, i // tps, 0, 0)),
                pl.BlockSpec((1, 1, M4, 256), lambda i: (layer, i // tps, 0, 0)),
                full(bd), full(g_mem), full(w_out), pl.BlockSpec((tm, D_MODEL), row), full(ffn_g)]
    args = [ret_o, fox_o, mq, mkt, mvs, bd, g_mem, w_out, x, ffn_g]
    params = pltpu.CompilerParams(dimension_semantics=("arbitrary",), vmem_limit_bytes=VMEM_LIMIT)
    if router is None:
        return pl.pallas_call(
            _out_proj_dense_kernel, grid=(T // tm,), in_specs=in_specs,
            out_specs=[pl.BlockSpec((tm, D_MODEL), row), pl.BlockSpec((tm, D_MODEL), row)],
            out_shape=[jax.ShapeDtypeStruct((T, D_MODEL), F32), jax.ShapeDtypeStruct((T, D_MODEL), BF16)],
            compiler_params=params, name="out_proj_dense",
        )(*args)
    rw, tri = router
    return pl.pallas_call(
        _out_proj_moe_kernel, grid=(T // tm,), in_specs=in_specs + [full(rw), full(tri)],
        out_specs=[pl.BlockSpec((tm, D_MODEL), row), pl.BlockSpec((tm, D_MODEL), row),
                   pl.BlockSpec((N_EXPERTS, tm), lambda i: (0, i)),
                   pl.BlockSpec((tm, LANE), row), pl.BlockSpec((8, LANE), fix)],
        out_shape=[jax.ShapeDtypeStruct((T, D_MODEL), F32), jax.ShapeDtypeStruct((T, D_MODEL), F32),
                   jax.ShapeDtypeStruct((N_EXPERTS, T), F32),
                   jax.ShapeDtypeStruct((T, LANE), F32), jax.ShapeDtypeStruct((8, LANE), F32)],
        scratch_shapes=[pltpu.VMEM((8, LANE), F32)],
        compiler_params=params, name="out_proj_moe",
    )(*args, rw, tri)


def _swiglu(x, wg_ref, wu_ref, wd_ref, y):
    for c in range(D_FF_PAD // FF_TILE):
        cols = slice(c * FF_TILE, (c + 1) * FF_TILE)
        g = jnp.dot(x, wg_ref[:, cols], preferred_element_type=F32)
        u = jnp.dot(x, wu_ref[:, cols], preferred_element_type=F32)
        a = (_silu(g) * u).astype(BF16)
        part = jnp.dot(a, wd_ref[cols, :], preferred_element_type=F32)
        y = part if y is None else y + part
    return y


def _dense_ffn_kernel(h_ref, wg_ref, wu_ref, wd_ref, x_ref, o_ref):
    o_ref[...] = _swiglu(h_ref[...], wg_ref.at[0], wu_ref.at[0], wd_ref.at[0], x_ref[...])


def _dense_ffn(h, wg, wu, wd, x, layer):
    T = h.shape[0]
    tm = TOK_TILE
    once = pl.Buffered(1)
    return pl.pallas_call(
        _dense_ffn_kernel,
        grid=(T // tm,),
        in_specs=[pl.BlockSpec((tm, D_MODEL), lambda i: (i, 0)),
                  pl.BlockSpec((1, D_MODEL, D_FF_PAD), lambda i: (layer, 0, 0), pipeline_mode=once),
                  pl.BlockSpec((1, D_MODEL, D_FF_PAD), lambda i: (layer, 0, 0), pipeline_mode=once),
                  pl.BlockSpec((1, D_FF_PAD, D_MODEL), lambda i: (layer, 0, 0), pipeline_mode=once),
                  pl.BlockSpec((tm, D_MODEL), lambda i: (i, 0))],
        out_specs=pl.BlockSpec((tm, D_MODEL), lambda i: (i, 0)),
        out_shape=jax.ShapeDtypeStruct((T, D_MODEL), F32),
        compiler_params=pltpu.CompilerParams(dimension_semantics=("arbitrary",),
                                             vmem_limit_bytes=VMEM_LIMIT),
        name="dense_ffn",
    )(h, wg, wu, wd, x)


def _row_copy(src, s, dst, d, sem):
    return pltpu.make_async_copy(src.at[pl.ds(s, 1)], dst.at[pl.ds(d, 1)], sem)


def _moe_ffn_kernel(te_ref, nu_ref, src_ref, h_hbm, wg_ref, wu_ref, wd_ref, o_ref, x_ref, sem):
    i = pl.program_id(0)
    n_used = nu_ref[0]
    tm = x_ref.shape[1]
    slot = i % 2
    nxt = 1 - slot

    def tile_wait(s):
        pltpu.make_async_copy(h_hbm.at[pl.ds(0, tm)], x_ref.at[s], sem.at[s]).wait()

    @pl.when(i == 0)
    def _():
        def issue(r, carry):
            _row_copy(h_hbm, src_ref[r], x_ref.at[0], r, sem.at[0]).start()
            return carry

        lax.fori_loop(0, tm, issue, 0, unroll=8)

    @pl.when(i < n_used)
    def _():
        tile_wait(slot)
        x = x_ref[slot].astype(BF16)
        base = (i + 1) * tm
        for r in range(tm):
            _row_copy(h_hbm, src_ref[base + r], x_ref.at[nxt], r, sem.at[nxt]).start(priority=r % 2)
        o_ref[...] = _swiglu(x, wg_ref.at[0, 0], wu_ref.at[0, 0], wd_ref.at[0, 0], None)

        @pl.when(i + 1 == n_used)
        def _():
            tile_wait(nxt)

    @pl.when(i >= n_used)
    def _():
        o_ref[...] = jnp.zeros_like(o_ref)


def _moe_ffn(h, src, tile_expert, n_used, wg, wu, wd, layer):
    tm = MOE_TILE
    n_tiles = tile_expert.shape[0]
    assert src.shape[0] == (n_tiles + 1) * tm
    w_block = lambda i, te, nu, sr: (layer, te[i], 0, 0)
    return pl.pallas_call(
        _moe_ffn_kernel,
        grid_spec=pltpu.PrefetchScalarGridSpec(
            num_scalar_prefetch=3, grid=(n_tiles,),
            in_specs=[pl.BlockSpec(memory_space=pl.ANY),
                      pl.BlockSpec((1, 1, D_MODEL, D_FF_PAD), w_block),
                      pl.BlockSpec((1, 1, D_MODEL, D_FF_PAD), w_block),
                      pl.BlockSpec((1, 1, D_FF_PAD, D_MODEL), w_block)],
            out_specs=pl.BlockSpec((tm, D_MODEL), lambda i, te, nu, sr: (i, 0)),
            scratch_shapes=[pltpu.VMEM((2, tm, D_MODEL), F32), pltpu.SemaphoreType.DMA((2,))]),
        out_shape=jax.ShapeDtypeStruct((n_tiles * tm, D_MODEL), F32),
        compiler_params=pltpu.CompilerParams(dimension_semantics=("arbitrary",),
                                             vmem_limit_bytes=VMEM_LIMIT),
        name="moe_ffn",
    )(tile_expert, n_used, src, h, wg, wu, wd)


def _combine_kernel(tc, d1_ref, d2_ref, o_hbm, x_ref, route_ref, out_ref, b1_ref, b2_ref, sem):
    base = pl.program_id(0) * tc

    def issue(r, carry):
        t = base + r
        _row_copy(o_hbm, d1_ref[t], b1_ref, r, sem.at[0]).start(priority=0)
        _row_copy(o_hbm, d2_ref[t], b2_ref, r, sem.at[1]).start(priority=1)
        return carry

    lax.fori_loop(0, tc, issue, 0, unroll=8)
    pltpu.make_async_copy(o_hbm.at[pl.ds(0, tc)], b1_ref, sem.at[0]).wait()
    pltpu.make_async_copy(o_hbm.at[pl.ds(0, tc)], b2_ref, sem.at[1]).wait()
    route = route_ref[...]
    g1 = route[:, 4:5]
    g2 = route[:, 5:6]
    out_ref[...] = x_ref[...] + g1 * b1_ref[...] + g2 * b2_ref[...]


def _combine(o, d1, d2, x, route):
    T = x.shape[0]
    tc = 256
    return pl.pallas_call(
        functools.partial(_combine_kernel, tc),
        grid_spec=pltpu.PrefetchScalarGridSpec(
            num_scalar_prefetch=2, grid=(T // tc,),
            in_specs=[pl.BlockSpec(memory_space=pl.ANY),
                      pl.BlockSpec((tc, D_MODEL), lambda i, a, b: (i, 0)),
                      pl.BlockSpec((tc, LANE), lambda i, a, b: (i, 0))],
            out_specs=pl.BlockSpec((tc, D_MODEL), lambda i, a, b: (i, 0)),
            scratch_shapes=[pltpu.VMEM((tc, D_MODEL), F32), pltpu.VMEM((tc, D_MODEL), F32),
                            pltpu.SemaphoreType.DMA((2,))]),
        out_shape=jax.ShapeDtypeStruct((T, D_MODEL), F32),
        compiler_params=pltpu.CompilerParams(dimension_semantics=("arbitrary",),
                                             vmem_limit_bytes=VMEM_LIMIT),
        name="moe_combine",
    )(d1, d2, o, x, route)


def _plan_kernel(rows, d1_ref, d2_ref, src_ref):
    def clear(r, carry):
        src_ref[r] = 0
        return carry

    def place(t, carry):
        src_ref[d1_ref[t]] = t
        src_ref[d2_ref[t]] = t
        return carry

    lax.fori_loop(0, rows, clear, 0, unroll=8)
    lax.fori_loop(0, d1_ref.shape[0], place, 0, unroll=8)


def _plan(d1, d2, rows):
    return pl.pallas_call(
        functools.partial(_plan_kernel, rows),
        grid_spec=pltpu.PrefetchScalarGridSpec(
            num_scalar_prefetch=2, grid=(1,), in_specs=[],
            out_specs=pl.BlockSpec(memory_space=pltpu.SMEM)),
        out_shape=jax.ShapeDtypeStruct((rows,), jnp.int32),
        name="moe_plan",
    )(d1, d2)


def _moe_layer(h, xm, routet, route, counts, wg, wu, wd, layer):
    T = h.shape[0]
    tm = MOE_TILE
    n_tiles = (2 * T) // tm + N_EXPERTS
    e1 = routet[0].astype(jnp.int32)
    e2 = routet[1].astype(jnp.int32)
    r1 = routet[2].astype(jnp.int32)
    r2 = routet[3].astype(jnp.int32)
    cnt = counts[:, 0].astype(jnp.int32)
    tiles = (cnt + tm - 1) // tm
    tile_end = jnp.cumsum(tiles)
    offs = (tile_end - tiles) * tm
    d1 = offs[e1] + r1
    d2 = offs[e2] + r2
    n_used = tile_end[-1:]
    tile_id = jnp.arange(n_tiles, dtype=jnp.int32)
    tile_expert = jnp.sum(tile_id[:, None] >= tile_end[None, :], axis=1).astype(jnp.int32)
    tile_expert = jnp.minimum(tile_expert, tile_expert[jnp.maximum(n_used[0] - 1, 0)])
    src = _plan(d1, d2, (n_tiles + 1) * tm)
    o = _moe_ffn(h, src, tile_expert, n_used.astype(jnp.int32), wg, wu, wd, layer)
    return _combine(o, d1, d2, xm, route)


def _constants(seq):
    tm = TOK_TILE
    r = np.arange(tm)
    tri_upper = (r[:, None] <= r[None, :]).astype(np.float32)
    tri_strict = (r[:, None] < r[None, :]).astype(np.float32)
    d = np.arange(256)
    bd = (d[:, None] // 64 == d[None, :] // 64).astype(np.float32) / 64.0
    m = np.arange(MQ_SLAB) - MQ_OFF
    valid = (m >= 0) & (m < MEM_HEADS * MEM_DH)
    bdm = ((m[:, None] // MEM_DH == m[None, :] // MEM_DH) & valid[:, None] & valid[None, :]).astype(np.float32) / MEM_DH
    shift = (d[:, None] + MQ_OFF == np.arange(MQ_SLAB)[None, :]).astype(np.float32)
    in_consts = (jnp.asarray(tri_upper, BF16), jnp.asarray(bd, BF16), jnp.asarray(bdm, BF16))

    half = RET_DK // 2
    pos = jnp.arange(seq, dtype=F32)
    inv_freq = ROPE_BASE ** (-jnp.arange(half, dtype=F32) / half)
    ang = pos[:, None] * inv_freq[None, :]
    cos = jnp.tile(jnp.cos(ang), (1, LANE // half))
    sin = jnp.tile(jnp.concatenate([-jnp.sin(ang), jnp.sin(ang)], axis=1), (1, LANE // RET_DK))

    C = RET_CHUNK
    hh = jnp.arange(RET_HEADS, dtype=F32)
    log_g = jnp.log(1.0 - 2.0 ** (-5.0 - hh))
    idx = jnp.arange(C, dtype=F32)
    diff = idx[:, None] - idx[None, :]
    dmat = jnp.where(diff >= 0, jnp.exp(log_g[:, None, None] * jnp.maximum(diff, 0.0)), 0.0)
    dstack = dmat.reshape(RET_HEADS * C, C)
    k_head = np.arange(RET_HEADS * RET_DK) // RET_DK
    v_head = np.arange(RET_HEADS * RET_DV) // RET_DV
    qd = jnp.exp(log_g[k_head][None, :] * (idx[:, None] + 1.0))
    kd = jnp.exp(log_g[k_head][None, :] * (C - 1.0 - idx[:, None]))
    cd = jnp.exp(log_g[v_head] * C)[None, :]
    bm = jnp.asarray((k_head[:, None] == v_head[None, :]).astype(np.float32))
    hm = jnp.asarray((np.arange(RET_HEADS)[:, None] == k_head[None, :]).astype(np.float32), BF16)
    ret_tables = (hm, dstack, qd, kd, cd, bm)
    return in_consts, cos, sin, ret_tables, jnp.asarray(tri_strict, BF16), jnp.asarray(shift, BF16)


def _pad_heads(g):
    depth = g.shape[0]
    g = g.reshape(depth, FOX_HEADS, FOX_DH)
    return jnp.pad(g, ((0, 0), (0, 0), (0, LANE - FOX_DH))).reshape(depth, 1, FOX_HEADS * LANE)


def kernel(x, mem, attn_norm_g, w_in, forget_bias, fox_q_norm_g, fox_k_norm_g, mem_q_norm_g, mem_k_norm_g,
           mem_norm_g, w_mem_kv, out_norm_g, w_out, ffn_norm_g, dense_w_gate, dense_w_up, dense_w_down,
           router_w, expert_w_gate, expert_w_up, expert_w_down):
    B, S, D = x.shape
    depth = w_in.shape[0]
    T = B * S
    in_consts, cos, sin, ret_tables, tri_strict, shift = _constants(S)
    bd = in_consts[1]

    w_in_p = jnp.pad(w_in, ((0, 0), (0, 0), (0, P_IN - w_in.shape[-1]))).astype(BF16)
    fqg = jnp.tile(fox_q_norm_g, (1, FOX_HEADS))[:, None, :]
    fkg = jnp.tile(fox_k_norm_g, (1, FOX_HEADS))[:, None, :]
    fb = jnp.pad(forget_bias, ((0, 0), (0, 8 - FOX_HEADS)))[:, :, None]
    mqg = jnp.tile(mem_q_norm_g, (1, MEM_HEADS))
    mqg = jnp.pad(mqg, ((0, 0), (MQ_OFF, MQ_SLAB - MQ_OFF - mqg.shape[-1])))[:, None, :]
    mkg = jnp.tile(mem_k_norm_g, (1, MEM_HEADS))[:, None, :]
    n_ret = RET_HEADS * RET_DV
    n_fox = FOX_HEADS * FOX_DH
    g_ret = out_norm_g[:, None, :n_ret]
    g_fox = _pad_heads(out_norm_g[:, n_ret:n_ret + n_fox])
    g_mem = out_norm_g[:, None, n_ret + n_fox:]
    w_fox = w_out[:, n_ret:n_ret + n_fox].reshape(depth, FOX_HEADS, FOX_DH, D)
    w_fox = jnp.pad(w_fox, ((0, 0), (0, 0), (0, LANE - FOX_DH), (0, 0))).reshape(depth, FOX_HEADS * LANE, D)
    w_out_p = jnp.concatenate([w_out[:, :n_ret], w_fox, w_out[:, n_ret + n_fox:]], axis=1).astype(BF16)
    ffpad = D_FF_PAD - D_FF
    pad_cols = lambda w: jnp.pad(w, [(0, 0)] * (w.ndim - 1) + [(0, ffpad)]).astype(BF16)
    pad_rows = lambda w: jnp.pad(w, [(0, 0)] * (w.ndim - 2) + [(0, ffpad), (0, 0)]).astype(BF16)
    dwg, dwu, dwd = pad_cols(dense_w_gate), pad_cols(dense_w_up), pad_rows(dense_w_down)
    ewg, ewu, ewd = pad_cols(expert_w_gate), pad_cols(expert_w_up), pad_rows(expert_w_down)
    rwt = jnp.swapaxes(router_w, 1, 2)
    rwt_hi = rwt.astype(BF16)
    rwt_lo = (rwt - rwt_hi.astype(F32)).astype(BF16)
    pad16 = lambda w: jnp.pad(w, ((0, 0), (0, 16 - N_EXPERTS), (0, 0)))
    rw = jnp.concatenate([pad16(rwt_hi), pad16(rwt_lo)], axis=1)

    mkt, mvs = _mem_prep(mem, mem_norm_g[:, None, :], w_mem_kv.astype(BF16), mkg, bd, shift)

    xt = x.reshape(T, D)
    for l in range(depth):
        rq, rk, rv, rg, fqt, fka, fvt, mq = _in_proj(
            xt, attn_norm_g[l][None], w_in_p[l], cos, sin, fqg[l], fkg[l], mqg[l], fb[l], in_consts, S)
        ret_o = _retention(rq, rk, rv, rg, ret_tables, g_ret[l], B, S)
        fox_o = _fox(fqt, fka, fvt, g_fox[l], B, S)
        j = l // 2
        if l % 2 == 0:
            xm, h2 = _out_proj(ret_o, fox_o, mq, mkt, mvs, bd, g_mem[l], w_out_p[l], xt, ffn_norm_g[l][None], l, S)
            xt = _dense_ffn(h2, dwg, dwu, dwd, xm, j)
        else:
            xm, h2, routet, route, counts = _out_proj(ret_o, fox_o, mq, mkt, mvs, bd, g_mem[l], w_out_p[l], xt,
                                                      ffn_norm_g[l][None], l, S, router=(rw[j], tri_strict))
            xt = _moe_layer(h2, xm, routet, route, counts, ewg, ewu, ewd, j)
    return xt.reshape(B, S, D)
```

```python
import functools

import numpy as np
import jax
import jax.numpy as jnp
from jax import lax
from jax.experimental import pallas as pl
from jax.experimental.pallas import tpu as pltpu

F32 = jnp.float32
BF16 = jnp.bfloat16

D_MODEL = 1024
RET_HEADS = 4
RET_DK = 64
RET_DV = 128
FOX_HEADS = 4
FOX_DH = 64
MEM_HEADS = 4
MEM_DH = 64
RET_CHUNK = 128
D_FF = 2752
N_EXPERTS = 8
ROPE_BASE = 10000.0
EPS = 1e-6

LANE = 128
D_FF_PAD = 2816
FF_TILE = D_FF_PAD // 2
TOK_TILE = 512
FOX_Q_TILE = 1024
FOX_K_TILE = 512
LOG2E = 1.4426950408889634
ROW_TILE = 8
assert D_MODEL == ROW_TILE * LANE
MOE_TILE = 512
VMEM_LIMIT = 56 * 1024 * 1024
NEG = -1e30

RQ, RK, RV, RG = 0, 256, 512, 1024
FQ, FK, FV = 1536, 1792, 2048
FF_COL = 2304
MQ_OFF = FOX_HEADS
MQ_SLAB = 384
P_IN = FF_COL + MQ_SLAB
N_PIECE = 3
V_ONE = 64


def _split3(x):
    hi = x.astype(BF16)
    r1 = x - hi.astype(F32)
    mid = r1.astype(BF16)
    lo = (r1 - mid.astype(F32)).astype(BF16)
    return hi, mid, lo


def _group_mean_sq(x, bd):
    sq = x * x
    hi = sq.astype(BF16)
    lo = (sq - hi.astype(F32)).astype(BF16)
    return (jnp.dot(hi, bd, preferred_element_type=F32)
            + jnp.dot(lo, bd, preferred_element_type=F32))


def _silu(x):
    return x / (1.0 + jnp.exp(-x))


def _in_proj_kernel(tiles_per_seq, x_ref, g_ref, w_ref, cos_ref, sin_ref, fqg_ref, fkg_ref, mqg_ref,
                    fb_ref, tri_ref, bd_ref, bdm_ref,
                    rq_ref, rk_ref, rv_ref, rg_ref, fqt_ref, fka_ref, fvt_ref, mq_ref, carry_ref):
    i = pl.program_id(0)

    @pl.when(i % tiles_per_seq == 0)
    def _():
        carry_ref[...] = jnp.zeros_like(carry_ref)

    x = x_ref[...]
    ms = jnp.mean(x * x, axis=-1, keepdims=True)
    h = (x * lax.rsqrt(ms + EPS) * g_ref[...]).astype(BF16)

    def proj(a, b):
        return jnp.dot(h, w_ref[:, a:b], preferred_element_type=F32)

    tm = x.shape[0]
    cos = cos_ref[...]
    sin = sin_ref[...]
    first = (lax.broadcasted_iota(jnp.int32, (tm, LANE), 1) % RET_DK) < RET_DK // 2
    for base, ref, scale in ((RQ, rq_ref, RET_DK ** -0.5), (RK, rk_ref, 1.0)):
        for a in (0, LANE):
            t = proj(base + a, base + a + LANE)
            partner = jnp.where(first, pltpu.roll(t, LANE - RET_DK // 2, 1), pltpu.roll(t, RET_DK // 2, 1))
            ref[:, a:a + LANE] = ((t * cos + partner * sin) * scale).astype(BF16)
    rv_ref[...] = proj(RV, RV + 512).astype(BF16)
    rg_ref[...] = proj(RG, RG + 512).astype(BF16)

    zm = proj(FF_COL, FF_COL + MQ_SLAB)
    mq = zm * lax.rsqrt(_group_mean_sq(zm, bdm_ref[...]) + EPS) * mqg_ref[...] * (MEM_DH ** -0.5)
    mq_ref[...] = mq.astype(BF16)

    z = zm[:, 0:LANE].T[0:8, :] + fb_ref[...]
    logf = jnp.minimum(z, 0.0) - jnp.log(1.0 + jnp.exp(-jnp.abs(z)))
    hrow = lax.broadcasted_iota(jnp.int32, logf.shape, 0)
    logf = jnp.where(hrow < FOX_HEADS, logf, 0.0)
    tri = tri_ref[...]
    c = carry_ref[:, 0:1]
    for piece in _split3(logf):
        c = c + jnp.dot(piece, tri, preferred_element_type=F32)
    carry_ref[...] = jnp.broadcast_to(c[:, tm - 1:tm], carry_ref.shape)

    pieces = [p.astype(F32) for p in _split3(c * LOG2E)]
    bd = bd_ref[...]
    fq = proj(FQ, FQ + 256)
    fk = proj(FK, FK + 256)
    qt = (fq * lax.rsqrt(_group_mean_sq(fq, bd) + EPS) * fqg_ref[...] * (FOX_DH ** -0.5 * LOG2E)).T
    kt = (fk * lax.rsqrt(_group_mean_sq(fk, bd) + EPS) * fkg_ref[...]).T
    vt = proj(FV, FV + 256).T
    erow = lax.broadcasted_iota(jnp.int32, (8, tm), 0)
    zeros = jnp.zeros((LANE - FOX_DH - 8, tm), F32)
    ones_row = jnp.where(erow == V_ONE - FOX_DH, 1.0, 0.0)
    for hh in range(FOX_HEADS):
        feat = slice(hh * FOX_DH, (hh + 1) * FOX_DH)
        rows = slice(hh * LANE, (hh + 1) * LANE)
        eq = jnp.where(erow < 2 * N_PIECE, 1.0, 0.0)
        ek = eq
        for p in range(N_PIECE):
            c_row = pieces[p][hh:hh + 1, :]
            eq = jnp.where(erow == p, c_row, eq)
            ek = jnp.where(erow == N_PIECE + p, -c_row, ek)
        fqt_ref[rows, :] = jnp.concatenate([qt[feat], eq, zeros], axis=0).astype(BF16)
        fka_ref[:, rows] = jnp.concatenate([kt[feat], ek, zeros], axis=0).T.astype(BF16)
        fvt_ref[rows, :] = jnp.concatenate([vt[feat], ones_row, zeros], axis=0).astype(BF16)


def _in_proj(x, g, w, cos, sin, fqg, fkg, mqg, fb, consts, seq):
    T = x.shape[0]
    tm = TOK_TILE
    nt = T // tm
    tps = seq // tm
    row = lambda i: (i, 0)
    fix = lambda i: (0, 0)
    full = lambda a: pl.BlockSpec(a.shape, fix)
    tri, bd, bdm = consts
    widths = (256, 256, 512, 512, -512, 512, -512, MQ_SLAB)
    outs = [jax.ShapeDtypeStruct((T, n) if n > 0 else (-n, T), BF16) for n in widths]
    out_specs = [pl.BlockSpec((tm, n), row) if n > 0 else pl.BlockSpec((-n, tm), lambda i: (0, i))
                 for n in widths]
    return pl.pallas_call(
        functools.partial(_in_proj_kernel, tps),
        grid=(nt,),
        in_specs=[pl.BlockSpec((tm, D_MODEL), row), full(g), full(w),
                  pl.BlockSpec((tm, LANE), lambda i: (i % tps, 0)),
                  pl.BlockSpec((tm, LANE), lambda i: (i % tps, 0)),
                  full(fqg), full(fkg), full(mqg), full(fb), full(tri), full(bd), full(bdm)],
        out_specs=out_specs,
        out_shape=outs,
        scratch_shapes=[pltpu.VMEM((8, LANE), F32)],
        compiler_params=pltpu.CompilerParams(dimension_semantics=("arbitrary",),
                                             vmem_limit_bytes=VMEM_LIMIT),
        name="in_proj",
    )(x, g, w, cos, sin, fqg, fkg, mqg, fb, tri, bd, bdm)


def _retention_kernel(chunks, rq_ref, rk_ref, rv_ref, rg_ref, hm_ref, dst_ref, qd_ref, kd_ref, cd_ref,
                      bm_ref, g_ref, o_ref, state_ref):
    @pl.when(pl.program_id(1) == 0)
    def _():
        state_ref[...] = jnp.zeros_like(state_ref)

    C = RET_CHUNK
    for ci in range(chunks):
        rows = slice(ci * C, (ci + 1) * C)
        qc = rq_ref[rows, :]
        kc = rk_ref[rows, :]
        vc = rv_ref[rows, :]
        qs = jnp.concatenate([qc * hm_ref[hh:hh + 1, :] for hh in range(RET_HEADS)], axis=0)
        sc = lax.dot_general(qs, kc, (((1,), (1,)), ((), ())), preferred_element_type=F32)
        sc = (sc * dst_ref[...]).astype(BF16)
        state = state_ref[...]
        qdec = (qc.astype(F32) * qd_ref[...]).astype(BF16)
        cross = jnp.dot(qdec, state.astype(BF16), preferred_element_type=F32)
        kdec = (kc.astype(F32) * kd_ref[...]).astype(BF16)
        kv = lax.dot_general(kdec, vc, (((0,), (0,)), ((), ())), preferred_element_type=F32)
        state_ref[...] = state * cd_ref[...] + kv * bm_ref[...]
        for hh in range(RET_HEADS):
            cols = slice(hh * RET_DV, (hh + 1) * RET_DV)
            o = jnp.dot(sc[hh * C:(hh + 1) * C, :], vc[:, cols], preferred_element_type=F32) + cross[:, cols]
            msq = jnp.mean(o * o, axis=-1, keepdims=True)
            y = o * lax.rsqrt(msq + EPS) * g_ref[:, cols]
            o_ref[rows, cols] = (y * _silu(rg_ref[rows, cols].astype(F32))).astype(BF16)


def _retention(rq, rk, rv, rg, tables, g_ret, batch, seq):
    T = rq.shape[0]
    tm = TOK_TILE
    tps = seq // tm
    row = lambda b, i: (b * tps + i, 0)
    fix = lambda b, i: (0, 0)
    full = lambda a: pl.BlockSpec(a.shape, fix)
    return pl.pallas_call(
        functools.partial(_retention_kernel, tm // RET_CHUNK),
        grid=(batch, tps),
        in_specs=[pl.BlockSpec((tm, 256), row), pl.BlockSpec((tm, 256), row),
                  pl.BlockSpec((tm, 512), row), pl.BlockSpec((tm, 512), row)]
                 + [full(t) for t in tables] + [full(g_ret)],
        out_specs=pl.BlockSpec((tm, 512), row),
        out_shape=jax.ShapeDtypeStruct((T, 512), BF16),
        scratch_shapes=[pltpu.VMEM((RET_HEADS * RET_DK, RET_HEADS * RET_DV), F32)],
        compiler_params=pltpu.CompilerParams(dimension_semantics=("arbitrary", "arbitrary"),
                                             vmem_limit_bytes=VMEM_LIMIT),
        name="retention",
    )(rq, rk, rv, rg, *tables, g_ret)


def _fox_kernel(tq, tk, qt_ref, k_ref, vt_ref, g_ref, o_ref, m_ref, acc_ref, s0_ref, s1_ref):
    i = pl.program_id(2)
    qt = qt_ref[...]
    m_ref[...] = jnp.full_like(m_ref, NEG)
    acc_ref[...] = jnp.zeros_like(acc_ref)
    assert tq == 2 * tk
    s_refs = (s0_ref, s1_ref)

    def scores(j, slot):
        start = pl.multiple_of(j * tk, tk)
        s_refs[slot][...] = jnp.dot(k_ref[pl.ds(start, tk), :], qt, preferred_element_type=F32)

    def update(j, slot, diag):
        start = pl.multiple_of(j * tk, tk)
        st = s_refs[slot][...]
        if diag is not None:
            key = lax.broadcasted_iota(jnp.int32, st.shape, 0) + diag * tk
            qry = lax.broadcasted_iota(jnp.int32, st.shape, 1)
            st = jnp.where(key <= qry, st, NEG)
        m_prev = m_ref[...]
        m_new = jnp.maximum(m_prev, jnp.max(st, axis=0, keepdims=True))
        alpha = jnp.exp2(m_prev - m_new)
        p = jnp.exp2(st - m_new).astype(BF16)
        vt = vt_ref[:, pl.ds(start, tk)]
        acc_ref[...] = alpha * acc_ref[...] + jnp.dot(vt, p, preferred_element_type=F32)
        m_ref[...] = m_new

    scores(0, 0)

    def body(t, carry):
        scores(2 * t + 1, 1)
        update(2 * t, 0, None)
        scores(2 * t + 2, 0)
        update(2 * t + 1, 1, None)
        return carry

    lax.fori_loop(0, i, body, 0)
    scores(2 * i + 1, 1)
    update(2 * i, 0, 0)
    update(2 * i + 1, 1, 1)
    acc = acc_ref[...]
    row = lax.broadcasted_iota(jnp.int32, acc.shape, 0)
    o = jnp.where(row < FOX_DH, acc / acc[V_ONE:V_ONE + 1, :], 0.0)
    msq = jnp.sum(o * o, axis=0, keepdims=True) * (1.0 / FOX_DH)
    o_ref[...] = ((o * lax.rsqrt(msq + EPS)).T * g_ref[...]).astype(BF16)


def _fox(fqt, fka, fvt, g_fox, batch, seq):
    T = fka.shape[0]
    tq = FOX_Q_TILE
    tk = FOX_K_TILE
    nq = seq // tq
    return pl.pallas_call(
        functools.partial(_fox_kernel, tq, tk),
        grid=(batch, FOX_HEADS, nq),
        in_specs=[pl.BlockSpec((LANE, tq), lambda b, h, i: (h, b * nq + i)),
                  pl.BlockSpec((seq, LANE), lambda b, h, i: (b, h)),
                  pl.BlockSpec((LANE, seq), lambda b, h, i: (h, b)),
                  pl.BlockSpec((1, LANE), lambda b, h, i: (0, h))],
        out_specs=pl.BlockSpec((tq, LANE), lambda b, h, i: (b * nq + i, h)),
        out_shape=jax.ShapeDtypeStruct((T, FOX_HEADS * LANE), BF16),
        scratch_shapes=[pltpu.VMEM((1, tq), F32), pltpu.VMEM((LANE, tq), F32),
                        pltpu.VMEM((tk, tq), F32), pltpu.VMEM((tk, tq), F32)],
        compiler_params=pltpu.CompilerParams(dimension_semantics=("arbitrary", "arbitrary", "arbitrary"),
                                             vmem_limit_bytes=VMEM_LIMIT),
        name="fox",
    )(fqt, fka, fvt, g_fox)


def _mem_prep_kernel(mem_ref, g_ref, w_ref, kg_ref, bd_ref, shift_ref, mkt_ref, mvs_ref):
    x = mem_ref[0]
    ms = jnp.mean(x * x, axis=-1, keepdims=True)
    mn = (x * lax.rsqrt(ms + EPS) * g_ref[0]).astype(BF16)
    mkv = jnp.dot(mn, w_ref[0], preferred_element_type=F32)
    width = MEM_HEADS * MEM_DH
    mk = mkv[:, :width]
    mv = mkv[:, width:]
    mk = mk * lax.rsqrt(_group_mean_sq(mk, bd_ref[...]) + EPS) * kg_ref[0]
    mk = jnp.dot(mk.astype(BF16), shift_ref[...], preferred_element_type=F32)
    mkt = mk.T
    feat = (lax.broadcasted_iota(jnp.int32, mkt.shape, 0) - MQ_OFF) // MEM_DH
    lane_head = lax.broadcasted_iota(jnp.int32, mv.shape, 1) // MEM_DH
    M = x.shape[0]
    for hh in range(MEM_HEADS):
        mkt_ref[0, 0, :, hh * M:(hh + 1) * M] = jnp.where(feat == hh, mkt, 0.0).astype(BF16)
        mvs_ref[0, 0, hh * M:(hh + 1) * M, :] = jnp.where(lane_head == hh, mv, 0.0).astype(BF16)


def _mem_prep(mem, mem_norm_g, w_mem_kv, mem_k_g, bd, shift):
    B, M, _ = mem.shape
    depth = w_mem_kv.shape[0]
    width = MEM_HEADS * MEM_DH
    return pl.pallas_call(
        _mem_prep_kernel,
        grid=(depth, B),
        in_specs=[pl.BlockSpec((1, M, D_MODEL), lambda l, b: (b, 0, 0)),
                  pl.BlockSpec((1, 1, D_MODEL), lambda l, b: (l, 0, 0)),
                  pl.BlockSpec((1, D_MODEL, 2 * width), lambda l, b: (l, 0, 0)),
                  pl.BlockSpec((1, 1, width), lambda l, b: (l, 0, 0)),
                  pl.BlockSpec(bd.shape, lambda l, b: (0, 0)),
                  pl.BlockSpec(shift.shape, lambda l, b: (0, 0))],
        out_specs=[pl.BlockSpec((1, 1, MQ_SLAB, MEM_HEADS * M), lambda l, b: (l, b, 0, 0)),
                   pl.BlockSpec((1, 1, MEM_HEADS * M, width), lambda l, b: (l, b, 0, 0))],
        out_shape=[jax.ShapeDtypeStruct((depth, B, MQ_SLAB, MEM_HEADS * M), BF16),
                   jax.ShapeDtypeStruct((depth, B, MEM_HEADS * M, width), BF16)],
        compiler_params=pltpu.CompilerParams(dimension_semantics=("arbitrary", "arbitrary"),
                                             vmem_limit_bytes=VMEM_LIMIT),
        name="mem_prep",
    )(mem, mem_norm_g, w_mem_kv, mem_k_g, bd, shift)


def _out_proj_body(ret_ref, fox_ref, mq_ref, mkt_ref, mvs_ref, bd_ref, gm_ref, w_ref, x_ref, fg_ref):
    M = mkt_ref.shape[-1] // MEM_HEADS
    s = jnp.dot(mq_ref[...], mkt_ref[0, 0], preferred_element_type=F32)
    mem_o = None
    for hh in range(MEM_HEADS):
        sh = s[:, hh * M:(hh + 1) * M]
        p = jnp.exp(sh - jnp.max(sh, axis=-1, keepdims=True))
        p = (p / jnp.sum(p, axis=-1, keepdims=True)).astype(BF16)
        part = jnp.dot(p, mvs_ref[0, 0, hh * M:(hh + 1) * M, :], preferred_element_type=F32)
        mem_o = part if mem_o is None else mem_o + part
    mem_o = (mem_o * lax.rsqrt(_group_mean_sq(mem_o, bd_ref[...]) + EPS) * gm_ref[...]).astype(BF16)
    y = (jnp.dot(ret_ref[...], w_ref[0:512, :], preferred_element_type=F32)
         + jnp.dot(fox_ref[...], w_ref[512:1024, :], preferred_element_type=F32)
         + jnp.dot(mem_o, w_ref[1024:1280, :], preferred_element_type=F32))
    xm = x_ref[...] + y
    ms = jnp.mean(xm * xm, axis=-1, keepdims=True)
    return xm, xm * lax.rsqrt(ms + EPS) * fg_ref[...]


def _out_proj_dense_kernel(ret_ref, fox_ref, mq_ref, mkt_ref, mvs_ref, bd_ref, gm_ref, w_ref, x_ref, fg_ref,
                           xm_ref, h_ref):
    xm, hn = _out_proj_body(ret_ref, fox_ref, mq_ref, mkt_ref, mvs_ref, bd_ref, gm_ref, w_ref, x_ref, fg_ref)
    xm_ref[...] = xm
    h_ref[...] = hn.astype(BF16)


def _out_proj_moe_kernel(ret_ref, fox_ref, mq_ref, mkt_ref, mvs_ref, bd_ref, gm_ref, w_ref, x_ref, fg_ref,
                         rwt_ref, tri_ref, xm_ref, h_ref, routet_ref, route_ref, cnt_ref, carry_ref):
    @pl.when(pl.program_id(0) == 0)
    def _():
        carry_ref[...] = jnp.zeros_like(carry_ref)

    xm, hn = _out_proj_body(ret_ref, fox_ref, mq_ref, mkt_ref, mvs_ref, bd_ref, gm_ref, w_ref, x_ref, fg_ref)
    xm_ref[...] = xm
    _to_row_tiles(h_ref, hn)
    tm = hn.shape[0]
    h_hi = hn.astype(BF16)
    h_lo = (hn - h_hi.astype(F32)).astype(BF16)
    rwt = rwt_ref[...]
    nt = (((1,), (1,)), ((), ()))
    a = lax.dot_general(rwt, h_hi, nt, preferred_element_type=F32)
    b = lax.dot_general(rwt[0:16], h_lo, nt, preferred_element_type=F32)
    lg = a[0:N_EXPERTS] + a[16:16 + N_EXPERTS] + b[0:N_EXPERTS]
    row = lax.broadcasted_iota(jnp.int32, lg.shape, 0).astype(F32)
    m1 = jnp.max(lg, axis=0, keepdims=True)
    i1 = jnp.min(jnp.where(lg == m1, row, float(N_EXPERTS)), axis=0, keepdims=True)
    oh1 = row == i1
    lg2 = jnp.where(oh1, NEG, lg)
    m2 = jnp.max(lg2, axis=0, keepdims=True)
    i2 = jnp.min(jnp.where(lg2 == m2, row, float(N_EXPERTS)), axis=0, keepdims=True)
    oh2 = row == i2
    e = jnp.exp(m2 - m1)
    g1 = 1.0 / (1.0 + e)
    g2 = e / (1.0 + e)
    oh = jnp.where(oh1 | oh2, 1.0, 0.0)
    oh16 = jnp.concatenate([oh, jnp.zeros_like(oh)], axis=0).astype(BF16)
    before = carry_ref[:, 0:1] + jnp.dot(oh16, tri_ref[...], preferred_element_type=F32)[0:N_EXPERTS]
    r1 = jnp.sum(jnp.where(oh1, before, 0.0), axis=0, keepdims=True)
    r2 = jnp.sum(jnp.where(oh2, before, 0.0), axis=0, keepdims=True)
    total = before[:, tm - 1:tm] + oh[:, tm - 1:tm]
    carry_ref[...] = jnp.broadcast_to(total, carry_ref.shape)
    cnt_ref[...] = jnp.broadcast_to(total, cnt_ref.shape)
    rt = jnp.zeros_like(lg)
    for r, val in enumerate((i1, i2, r1, r2, g1, g2)):
        rt = jnp.where(row == r, val, rt)
    routet_ref[...] = rt
    route_ref[...] = jnp.concatenate([rt, jnp.zeros((LANE - N_EXPERTS, tm), F32)], axis=0).T


def _out_proj(ret_o, fox_o, mq, mkt, mvs, bd, g_mem, w_out, x, ffn_g, layer, seq, router=None):
    T = x.shape[0]
    tm = TOK_TILE
    tps = seq // tm
    row = lambda i: (i, 0)
    fix = lambda i: (0, 0)
    full = lambda a: pl.BlockSpec(a.shape, fix)
    M4 = mkt.shape[-1]
    in_specs = [pl.BlockSpec((tm, 512), row), pl.BlockSpec((tm, 512), row), pl.BlockSpec((tm, MQ_SLAB), row),
                pl.BlockSpec((1, 1, MQ_SLAB, M4), lambda i: (layer, i // tps, 0, 0)),
                pl.BlockSpec((1, 1, M4, 256), lambda i: (layer, i // tps, 0, 0)),
                full(bd), full(g_mem), full(w_out), pl.BlockSpec((tm, D_MODEL), row), full(ffn_g)]
    args = [ret_o, fox_o, mq, mkt, mvs, bd, g_mem, w_out, x, ffn_g]
    params = pltpu.CompilerParams(dimension_semantics=("arbitrary",), vmem_limit_bytes=VMEM_LIMIT)
    if router is None:
        return pl.pallas_call(
            _out_proj_dense_kernel, grid=(T // tm,), in_specs=in_specs,
            out_specs=[pl.BlockSpec((tm, D_MODEL), row), pl.BlockSpec((tm, D_MODEL), row)],
            out_shape=[jax.ShapeDtypeStruct((T, D_MODEL), F32), jax.ShapeDtypeStruct((T, D_MODEL), BF16)],
            compiler_params=params, name="out_proj_dense",
        )(*args)
    rw, tri = router
    return pl.pallas_call(
        _out_proj_moe_kernel, grid=(T // tm,), in_specs=in_specs + [full(rw), full(tri)],
        out_specs=[pl.BlockSpec((tm, D_MODEL), row), pl.BlockSpec((tm * ROW_TILE, LANE), row),
                   pl.BlockSpec((N_EXPERTS, tm), lambda i: (0, i)),
                   pl.BlockSpec((tm, LANE), row), pl.BlockSpec((8, LANE), fix)],
        out_shape=[jax.ShapeDtypeStruct((T, D_MODEL), F32), jax.ShapeDtypeStruct((T * ROW_TILE, LANE), F32),
                   jax.ShapeDtypeStruct((N_EXPERTS, T), F32),
                   jax.ShapeDtypeStruct((T, LANE), F32), jax.ShapeDtypeStruct((8, LANE), F32)],
        scratch_shapes=[pltpu.VMEM((8, LANE), F32)],
        compiler_params=params, name="out_proj_moe",
    )(*args, rw, tri)


def _swiglu(x, wg_ref, wu_ref, wd_ref, y):
    for c in range(D_FF_PAD // FF_TILE):
        cols = slice(c * FF_TILE, (c + 1) * FF_TILE)
        g = jnp.dot(x, wg_ref[:, cols], preferred_element_type=F32)
        u = jnp.dot(x, wu_ref[:, cols], preferred_element_type=F32)
        a = (_silu(g) * u).astype(BF16)
        part = jnp.dot(a, wd_ref[cols, :], preferred_element_type=F32)
        y = part if y is None else y + part
    return y


def _dense_ffn_kernel(h_ref, wg_ref, wu_ref, wd_ref, x_ref, o_ref):
    o_ref[...] = _swiglu(h_ref[...], wg_ref.at[0], wu_ref.at[0], wd_ref.at[0], x_ref[...])


def _dense_ffn(h, wg, wu, wd, x, layer):
    T = h.shape[0]
    tm = TOK_TILE
    once = pl.Buffered(1)
    return pl.pallas_call(
        _dense_ffn_kernel,
        grid=(T // tm,),
        in_specs=[pl.BlockSpec((tm, D_MODEL), lambda i: (i, 0)),
                  pl.BlockSpec((1, D_MODEL, D_FF_PAD), lambda i: (layer, 0, 0), pipeline_mode=once),
                  pl.BlockSpec((1, D_MODEL, D_FF_PAD), lambda i: (layer, 0, 0), pipeline_mode=once),
                  pl.BlockSpec((1, D_FF_PAD, D_MODEL), lambda i: (layer, 0, 0), pipeline_mode=once),
                  pl.BlockSpec((tm, D_MODEL), lambda i: (i, 0))],
        out_specs=pl.BlockSpec((tm, D_MODEL), lambda i: (i, 0)),
        out_shape=jax.ShapeDtypeStruct((T, D_MODEL), F32),
        compiler_params=pltpu.CompilerParams(dimension_semantics=("arbitrary",),
                                             vmem_limit_bytes=VMEM_LIMIT),
        name="dense_ffn",
    )(h, wg, wu, wd, x)


def _to_row_tiles(ref, x):
    for s in range(ROW_TILE):
        ref[pl.ds(s, x.shape[0], stride=ROW_TILE), :] = x[:, s * LANE:(s + 1) * LANE]


def _from_row_tiles(ref, n):
    return jnp.concatenate([ref[pl.ds(s, n, stride=ROW_TILE), :] for s in range(ROW_TILE)], axis=1)


def _row_copy(src, s, dst, d, sem):
    return pltpu.make_async_copy(src.at[pl.ds(pl.multiple_of(s * ROW_TILE, ROW_TILE), ROW_TILE)],
                                 dst.at[pl.ds(pl.multiple_of(d * ROW_TILE, ROW_TILE), ROW_TILE)], sem)


def _moe_ffn_kernel(te_ref, nu_ref, src_ref, h_hbm, wg_ref, wu_ref, wd_ref, o_ref, x_ref, sem):
    i = pl.program_id(0)
    n_used = nu_ref[0]
    tm = x_ref.shape[1] // ROW_TILE
    slot = i % 2
    nxt = 1 - slot

    def tile_wait(s):
        pltpu.make_async_copy(h_hbm.at[pl.ds(0, tm * ROW_TILE)], x_ref.at[s], sem.at[s]).wait()

    @pl.when(i == 0)
    def _():
        def issue(r, carry):
            _row_copy(h_hbm, src_ref[r], x_ref.at[0], r, sem.at[0]).start()
            return carry

        lax.fori_loop(0, tm, issue, 0, unroll=8)

    @pl.when(i < n_used)
    def _():
        tile_wait(slot)
        x = _from_row_tiles(x_ref.at[slot], tm).astype(BF16)
        base = (i + 1) * tm
        for r in range(tm):
            _row_copy(h_hbm, src_ref[base + r], x_ref.at[nxt], r, sem.at[nxt]).start(priority=r % 2)
        _to_row_tiles(o_ref, _swiglu(x, wg_ref.at[0, 0], wu_ref.at[0, 0], wd_ref.at[0, 0], None))

        @pl.when(i + 1 == n_used)
        def _():
            tile_wait(nxt)

    @pl.when(i >= n_used)
    def _():
        o_ref[...] = jnp.zeros_like(o_ref)


def _moe_ffn(h, src, tile_expert, n_used, wg, wu, wd, layer):
    tm = MOE_TILE
    n_tiles = tile_expert.shape[0]
    assert src.shape[0] == (n_tiles + 1) * tm
    w_block = lambda i, te, nu, sr: (layer, te[i], 0, 0)
    return pl.pallas_call(
        _moe_ffn_kernel,
        grid_spec=pltpu.PrefetchScalarGridSpec(
            num_scalar_prefetch=3, grid=(n_tiles,),
            in_specs=[pl.BlockSpec(memory_space=pl.ANY),
                      pl.BlockSpec((1, 1, D_MODEL, D_FF_PAD), w_block),
                      pl.BlockSpec((1, 1, D_MODEL, D_FF_PAD), w_block),
                      pl.BlockSpec((1, 1, D_FF_PAD, D_MODEL), w_block)],
            out_specs=pl.BlockSpec((tm * ROW_TILE, LANE), lambda i, te, nu, sr: (i, 0)),
            scratch_shapes=[pltpu.VMEM((2, tm * ROW_TILE, LANE), F32), pltpu.SemaphoreType.DMA((2,))]),
        out_shape=jax.ShapeDtypeStruct((n_tiles * tm * ROW_TILE, LANE), F32),
        compiler_params=pltpu.CompilerParams(dimension_semantics=("arbitrary",),
                                             vmem_limit_bytes=VMEM_LIMIT),
        name="moe_ffn",
    )(tile_expert, n_used, src, h, wg, wu, wd)


def _combine_kernel(tc, d1_ref, d2_ref, o_hbm, x_ref, route_ref, out_ref, b1_ref, b2_ref, sem):
    base = pl.program_id(0) * tc

    def issue(r, carry):
        t = base + r
        _row_copy(o_hbm, d1_ref[t], b1_ref, r, sem.at[0]).start(priority=0)
        _row_copy(o_hbm, d2_ref[t], b2_ref, r, sem.at[1]).start(priority=1)
        return carry

    lax.fori_loop(0, tc, issue, 0, unroll=8)
    pltpu.make_async_copy(o_hbm.at[pl.ds(0, tc * ROW_TILE)], b1_ref, sem.at[0]).wait()
    pltpu.make_async_copy(o_hbm.at[pl.ds(0, tc * ROW_TILE)], b2_ref, sem.at[1]).wait()
    route = route_ref[...]
    g1 = route[:, 4:5]
    g2 = route[:, 5:6]
    out_ref[...] = x_ref[...] + g1 * _from_row_tiles(b1_ref, tc) + g2 * _from_row_tiles(b2_ref, tc)


def _combine(o, d1, d2, x, route):
    T = x.shape[0]
    tc = 256
    return pl.pallas_call(
        functools.partial(_combine_kernel, tc),
        grid_spec=pltpu.PrefetchScalarGridSpec(
            num_scalar_prefetch=2, grid=(T // tc,),
            in_specs=[pl.BlockSpec(memory_space=pl.ANY),
                      pl.BlockSpec((tc, D_MODEL), lambda i, a, b: (i, 0)),
                      pl.BlockSpec((tc, LANE), lambda i, a, b: (i, 0))],
            out_specs=pl.BlockSpec((tc, D_MODEL), lambda i, a, b: (i, 0)),
            scratch_shapes=[pltpu.VMEM((tc * ROW_TILE, LANE), F32), pltpu.VMEM((tc * ROW_TILE, LANE), F32),
                            pltpu.SemaphoreType.DMA((2,))]),
        out_shape=jax.ShapeDtypeStruct((T, D_MODEL), F32),
        compiler_params=pltpu.CompilerParams(dimension_semantics=("arbitrary",),
                                             vmem_limit_bytes=VMEM_LIMIT),
        name="moe_combine",
    )(d1, d2, o, x, route)


def _plan_kernel(rows, d1_ref, d2_ref, src_ref):
    def clear(r, carry):
        src_ref[r] = 0
        return carry

    def place(t, carry):
        src_ref[d1_ref[t]] = t
        src_ref[d2_ref[t]] = t
        return carry

    lax.fori_loop(0, rows, clear, 0, unroll=8)
    lax.fori_loop(0, d1_ref.shape[0], place, 0, unroll=8)


def _plan(d1, d2, rows):
    return pl.pallas_call(
        functools.partial(_plan_kernel, rows),
        grid_spec=pltpu.PrefetchScalarGridSpec(
            num_scalar_prefetch=2, grid=(1,), in_specs=[],
            out_specs=pl.BlockSpec(memory_space=pltpu.SMEM)),
        out_shape=jax.ShapeDtypeStruct((rows,), jnp.int32),
        name="moe_plan",
    )(d1, d2)


def _moe_layer(h, xm, routet, route, counts, wg, wu, wd, layer):
    T = xm.shape[0]
    tm = MOE_TILE
    n_tiles = (2 * T) // tm + N_EXPERTS
    e1 = routet[0].astype(jnp.int32)
    e2 = routet[1].astype(jnp.int32)
    r1 = routet[2].astype(jnp.int32)
    r2 = routet[3].astype(jnp.int32)
    cnt = counts[:, 0].astype(jnp.int32)
    tiles = (cnt + tm - 1) // tm
    tile_end = jnp.cumsum(tiles)
    offs = (tile_end - tiles) * tm
    d1 = offs[e1] + r1
    d2 = offs[e2] + r2
    n_used = tile_end[-1:]
    tile_id = jnp.arange(n_tiles, dtype=jnp.int32)
    tile_expert = jnp.sum(tile_id[:, None] >= tile_end[None, :], axis=1).astype(jnp.int32)
    tile_expert = jnp.minimum(tile_expert, tile_expert[jnp.maximum(n_used[0] - 1, 0)])
    src = _plan(d1, d2, (n_tiles + 1) * tm)
    o = _moe_ffn(h, src, tile_expert, n_used.astype(jnp.int32), wg, wu, wd, layer)
    return _combine(o, d1, d2, xm, route)


def _constants(seq):
    tm = TOK_TILE
    r = np.arange(tm)
    tri_upper = (r[:, None] <= r[None, :]).astype(np.float32)
    tri_strict = (r[:, None] < r[None, :]).astype(np.float32)
    d = np.arange(256)
    bd = (d[:, None] // 64 == d[None, :] // 64).astype(np.float32) / 64.0
    m = np.arange(MQ_SLAB) - MQ_OFF
    valid = (m >= 0) & (m < MEM_HEADS * MEM_DH)
    bdm = ((m[:, None] // MEM_DH == m[None, :] // MEM_DH) & valid[:, None] & valid[None, :]).astype(np.float32) / MEM_DH
    shift = (d[:, None] + MQ_OFF == np.arange(MQ_SLAB)[None, :]).astype(np.float32)
    in_consts = (jnp.asarray(tri_upper, BF16), jnp.asarray(bd, BF16), jnp.asarray(bdm, BF16))

    half = RET_DK // 2
    pos = jnp.arange(seq, dtype=F32)
    inv_freq = ROPE_BASE ** (-jnp.arange(half, dtype=F32) / half)
    ang = pos[:, None] * inv_freq[None, :]
    cos = jnp.tile(jnp.cos(ang), (1, LANE // half))
    sin = jnp.tile(jnp.concatenate([-jnp.sin(ang), jnp.sin(ang)], axis=1), (1, LANE // RET_DK))

    C = RET_CHUNK
    hh = jnp.arange(RET_HEADS, dtype=F32)
    log_g = jnp.log(1.0 - 2.0 ** (-5.0 - hh))
    idx = jnp.arange(C, dtype=F32)
    diff = idx[:, None] - idx[None, :]
    dmat = jnp.where(diff >= 0, jnp.exp(log_g[:, None, None] * jnp.maximum(diff, 0.0)), 0.0)
    dstack = dmat.reshape(RET_HEADS * C, C)
    k_head = np.arange(RET_HEADS * RET_DK) // RET_DK
    v_head = np.arange(RET_HEADS * RET_DV) // RET_DV
    qd = jnp.exp(log_g[k_head][None, :] * (idx[:, None] + 1.0))
    kd = jnp.exp(log_g[k_head][None, :] * (C - 1.0 - idx[:, None]))
    cd = jnp.exp(log_g[v_head] * C)[None, :]
    bm = jnp.asarray((k_head[:, None] == v_head[None, :]).astype(np.float32))
    hm = jnp.asarray((np.arange(RET_HEADS)[:, None] == k_head[None, :]).astype(np.float32), BF16)
    ret_tables = (hm, dstack, qd, kd, cd, bm)
    return in_consts, cos, sin, ret_tables, jnp.asarray(tri_strict, BF16), jnp.asarray(shift, BF16)


def _pad_heads(g):
    depth = g.shape[0]
    g = g.reshape(depth, FOX_HEADS, FOX_DH)
    return jnp.pad(g, ((0, 0), (0, 0), (0, LANE - FOX_DH))).reshape(depth, 1, FOX_HEADS * LANE)


def kernel(x, mem, attn_norm_g, w_in, forget_bias, fox_q_norm_g, fox_k_norm_g, mem_q_norm_g, mem_k_norm_g,
           mem_norm_g, w_mem_kv, out_norm_g, w_out, ffn_norm_g, dense_w_gate, dense_w_up, dense_w_down,
           router_w, expert_w_gate, expert_w_up, expert_w_down):
    B, S, D = x.shape
    depth = w_in.shape[0]
    T = B * S
    in_consts, cos, sin, ret_tables, tri_strict, shift = _constants(S)
    bd = in_consts[1]

    w_in_p = jnp.pad(w_in, ((0, 0), (0, 0), (0, P_IN - w_in.shape[-1]))).astype(BF16)
    fqg = jnp.tile(fox_q_norm_g, (1, FOX_HEADS))[:, None, :]
    fkg = jnp.tile(fox_k_norm_g, (1, FOX_HEADS))[:, None, :]
    fb = jnp.pad(forget_bias, ((0, 0), (0, 8 - FOX_HEADS)))[:, :, None]
    mqg = jnp.tile(mem_q_norm_g, (1, MEM_HEADS))
    mqg = jnp.pad(mqg, ((0, 0), (MQ_OFF, MQ_SLAB - MQ_OFF - mqg.shape[-1])))[:, None, :]
    mkg = jnp.tile(mem_k_norm_g, (1, MEM_HEADS))[:, None, :]
    n_ret = RET_HEADS * RET_DV
    n_fox = FOX_HEADS * FOX_DH
    g_ret = out_norm_g[:, None, :n_ret]
    g_fox = _pad_heads(out_norm_g[:, n_ret:n_ret + n_fox])
    g_mem = out_norm_g[:, None, n_ret + n_fox:]
    w_fox = w_out[:, n_ret:n_ret + n_fox].reshape(depth, FOX_HEADS, FOX_DH, D)
    w_fox = jnp.pad(w_fox, ((0, 0), (0, 0), (0, LANE - FOX_DH), (0, 0))).reshape(depth, FOX_HEADS * LANE, D)
    w_out_p = jnp.concatenate([w_out[:, :n_ret], w_fox, w_out[:, n_ret + n_fox:]], axis=1).astype(BF16)
    ffpad = D_FF_PAD - D_FF
    pad_cols = lambda w: jnp.pad(w, [(0, 0)] * (w.ndim - 1) + [(0, ffpad)]).astype(BF16)
    pad_rows = lambda w: jnp.pad(w, [(0, 0)] * (w.ndim - 2) + [(0, ffpad), (0, 0)]).astype(BF16)
    dwg, dwu, dwd = pad_cols(dense_w_gate), pad_cols(dense_w_up), pad_rows(dense_w_down)
    ewg, ewu, ewd = pad_cols(expert_w_gate), pad_cols(expert_w_up), pad_rows(expert_w_down)
    rwt = jnp.swapaxes(router_w, 1, 2)
    rwt_hi = rwt.astype(BF16)
    rwt_lo = (rwt - rwt_hi.astype(F32)).astype(BF16)
    pad16 = lambda w: jnp.pad(w, ((0, 0), (0, 16 - N_EXPERTS), (0, 0)))
    rw = jnp.concatenate([pad16(rwt_hi), pad16(rwt_lo)], axis=1)

    mkt, mvs = _mem_prep(mem, mem_norm_g[:, None, :], w_mem_kv.astype(BF16), mkg, bd, shift)

    xt = x.reshape(T, D)
    for l in range(depth):
        rq, rk, rv, rg, fqt, fka, fvt, mq = _in_proj(
            xt, attn_norm_g[l][None], w_in_p[l], cos, sin, fqg[l], fkg[l], mqg[l], fb[l], in_consts, S)
        ret_o = _retention(rq, rk, rv, rg, ret_tables, g_ret[l], B, S)
        fox_o = _fox(fqt, fka, fvt, g_fox[l], B, S)
        j = l // 2
        if l % 2 == 0:
            xm, h2 = _out_proj(ret_o, fox_o, mq, mkt, mvs, bd, g_mem[l], w_out_p[l], xt, ffn_norm_g[l][None], l, S)
            xt = _dense_ffn(h2, dwg, dwu, dwd, xm, j)
        else:
            xm, h2, routet, route, counts = _out_proj(ret_o, fox_o, mq, mkt, mvs, bd, g_mem[l], w_out_p[l], xt,
                                                      ffn_norm_g[l][None], l, S, router=(rw[j], tri_strict))
            xt = _moe_layer(h2, xm, routet, route, counts, ewg, ewu, ewd, j)
    return xt.reshape(B, S, D)
```

```python
import functools

import numpy as np
import jax
import jax.numpy as jnp
from jax import lax
from jax.experimental import pallas as pl
from jax.experimental.pallas import tpu as pltpu

F32 = jnp.float32
BF16 = jnp.bfloat16

D_MODEL = 1024
RET_HEADS = 4
RET_DK = 64
RET_DV = 128
FOX_HEADS = 4
FOX_DH = 64
MEM_HEADS = 4
MEM_DH = 64
RET_CHUNK = 128
D_FF = 2752
N_EXPERTS = 8
ROPE_BASE = 10000.0
EPS = 1e-6

LANE = 128
FF_CHUNKS = ((0, 1408), (1408, 2688), (2688, D_FF))
TOK_TILE = 512
FOX_Q_TILE = 1024
FOX_K_TILE = 512
LOG2E = 1.4426950408889634
ROW_TILE = 8
assert D_MODEL == ROW_TILE * LANE
MOE_TILE = 512
VMEM_LIMIT = 56 * 1024 * 1024
NEG = -1e30

RQ, RK, RV, RG = 0, 256, 512, 1024
FQ, FK, FV = 1536, 1792, 2048
FF_COL = 2304
MQ_OFF = FOX_HEADS
MQ_SLAB = 384
P_IN = FF_COL + MQ_SLAB
N_PIECE = 3
V_ONE = 64


def _split3(x):
    hi = x.astype(BF16)
    r1 = x - hi.astype(F32)
    mid = r1.astype(BF16)
    lo = (r1 - mid.astype(F32)).astype(BF16)
    return hi, mid, lo


def _group_mean_sq(x, bd):
    sq = x * x
    hi = sq.astype(BF16)
    lo = (sq - hi.astype(F32)).astype(BF16)
    return (jnp.dot(hi, bd, preferred_element_type=F32)
            + jnp.dot(lo, bd, preferred_element_type=F32))


def _silu(x):
    return x / (1.0 + jnp.exp(-x))


def _in_proj_kernel(tiles_per_seq, x_ref, g_ref, w_ref, cos_ref, sin_ref, fqg_ref, fkg_ref, mqg_ref,
                    fb_ref, tri_ref, bd_ref, bdm_ref,
                    rq_ref, rk_ref, rv_ref, rg_ref, fqt_ref, fka_ref, fvt_ref, mq_ref, carry_ref):
    i = pl.program_id(0)

    @pl.when(i % tiles_per_seq == 0)
    def _():
        carry_ref[...] = jnp.zeros_like(carry_ref)

    x = x_ref[...]
    ms = jnp.mean(x * x, axis=-1, keepdims=True)
    h = (x * lax.rsqrt(ms + EPS) * g_ref[...]).astype(BF16)

    def proj(a, b):
        return jnp.dot(h, w_ref[:, a:b], preferred_element_type=F32)

    tm = x.shape[0]
    cos = cos_ref[...]
    sin = sin_ref[...]
    first = (lax.broadcasted_iota(jnp.int32, (tm, LANE), 1) % RET_DK) < RET_DK // 2
    for base, ref, scale in ((RQ, rq_ref, RET_DK ** -0.5), (RK, rk_ref, 1.0)):
        for a in (0, LANE):
            t = proj(base + a, base + a + LANE)
            partner = jnp.where(first, pltpu.roll(t, LANE - RET_DK // 2, 1), pltpu.roll(t, RET_DK // 2, 1))
            ref[:, a:a + LANE] = ((t * cos + partner * sin) * scale).astype(BF16)
    rv_ref[...] = proj(RV, RV + 512).astype(BF16)
    rg_ref[...] = proj(RG, RG + 512).astype(BF16)

    zm = proj(FF_COL, FF_COL + MQ_SLAB)
    mq = zm * lax.rsqrt(_group_mean_sq(zm, bdm_ref[...]) + EPS) * mqg_ref[...] * (MEM_DH ** -0.5)
    mq_ref[...] = mq.astype(BF16)

    z = zm[:, 0:LANE].T[0:8, :] + fb_ref[...]
    logf = jnp.minimum(z, 0.0) - jnp.log(1.0 + jnp.exp(-jnp.abs(z)))
    hrow = lax.broadcasted_iota(jnp.int32, logf.shape, 0)
    logf = jnp.where(hrow < FOX_HEADS, logf, 0.0)
    tri = tri_ref[...]
    c = carry_ref[:, 0:1]
    for piece in _split3(logf):
        c = c + jnp.dot(piece, tri, preferred_element_type=F32)
    carry_ref[...] = jnp.broadcast_to(c[:, tm - 1:tm], carry_ref.shape)

    pieces = [p.astype(F32) for p in _split3(c * LOG2E)]
    bd = bd_ref[...]
    fq = proj(FQ, FQ + 256)
    fk = proj(FK, FK + 256)
    qt = (fq * lax.rsqrt(_group_mean_sq(fq, bd) + EPS) * fqg_ref[...] * (FOX_DH ** -0.5 * LOG2E)).T
    kt = (fk * lax.rsqrt(_group_mean_sq(fk, bd) + EPS) * fkg_ref[...]).T
    vt = proj(FV, FV + 256).T
    erow = lax.broadcasted_iota(jnp.int32, (8, tm), 0)
    zeros = jnp.zeros((LANE - FOX_DH - 8, tm), F32)
    ones_row = jnp.where(erow == V_ONE - FOX_DH, 1.0, 0.0)
    for hh in range(FOX_HEADS):
        feat = slice(hh * FOX_DH, (hh + 1) * FOX_DH)
        rows = slice(hh * LANE, (hh + 1) * LANE)
        eq = jnp.where(erow < 2 * N_PIECE, 1.0, 0.0)
        ek = eq
        for p in range(N_PIECE):
            c_row = pieces[p][hh:hh + 1, :]
            eq = jnp.where(erow == p, c_row, eq)
            ek = jnp.where(erow == N_PIECE + p, -c_row, ek)
        fqt_ref[rows, :] = jnp.concatenate([qt[feat], eq, zeros], axis=0).astype(BF16)
        fka_ref[:, rows] = jnp.concatenate([kt[feat], ek, zeros], axis=0).T.astype(BF16)
        fvt_ref[rows, :] = jnp.concatenate([vt[feat], ones_row, zeros], axis=0).astype(BF16)


def _in_proj(x, g, w, cos, sin, fqg, fkg, mqg, fb, consts, seq):
    T = x.shape[0]
    tm = TOK_TILE
    nt = T // tm
    tps = seq // tm
    row = lambda i: (i, 0)
    fix = lambda i: (0, 0)
    full = lambda a: pl.BlockSpec(a.shape, fix)
    tri, bd, bdm = consts
    widths = (256, 256, 512, 512, -512, 512, -512, MQ_SLAB)
    outs = [jax.ShapeDtypeStruct((T, n) if n > 0 else (-n, T), BF16) for n in widths]
    out_specs = [pl.BlockSpec((tm, n), row) if n > 0 else pl.BlockSpec((-n, tm), lambda i: (0, i))
                 for n in widths]
    return pl.pallas_call(
        functools.partial(_in_proj_kernel, tps),
        grid=(nt,),
        in_specs=[pl.BlockSpec((tm, D_MODEL), row), full(g), full(w),
                  pl.BlockSpec((tm, LANE), lambda i: (i % tps, 0)),
                  pl.BlockSpec((tm, LANE), lambda i: (i % tps, 0)),
                  full(fqg), full(fkg), full(mqg), full(fb), full(tri), full(bd), full(bdm)],
        out_specs=out_specs,
        out_shape=outs,
        scratch_shapes=[pltpu.VMEM((8, LANE), F32)],
        compiler_params=pltpu.CompilerParams(dimension_semantics=("arbitrary",),
                                             vmem_limit_bytes=VMEM_LIMIT),
        name="in_proj",
    )(x, g, w, cos, sin, fqg, fkg, mqg, fb, tri, bd, bdm)


def _retention_kernel(chunks, rq_ref, rk_ref, rv_ref, rg_ref, hm_ref, dst_ref, qd_ref, kd_ref, cd_ref,
                      bm_ref, g_ref, o_ref, state_ref):
    @pl.when(pl.program_id(1) == 0)
    def _():
        state_ref[...] = jnp.zeros_like(state_ref)

    C = RET_CHUNK
    for ci in range(chunks):
        rows = slice(ci * C, (ci + 1) * C)
        qc = rq_ref[rows, :]
        kc = rk_ref[rows, :]
        vc = rv_ref[rows, :]
        qs = jnp.concatenate([qc * hm_ref[hh:hh + 1, :] for hh in range(RET_HEADS)], axis=0)
        sc = lax.dot_general(qs, kc, (((1,), (1,)), ((), ())), preferred_element_type=F32)
        sc = (sc * dst_ref[...]).astype(BF16)
        state = state_ref[...]
        qdec = (qc.astype(F32) * qd_ref[...]).astype(BF16)
        cross = jnp.dot(qdec, state.astype(BF16), preferred_element_type=F32)
        kdec = (kc.astype(F32) * kd_ref[...]).astype(BF16)
        kv = lax.dot_general(kdec, vc, (((0,), (0,)), ((), ())), preferred_element_type=F32)
        state_ref[...] = state * cd_ref[...] + kv * bm_ref[...]
        for hh in range(RET_HEADS):
            cols = slice(hh * RET_DV, (hh + 1) * RET_DV)
            o = jnp.dot(sc[hh * C:(hh + 1) * C, :], vc[:, cols], preferred_element_type=F32) + cross[:, cols]
            msq = jnp.mean(o * o, axis=-1, keepdims=True)
            y = o * lax.rsqrt(msq + EPS) * g_ref[:, cols]
            o_ref[rows, cols] = (y * _silu(rg_ref[rows, cols].astype(F32))).astype(BF16)


def _retention(rq, rk, rv, rg, tables, g_ret, batch, seq):
    T = rq.shape[0]
    tm = TOK_TILE
    tps = seq // tm
    row = lambda b, i: (b * tps + i, 0)
    fix = lambda b, i: (0, 0)
    full = lambda a: pl.BlockSpec(a.shape, fix)
    return pl.pallas_call(
        functools.partial(_retention_kernel, tm // RET_CHUNK),
        grid=(batch, tps),
        in_specs=[pl.BlockSpec((tm, 256), row), pl.BlockSpec((tm, 256), row),
                  pl.BlockSpec((tm, 512), row), pl.BlockSpec((tm, 512), row)]
                 + [full(t) for t in tables] + [full(g_ret)],
        out_specs=pl.BlockSpec((tm, 512), row),
        out_shape=jax.ShapeDtypeStruct((T, 512), BF16),
        scratch_shapes=[pltpu.VMEM((RET_HEADS * RET_DK, RET_HEADS * RET_DV), F32)],
        compiler_params=pltpu.CompilerParams(dimension_semantics=("arbitrary", "arbitrary"),
                                             vmem_limit_bytes=VMEM_LIMIT),
        name="retention",
    )(rq, rk, rv, rg, *tables, g_ret)


def _fox_kernel(tq, tk, qt_ref, k_ref, vt_ref, g_ref, o_ref, m_ref, acc_ref, s0_ref, s1_ref):
    i = pl.program_id(2)
    qt = qt_ref[...]
    m_ref[...] = jnp.full_like(m_ref, NEG)
    acc_ref[...] = jnp.zeros_like(acc_ref)
    assert tq == 2 * tk
    s_refs = (s0_ref, s1_ref)

    def scores(j, slot):
        start = pl.multiple_of(j * tk, tk)
        s_refs[slot][...] = jnp.dot(k_ref[pl.ds(start, tk), :], qt, preferred_element_type=F32)

    def update(j, slot, diag):
        start = pl.multiple_of(j * tk, tk)
        st = s_refs[slot][...]
        if diag is not None:
            key = lax.broadcasted_iota(jnp.int32, st.shape, 0) + diag * tk
            qry = lax.broadcasted_iota(jnp.int32, st.shape, 1)
            st = jnp.where(key <= qry, st, NEG)
        m_prev = m_ref[...]
        m_new = jnp.maximum(m_prev, jnp.max(st, axis=0, keepdims=True))
        alpha = jnp.exp2(m_prev - m_new)
        p = jnp.exp2(st - m_new).astype(BF16)
        vt = vt_ref[:, pl.ds(start, tk)]
        acc_ref[...] = alpha * acc_ref[...] + jnp.dot(vt, p, preferred_element_type=F32)
        m_ref[...] = m_new

    scores(0, 0)

    def body(t, carry):
        scores(2 * t + 1, 1)
        update(2 * t, 0, None)
        scores(2 * t + 2, 0)
        update(2 * t + 1, 1, None)
        return carry

    lax.fori_loop(0, i, body, 0)
    scores(2 * i + 1, 1)
    update(2 * i, 0, 0)
    update(2 * i + 1, 1, 1)
    acc = acc_ref[...]
    row = lax.broadcasted_iota(jnp.int32, acc.shape, 0)
    o = jnp.where(row < FOX_DH, acc / acc[V_ONE:V_ONE + 1, :], 0.0)
    msq = jnp.sum(o * o, axis=0, keepdims=True) * (1.0 / FOX_DH)
    o_ref[...] = ((o * lax.rsqrt(msq + EPS)).T * g_ref[...]).astype(BF16)


def _fox(fqt, fka, fvt, g_fox, batch, seq):
    T = fka.shape[0]
    tq = FOX_Q_TILE
    tk = FOX_K_TILE
    nq = seq // tq
    return pl.pallas_call(
        functools.partial(_fox_kernel, tq, tk),
        grid=(batch, FOX_HEADS, nq),
        in_specs=[pl.BlockSpec((LANE, tq), lambda b, h, i: (h, b * nq + i)),
                  pl.BlockSpec((seq, LANE), lambda b, h, i: (b, h)),
                  pl.BlockSpec((LANE, seq), lambda b, h, i: (h, b)),
                  pl.BlockSpec((1, LANE), lambda b, h, i: (0, h))],
        out_specs=pl.BlockSpec((tq, LANE), lambda b, h, i: (b * nq + i, h)),
        out_shape=jax.ShapeDtypeStruct((T, FOX_HEADS * LANE), BF16),
        scratch_shapes=[pltpu.VMEM((1, tq), F32), pltpu.VMEM((LANE, tq), F32),
                        pltpu.VMEM((tk, tq), F32), pltpu.VMEM((tk, tq), F32)],
        compiler_params=pltpu.CompilerParams(dimension_semantics=("arbitrary", "arbitrary", "arbitrary"),
                                             vmem_limit_bytes=VMEM_LIMIT),
        name="fox",
    )(fqt, fka, fvt, g_fox)


def _mem_prep_kernel(mem_ref, g_ref, w_ref, kg_ref, bd_ref, shift_ref, mkt_ref, mvs_ref):
    x = mem_ref[0]
    ms = jnp.mean(x * x, axis=-1, keepdims=True)
    mn = (x * lax.rsqrt(ms + EPS) * g_ref[0]).astype(BF16)
    mkv = jnp.dot(mn, w_ref[0], preferred_element_type=F32)
    width = MEM_HEADS * MEM_DH
    mk = mkv[:, :width]
    mv = mkv[:, width:]
    mk = mk * lax.rsqrt(_group_mean_sq(mk, bd_ref[...]) + EPS) * kg_ref[0]
    mk = jnp.dot(mk.astype(BF16), shift_ref[...], preferred_element_type=F32)
    mkt = mk.T
    feat = (lax.broadcasted_iota(jnp.int32, mkt.shape, 0) - MQ_OFF) // MEM_DH
    lane_head = lax.broadcasted_iota(jnp.int32, mv.shape, 1) // MEM_DH
    M = x.shape[0]
    for hh in range(MEM_HEADS):
        mkt_ref[0, 0, :, hh * M:(hh + 1) * M] = jnp.where(feat == hh, mkt, 0.0).astype(BF16)
        mvs_ref[0, 0, hh * M:(hh + 1) * M, :] = jnp.where(lane_head == hh, mv, 0.0).astype(BF16)


def _mem_prep(mem, mem_norm_g, w_mem_kv, mem_k_g, bd, shift):
    B, M, _ = mem.shape
    depth = w_mem_kv.shape[0]
    width = MEM_HEADS * MEM_DH
    return pl.pallas_call(
        _mem_prep_kernel,
        grid=(depth, B),
        in_specs=[pl.BlockSpec((1, M, D_MODEL), lambda l, b: (b, 0, 0)),
                  pl.BlockSpec((1, 1, D_MODEL), lambda l, b: (l, 0, 0)),
                  pl.BlockSpec((1, D_MODEL, 2 * width), lambda l, b: (l, 0, 0)),
                  pl.BlockSpec((1, 1, width), lambda l, b: (l, 0, 0)),
                  pl.BlockSpec(bd.shape, lambda l, b: (0, 0)),
                  pl.BlockSpec(shift.shape, lambda l, b: (0, 0))],
        out_specs=[pl.BlockSpec((1, 1, MQ_SLAB, MEM_HEADS * M), lambda l, b: (l, b, 0, 0)),
                   pl.BlockSpec((1, 1, MEM_HEADS * M, width), lambda l, b: (l, b, 0, 0))],
        out_shape=[jax.ShapeDtypeStruct((depth, B, MQ_SLAB, MEM_HEADS * M), BF16),
                   jax.ShapeDtypeStruct((depth, B, MEM_HEADS * M, width), BF16)],
        compiler_params=pltpu.CompilerParams(dimension_semantics=("arbitrary", "arbitrary"),
                                             vmem_limit_bytes=VMEM_LIMIT),
        name="mem_prep",
    )(mem, mem_norm_g, w_mem_kv, mem_k_g, bd, shift)


def _out_proj_body(ret_ref, fox_ref, mq_ref, mkt_ref, mvs_ref, bd_ref, gm_ref, w_ref, x_ref, fg_ref):
    M = mkt_ref.shape[-1] // MEM_HEADS
    s = jnp.dot(mq_ref[...], mkt_ref[0, 0], preferred_element_type=F32)
    mem_o = None
    for hh in range(MEM_HEADS):
        sh = s[:, hh * M:(hh + 1) * M]
        p = jnp.exp(sh - jnp.max(sh, axis=-1, keepdims=True))
        p = (p / jnp.sum(p, axis=-1, keepdims=True)).astype(BF16)
        part = jnp.dot(p, mvs_ref[0, 0, hh * M:(hh + 1) * M, :], preferred_element_type=F32)
        mem_o = part if mem_o is None else mem_o + part
    mem_o = (mem_o * lax.rsqrt(_group_mean_sq(mem_o, bd_ref[...]) + EPS) * gm_ref[...]).astype(BF16)
    y = (jnp.dot(ret_ref[...], w_ref[0:512, :], preferred_element_type=F32)
         + jnp.dot(fox_ref[...], w_ref[512:1024, :], preferred_element_type=F32)
         + jnp.dot(mem_o, w_ref[1024:1280, :], preferred_element_type=F32))
    xm = x_ref[...] + y
    ms = jnp.mean(xm * xm, axis=-1, keepdims=True)
    return xm, xm * lax.rsqrt(ms + EPS) * fg_ref[...]


def _out_proj_dense_kernel(ret_ref, fox_ref, mq_ref, mkt_ref, mvs_ref, bd_ref, gm_ref, w_ref, x_ref, fg_ref,
                           wg_ref, wu_ref, wd_ref, o_ref):
    xm, hn = _out_proj_body(ret_ref, fox_ref, mq_ref, mkt_ref, mvs_ref, bd_ref, gm_ref, w_ref, x_ref, fg_ref)
    o_ref[...] = _swiglu(hn.astype(BF16), wg_ref.at[0], wu_ref.at[0], wd_ref.at[0], xm)


def _out_proj_moe_kernel(ret_ref, fox_ref, mq_ref, mkt_ref, mvs_ref, bd_ref, gm_ref, w_ref, x_ref, fg_ref,
                         rwt_ref, tri_ref, xm_ref, h_ref, routet_ref, route_ref, cnt_ref, carry_ref):
    @pl.when(pl.program_id(0) == 0)
    def _():
        carry_ref[...] = jnp.zeros_like(carry_ref)

    xm, hn = _out_proj_body(ret_ref, fox_ref, mq_ref, mkt_ref, mvs_ref, bd_ref, gm_ref, w_ref, x_ref, fg_ref)
    xm_ref[...] = xm
    _to_row_tiles(h_ref, hn)
    tm = hn.shape[0]
    h_hi = hn.astype(BF16)
    h_lo = (hn - h_hi.astype(F32)).astype(BF16)
    rwt = rwt_ref[...]
    nt = (((1,), (1,)), ((), ()))
    a = lax.dot_general(rwt, h_hi, nt, preferred_element_type=F32)
    b = lax.dot_general(rwt[0:16], h_lo, nt, preferred_element_type=F32)
    lg = a[0:N_EXPERTS] + a[16:16 + N_EXPERTS] + b[0:N_EXPERTS]
    row = lax.broadcasted_iota(jnp.int32, lg.shape, 0).astype(F32)
    m1 = jnp.max(lg, axis=0, keepdims=True)
    i1 = jnp.min(jnp.where(lg == m1, row, float(N_EXPERTS)), axis=0, keepdims=True)
    oh1 = row == i1
    lg2 = jnp.where(oh1, NEG, lg)
    m2 = jnp.max(lg2, axis=0, keepdims=True)
    i2 = jnp.min(jnp.where(lg2 == m2, row, float(N_EXPERTS)), axis=0, keepdims=True)
    oh2 = row == i2
    e = jnp.exp(m2 - m1)
    g1 = 1.0 / (1.0 + e)
    g2 = e / (1.0 + e)
    oh = jnp.where(oh1 | oh2, 1.0, 0.0)
    oh16 = jnp.concatenate([oh, jnp.zeros_like(oh)], axis=0).astype(BF16)
    before = carry_ref[:, 0:1] + jnp.dot(oh16, tri_ref[...], preferred_element_type=F32)[0:N_EXPERTS]
    r1 = jnp.sum(jnp.where(oh1, before, 0.0), axis=0, keepdims=True)
    r2 = jnp.sum(jnp.where(oh2, before, 0.0), axis=0, keepdims=True)
    total = before[:, tm - 1:tm] + oh[:, tm - 1:tm]
    carry_ref[...] = jnp.broadcast_to(total, carry_ref.shape)
    cnt_ref[...] = jnp.broadcast_to(total, cnt_ref.shape)
    rt = jnp.zeros_like(lg)
    for r, val in enumerate((i1, i2, r1, r2, g1, g2)):
        rt = jnp.where(row == r, val, rt)
    routet_ref[...] = rt
    route_ref[...] = jnp.concatenate([rt, jnp.zeros((LANE - N_EXPERTS, tm), F32)], axis=0).T


def _out_proj(ret_o, fox_o, mq, mkt, mvs, bd, g_mem, w_out, x, ffn_g, layer, seq, dense=None, router=None):
    T = x.shape[0]
    tm = TOK_TILE
    tps = seq // tm
    row = lambda i: (i, 0)
    fix = lambda i: (0, 0)
    full = lambda a: pl.BlockSpec(a.shape, fix)
    M4 = mkt.shape[-1]
    in_specs = [pl.BlockSpec((tm, 512), row), pl.BlockSpec((tm, 512), row), pl.BlockSpec((tm, MQ_SLAB), row),
                pl.BlockSpec((1, 1, MQ_SLAB, M4), lambda i: (layer, i // tps, 0, 0)),
                pl.BlockSpec((1, 1, M4, 256), lambda i: (layer, i // tps, 0, 0)),
                full(bd), full(g_mem), full(w_out), pl.BlockSpec((tm, D_MODEL), row), full(ffn_g)]
    args = [ret_o, fox_o, mq, mkt, mvs, bd, g_mem, w_out, x, ffn_g]
    params = pltpu.CompilerParams(dimension_semantics=("arbitrary",), vmem_limit_bytes=VMEM_LIMIT)
    if router is None:
        wg, wu, wd, j = dense
        once = pl.Buffered(1)
        w_specs = [pl.BlockSpec((1,) + w.shape[1:], lambda i: (j, 0, 0), pipeline_mode=once) for w in (wg, wu, wd)]
        return pl.pallas_call(
            _out_proj_dense_kernel, grid=(T // tm,), in_specs=in_specs + w_specs,
            out_specs=pl.BlockSpec((tm, D_MODEL), row),
            out_shape=jax.ShapeDtypeStruct((T, D_MODEL), F32),
            compiler_params=params, name="out_proj_dense",
        )(*args, wg, wu, wd)
    rw, tri = router
    return pl.pallas_call(
        _out_proj_moe_kernel, grid=(T // tm,), in_specs=in_specs + [full(rw), full(tri)],
        out_specs=[pl.BlockSpec((tm, D_MODEL), row), pl.BlockSpec((tm * ROW_TILE, LANE), row),
                   pl.BlockSpec((N_EXPERTS, tm), lambda i: (0, i)),
                   pl.BlockSpec((tm, LANE), row), pl.BlockSpec((8, LANE), fix)],
        out_shape=[jax.ShapeDtypeStruct((T, D_MODEL), F32), jax.ShapeDtypeStruct((T * ROW_TILE, LANE), F32),
                   jax.ShapeDtypeStruct((N_EXPERTS, T), F32),
                   jax.ShapeDtypeStruct((T, LANE), F32), jax.ShapeDtypeStruct((8, LANE), F32)],
        scratch_shapes=[pltpu.VMEM((8, LANE), F32)],
        compiler_params=params, name="out_proj_moe",
    )(*args, rw, tri)


def _swiglu(x, wg_ref, wu_ref, wd_ref, y):
    for lo, hi in FF_CHUNKS:
        cols = slice(lo, hi)
        g = jnp.dot(x, wg_ref[:, cols], preferred_element_type=F32)
        u = jnp.dot(x, wu_ref[:, cols], preferred_element_type=F32)
        a = (_silu(g) * u).astype(BF16)
        part = jnp.dot(a, wd_ref[cols, :], preferred_element_type=F32)
        y = part if y is None else y + part
    return y


def _to_row_tiles(ref, x):
    for s in range(ROW_TILE):
        ref[pl.ds(s, x.shape[0], stride=ROW_TILE), :] = x[:, s * LANE:(s + 1) * LANE]


def _from_row_tiles(ref, n):
    return jnp.concatenate([ref[pl.ds(s, n, stride=ROW_TILE), :] for s in range(ROW_TILE)], axis=1)


def _row_copy(src, s, dst, d, sem):
    return pltpu.make_async_copy(src.at[pl.ds(pl.multiple_of(s * ROW_TILE, ROW_TILE), ROW_TILE)],
                                 dst.at[pl.ds(pl.multiple_of(d * ROW_TILE, ROW_TILE), ROW_TILE)], sem)


def _moe_ffn_kernel(te_ref, nu_ref, src_ref, h_hbm, wg_ref, wu_ref, wd_ref, o_ref, x_ref, sem):
    i = pl.program_id(0)
    n_used = nu_ref[0]
    tm = x_ref.shape[1] // ROW_TILE
    slot = i % 2
    nxt = 1 - slot

    def tile_wait(s):
        pltpu.make_async_copy(h_hbm.at[pl.ds(0, tm * ROW_TILE)], x_ref.at[s], sem.at[s]).wait()

    @pl.when(i == 0)
    def _():
        def issue(r, carry):
            _row_copy(h_hbm, src_ref[r], x_ref.at[0], r, sem.at[0]).start()
            return carry

        lax.fori_loop(0, tm, issue, 0, unroll=8)

    @pl.when(i < n_used)
    def _():
        tile_wait(slot)
        x = _from_row_tiles(x_ref.at[slot], tm).astype(BF16)
        base = (i + 1) * tm
        for r in range(tm):
            _row_copy(h_hbm, src_ref[base + r], x_ref.at[nxt], r, sem.at[nxt]).start(priority=r % 2)
        _to_row_tiles(o_ref, _swiglu(x, wg_ref.at[0, 0], wu_ref.at[0, 0], wd_ref.at[0, 0], None))

        @pl.when(i + 1 == n_used)
        def _():
            tile_wait(nxt)

    @pl.when(i >= n_used)
    def _():
        o_ref[...] = jnp.zeros_like(o_ref)


def _moe_ffn(h, src, tile_expert, n_used, wg, wu, wd, layer):
    tm = MOE_TILE
    n_tiles = tile_expert.shape[0]
    assert src.shape[0] == (n_tiles + 1) * tm
    w_block = lambda i, te, nu, sr: (layer, te[i], 0, 0)
    return pl.pallas_call(
        _moe_ffn_kernel,
        grid_spec=pltpu.PrefetchScalarGridSpec(
            num_scalar_prefetch=3, grid=(n_tiles,),
            in_specs=[pl.BlockSpec(memory_space=pl.ANY),
                      pl.BlockSpec((1, 1, D_MODEL, D_FF), w_block),
                      pl.BlockSpec((1, 1, D_MODEL, D_FF), w_block),
                      pl.BlockSpec((1, 1, D_FF, D_MODEL), w_block)],
            out_specs=pl.BlockSpec((tm * ROW_TILE, LANE), lambda i, te, nu, sr: (i, 0)),
            scratch_shapes=[pltpu.VMEM((2, tm * ROW_TILE, LANE), F32), pltpu.SemaphoreType.DMA((2,))]),
        out_shape=jax.ShapeDtypeStruct((n_tiles * tm * ROW_TILE, LANE), F32),
        compiler_params=pltpu.CompilerParams(dimension_semantics=("arbitrary",),
                                             vmem_limit_bytes=VMEM_LIMIT),
        name="moe_ffn",
    )(tile_expert, n_used, src, h, wg, wu, wd)


def _combine_kernel(tc, d1_ref, d2_ref, o_hbm, x_ref, route_ref, out_ref, b1_ref, b2_ref, sem):
    base = pl.program_id(0) * tc

    def issue(r, carry):
        t = base + r
        _row_copy(o_hbm, d1_ref[t], b1_ref, r, sem.at[0]).start(priority=0)
        _row_copy(o_hbm, d2_ref[t], b2_ref, r, sem.at[1]).start(priority=1)
        return carry

    lax.fori_loop(0, tc, issue, 0, unroll=8)
    pltpu.make_async_copy(o_hbm.at[pl.ds(0, tc * ROW_TILE)], b1_ref, sem.at[0]).wait()
    pltpu.make_async_copy(o_hbm.at[pl.ds(0, tc * ROW_TILE)], b2_ref, sem.at[1]).wait()
    route = route_ref[...]
    g1 = route[:, 4:5]
    g2 = route[:, 5:6]
    out_ref[...] = x_ref[...] + g1 * _from_row_tiles(b1_ref, tc) + g2 * _from_row_tiles(b2_ref, tc)


def _combine(o, d1, d2, x, route):
    T = x.shape[0]
    tc = 256
    return pl.pallas_call(
        functools.partial(_combine_kernel, tc),
        grid_spec=pltpu.PrefetchScalarGridSpec(
            num_scalar_prefetch=2, grid=(T // tc,),
            in_specs=[pl.BlockSpec(memory_space=pl.ANY),
                      pl.BlockSpec((tc, D_MODEL), lambda i, a, b: (i, 0)),
                      pl.BlockSpec((tc, LANE), lambda i, a, b: (i, 0))],
            out_specs=pl.BlockSpec((tc, D_MODEL), lambda i, a, b: (i, 0)),
            scratch_shapes=[pltpu.VMEM((tc * ROW_TILE, LANE), F32), pltpu.VMEM((tc * ROW_TILE, LANE), F32),
                            pltpu.SemaphoreType.DMA((2,))]),
        out_shape=jax.ShapeDtypeStruct((T, D_MODEL), F32),
        compiler_params=pltpu.CompilerParams(dimension_semantics=("arbitrary",),
                                             vmem_limit_bytes=VMEM_LIMIT),
        name="moe_combine",
    )(d1, d2, o, x, route)


def _plan_kernel(rows, d1_ref, d2_ref, src_ref):
    def clear(r, carry):
        src_ref[r] = 0
        return carry

    def place(t, carry):
        src_ref[d1_ref[t]] = t
        src_ref[d2_ref[t]] = t
        return carry

    lax.fori_loop(0, rows, clear, 0, unroll=8)
    lax.fori_loop(0, d1_ref.shape[0], place, 0, unroll=8)


def _plan(d1, d2, rows):
    return pl.pallas_call(
        functools.partial(_plan_kernel, rows),
        grid_spec=pltpu.PrefetchScalarGridSpec(
            num_scalar_prefetch=2, grid=(1,), in_specs=[],
            out_specs=pl.BlockSpec(memory_space=pltpu.SMEM)),
        out_shape=jax.ShapeDtypeStruct((rows,), jnp.int32),
        name="moe_plan",
    )(d1, d2)


def _moe_layer(h, xm, routet, route, counts, wg, wu, wd, layer):
    T = xm.shape[0]
    tm = MOE_TILE
    n_tiles = (2 * T) // tm + N_EXPERTS
    e1 = routet[0].astype(jnp.int32)
    e2 = routet[1].astype(jnp.int32)
    r1 = routet[2].astype(jnp.int32)
    r2 = routet[3].astype(jnp.int32)
    cnt = counts[:, 0].astype(jnp.int32)
    tiles = (cnt + tm - 1) // tm
    tile_end = jnp.cumsum(tiles)
    offs = (tile_end - tiles) * tm
    d1 = offs[e1] + r1
    d2 = offs[e2] + r2
    n_used = tile_end[-1:]
    tile_id = jnp.arange(n_tiles, dtype=jnp.int32)
    tile_expert = jnp.sum(tile_id[:, None] >= tile_end[None, :], axis=1).astype(jnp.int32)
    tile_expert = jnp.minimum(tile_expert, tile_expert[jnp.maximum(n_used[0] - 1, 0)])
    src = _plan(d1, d2, (n_tiles + 1) * tm)
    o = _moe_ffn(h, src, tile_expert, n_used.astype(jnp.int32), wg, wu, wd, layer)
    return _combine(o, d1, d2, xm, route)


def _constants(seq):
    tm = TOK_TILE
    r = np.arange(tm)
    tri_upper = (r[:, None] <= r[None, :]).astype(np.float32)
    tri_strict = (r[:, None] < r[None, :]).astype(np.float32)
    d = np.arange(256)
    bd = (d[:, None] // 64 == d[None, :] // 64).astype(np.float32) / 64.0
    m = np.arange(MQ_SLAB) - MQ_OFF
    valid = (m >= 0) & (m < MEM_HEADS * MEM_DH)
    bdm = ((m[:, None] // MEM_DH == m[None, :] // MEM_DH) & valid[:, None] & valid[None, :]).astype(np.float32) / MEM_DH
    shift = (d[:, None] + MQ_OFF == np.arange(MQ_SLAB)[None, :]).astype(np.float32)
    in_consts = (jnp.asarray(tri_upper, BF16), jnp.asarray(bd, BF16), jnp.asarray(bdm, BF16))

    half = RET_DK // 2
    pos = jnp.arange(seq, dtype=F32)
    inv_freq = ROPE_BASE ** (-jnp.arange(half, dtype=F32) / half)
    ang = pos[:, None] * inv_freq[None, :]
    cos = jnp.tile(jnp.cos(ang), (1, LANE // half))
    sin = jnp.tile(jnp.concatenate([-jnp.sin(ang), jnp.sin(ang)], axis=1), (1, LANE // RET_DK))

    C = RET_CHUNK
    hh = jnp.arange(RET_HEADS, dtype=F32)
    log_g = jnp.log(1.0 - 2.0 ** (-5.0 - hh))
    idx = jnp.arange(C, dtype=F32)
    diff = idx[:, None] - idx[None, :]
    dmat = jnp.where(diff >= 0, jnp.exp(log_g[:, None, None] * jnp.maximum(diff, 0.0)), 0.0)
    dstack = dmat.reshape(RET_HEADS * C, C)
    k_head = np.arange(RET_HEADS * RET_DK) // RET_DK
    v_head = np.arange(RET_HEADS * RET_DV) // RET_DV
    qd = jnp.exp(log_g[k_head][None, :] * (idx[:, None] + 1.0))
    kd = jnp.exp(log_g[k_head][None, :] * (C - 1.0 - idx[:, None]))
    cd = jnp.exp(log_g[v_head] * C)[None, :]
    bm = jnp.asarray((k_head[:, None] == v_head[None, :]).astype(np.float32))
    hm = jnp.asarray((np.arange(RET_HEADS)[:, None] == k_head[None, :]).astype(np.float32), BF16)
    ret_tables = (hm, dstack, qd, kd, cd, bm)
    return in_consts, cos, sin, ret_tables, jnp.asarray(tri_strict, BF16), jnp.asarray(shift, BF16)


def _pad_heads(g):
    depth = g.shape[0]
    g = g.reshape(depth, FOX_HEADS, FOX_DH)
    return jnp.pad(g, ((0, 0), (0, 0), (0, LANE - FOX_DH))).reshape(depth, 1, FOX_HEADS * LANE)


def kernel(x, mem, attn_norm_g, w_in, forget_bias, fox_q_norm_g, fox_k_norm_g, mem_q_norm_g, mem_k_norm_g,
           mem_norm_g, w_mem_kv, out_norm_g, w_out, ffn_norm_g, dense_w_gate, dense_w_up, dense_w_down,
           router_w, expert_w_gate, expert_w_up, expert_w_down):
    B, S, D = x.shape
    depth = w_in.shape[0]
    T = B * S
    in_consts, cos, sin, ret_tables, tri_strict, shift = _constants(S)
    bd = in_consts[1]

    w_in_p = jnp.pad(w_in, ((0, 0), (0, 0), (0, P_IN - w_in.shape[-1]))).astype(BF16)
    fqg = jnp.tile(fox_q_norm_g, (1, FOX_HEADS))[:, None, :]
    fkg = jnp.tile(fox_k_norm_g, (1, FOX_HEADS))[:, None, :]
    fb = jnp.pad(forget_bias, ((0, 0), (0, 8 - FOX_HEADS)))[:, :, None]
    mqg = jnp.tile(mem_q_norm_g, (1, MEM_HEADS))
    mqg = jnp.pad(mqg, ((0, 0), (MQ_OFF, MQ_SLAB - MQ_OFF - mqg.shape[-1])))[:, None, :]
    mkg = jnp.tile(mem_k_norm_g, (1, MEM_HEADS))[:, None, :]
    n_ret = RET_HEADS * RET_DV
    n_fox = FOX_HEADS * FOX_DH
    g_ret = out_norm_g[:, None, :n_ret]
    g_fox = _pad_heads(out_norm_g[:, n_ret:n_ret + n_fox])
    g_mem = out_norm_g[:, None, n_ret + n_fox:]
    w_fox = w_out[:, n_ret:n_ret + n_fox].reshape(depth, FOX_HEADS, FOX_DH, D)
    w_fox = jnp.pad(w_fox, ((0, 0), (0, 0), (0, LANE - FOX_DH), (0, 0))).reshape(depth, FOX_HEADS * LANE, D)
    w_out_p = jnp.concatenate([w_out[:, :n_ret], w_fox, w_out[:, n_ret + n_fox:]], axis=1).astype(BF16)
    dwg, dwu, dwd = dense_w_gate.astype(BF16), dense_w_up.astype(BF16), dense_w_down.astype(BF16)
    ewg, ewu, ewd = expert_w_gate.astype(BF16), expert_w_up.astype(BF16), expert_w_down.astype(BF16)
    rwt = jnp.swapaxes(router_w, 1, 2)
    rwt_hi = rwt.astype(BF16)
    rwt_lo = (rwt - rwt_hi.astype(F32)).astype(BF16)
    pad16 = lambda w: jnp.pad(w, ((0, 0), (0, 16 - N_EXPERTS), (0, 0)))
    rw = jnp.concatenate([pad16(rwt_hi), pad16(rwt_lo)], axis=1)

    mkt, mvs = _mem_prep(mem, mem_norm_g[:, None, :], w_mem_kv.astype(BF16), mkg, bd, shift)

    xt = x.reshape(T, D)
    for l in range(depth):
        rq, rk, rv, rg, fqt, fka, fvt, mq = _in_proj(
            xt, attn_norm_g[l][None], w_in_p[l], cos, sin, fqg[l], fkg[l], mqg[l], fb[l], in_consts, S)
        ret_o = _retention(rq, rk, rv, rg, ret_tables, g_ret[l], B, S)
        fox_o = _fox(fqt, fka, fvt, g_fox[l], B, S)
        j = l // 2
        if l % 2 == 0:
            xt = _out_proj(ret_o, fox_o, mq, mkt, mvs, bd, g_mem[l], w_out_p[l], xt, ffn_norm_g[l][None], l, S,
                           dense=(dwg, dwu, dwd, j))
        else:
            xm, h2, routet, route, counts = _out_proj(ret_o, fox_o, mq, mkt, mvs, bd, g_mem[l], w_out_p[l], xt,
                                                      ffn_norm_g[l][None], l, S, router=(rw[j], tri_strict))
            xt = _moe_layer(h2, xm, routet, route, counts, ewg, ewu, ewd, j)
    return xt.reshape(B, S, D)
```

```python
import functools

import numpy as np
import jax
import jax.numpy as jnp
from jax import lax
from jax.experimental import pallas as pl
from jax.experimental.pallas import tpu as pltpu

F32 = jnp.float32
BF16 = jnp.bfloat16

D_MODEL = 1024
RET_HEADS = 4
RET_DK = 64
RET_DV = 128
FOX_HEADS = 4
FOX_DH = 64
MEM_HEADS = 4
MEM_DH = 64
RET_CHUNK = 128
D_FF = 2752
N_EXPERTS = 8
ROPE_BASE = 10000.0
EPS = 1e-6

LANE = 128
FF_CHUNKS = ((0, 1408), (1408, 2688), (2688, D_FF))
TOK_TILE = 512
FOX_Q_TILE = 1024
FOX_K_TILE = 512
LOG2E = 1.4426950408889634
ROW_TILE = 8
assert D_MODEL == ROW_TILE * LANE
MOE_TILE = 512
VMEM_LIMIT = 56 * 1024 * 1024
NEG = -1e30

RQ, RK, RV, RG = 0, 256, 512, 1024
FQ, FK, FV = 1536, 1792, 2048
FF_COL = 2304
MQ_OFF = FOX_HEADS
MQ_SLAB = 384
P_IN = FF_COL + MQ_SLAB
N_PIECE = 3
V_ONE = 64


def _split3(x):
    hi = x.astype(BF16)
    r1 = x - hi.astype(F32)
    mid = r1.astype(BF16)
    lo = (r1 - mid.astype(F32)).astype(BF16)
    return hi, mid, lo


def _group_mean_sq(x, bd):
    sq = x * x
    hi = sq.astype(BF16)
    lo = (sq - hi.astype(F32)).astype(BF16)
    return (jnp.dot(hi, bd, preferred_element_type=F32)
            + jnp.dot(lo, bd, preferred_element_type=F32))


def _silu(x):
    return x / (1.0 + jnp.exp(-x))


def _in_proj_kernel(tiles_per_seq, x_ref, g_ref, w_ref, cos_ref, sin_ref, fqg_ref, fkg_ref, mqg_ref,
                    fb_ref, tri_ref, bd_ref, bdm_ref,
                    rq_ref, rk_ref, rv_ref, rg_ref, fqt_ref, fka_ref, fvt_ref, mq_ref, carry_ref):
    i = pl.program_id(0)

    @pl.when(i % tiles_per_seq == 0)
    def _():
        carry_ref[...] = jnp.zeros_like(carry_ref)

    x = x_ref[...]
    ms = jnp.mean(x * x, axis=-1, keepdims=True)
    h = (x * lax.rsqrt(ms + EPS) * g_ref[...]).astype(BF16)

    def proj(a, b):
        return jnp.dot(h, w_ref[:, a:b], preferred_element_type=F32)

    tm = x.shape[0]
    cos = cos_ref[...]
    sin = sin_ref[...]
    first = (lax.broadcasted_iota(jnp.int32, (tm, LANE), 1) % RET_DK) < RET_DK // 2
    for base, ref, scale in ((RQ, rq_ref, RET_DK ** -0.5), (RK, rk_ref, 1.0)):
        for a in (0, LANE):
            t = proj(base + a, base + a + LANE)
            partner = jnp.where(first, pltpu.roll(t, LANE - RET_DK // 2, 1), pltpu.roll(t, RET_DK // 2, 1))
            ref[:, a:a + LANE] = ((t * cos + partner * sin) * scale).astype(BF16)
    rv_ref[...] = proj(RV, RV + 512).astype(BF16)
    rg_ref[...] = proj(RG, RG + 512).astype(BF16)

    zm = proj(FF_COL, FF_COL + MQ_SLAB)
    mq = zm * lax.rsqrt(_group_mean_sq(zm, bdm_ref[...]) + EPS) * mqg_ref[...] * (MEM_DH ** -0.5)
    mq_ref[...] = mq.astype(BF16)

    z = zm[:, 0:LANE].T[0:8, :] + fb_ref[...]
    logf = jnp.minimum(z, 0.0) - jnp.log(1.0 + jnp.exp(-jnp.abs(z)))
    hrow = lax.broadcasted_iota(jnp.int32, logf.shape, 0)
    logf = jnp.where(hrow < FOX_HEADS, logf, 0.0)
    tri = tri_ref[...]
    c = carry_ref[:, 0:1]
    for piece in _split3(logf):
        c = c + jnp.dot(piece, tri, preferred_element_type=F32)
    carry_ref[...] = jnp.broadcast_to(c[:, tm - 1:tm], carry_ref.shape)

    pieces = [p.astype(F32) for p in _split3(c * LOG2E)]
    bd = bd_ref[...]
    fq = proj(FQ, FQ + 256)
    fk = proj(FK, FK + 256)
    qt = (fq * lax.rsqrt(_group_mean_sq(fq, bd) + EPS) * fqg_ref[...] * (FOX_DH ** -0.5 * LOG2E)).T
    kt = (fk * lax.rsqrt(_group_mean_sq(fk, bd) + EPS) * fkg_ref[...]).T
    vt = proj(FV, FV + 256).T
    erow = lax.broadcasted_iota(jnp.int32, (8, tm), 0)
    zeros = jnp.zeros((LANE - FOX_DH - 8, tm), F32)
    ones_row = jnp.where(erow == V_ONE - FOX_DH, 1.0, 0.0)
    for hh in range(FOX_HEADS):
        feat = slice(hh * FOX_DH, (hh + 1) * FOX_DH)
        rows = slice(hh * LANE, (hh + 1) * LANE)
        eq = jnp.where(erow < 2 * N_PIECE, 1.0, 0.0)
        ek = eq
        for p in range(N_PIECE):
            c_row = pieces[p][hh:hh + 1, :]
            eq = jnp.where(erow == p, c_row, eq)
            ek = jnp.where(erow == N_PIECE + p, -c_row, ek)
        fqt_ref[rows, :] = jnp.concatenate([qt[feat], eq, zeros], axis=0).astype(BF16)
        fka_ref[:, rows] = jnp.concatenate([kt[feat], ek, zeros], axis=0).T.astype(BF16)
        fvt_ref[rows, :] = jnp.concatenate([vt[feat], ones_row, zeros], axis=0).astype(BF16)


def _in_proj(x, g, w, cos, sin, fqg, fkg, mqg, fb, consts, seq):
    T = x.shape[0]
    tm = TOK_TILE
    nt = T // tm
    tps = seq // tm
    row = lambda i: (i, 0)
    fix = lambda i: (0, 0)
    full = lambda a: pl.BlockSpec(a.shape, fix)
    tri, bd, bdm = consts
    widths = (256, 256, 512, 512, -512, 512, -512, MQ_SLAB)
    outs = [jax.ShapeDtypeStruct((T, n) if n > 0 else (-n, T), BF16) for n in widths]
    out_specs = [pl.BlockSpec((tm, n), row) if n > 0 else pl.BlockSpec((-n, tm), lambda i: (0, i))
                 for n in widths]
    return pl.pallas_call(
        functools.partial(_in_proj_kernel, tps),
        grid=(nt,),
        in_specs=[pl.BlockSpec((tm, D_MODEL), row), full(g), full(w),
                  pl.BlockSpec((tm, LANE), lambda i: (i % tps, 0)),
                  pl.BlockSpec((tm, LANE), lambda i: (i % tps, 0)),
                  full(fqg), full(fkg), full(mqg), full(fb), full(tri), full(bd), full(bdm)],
        out_specs=out_specs,
        out_shape=outs,
        scratch_shapes=[pltpu.VMEM((8, LANE), F32)],
        compiler_params=pltpu.CompilerParams(dimension_semantics=("arbitrary",),
                                             vmem_limit_bytes=VMEM_LIMIT),
        name="in_proj",
    )(x, g, w, cos, sin, fqg, fkg, mqg, fb, tri, bd, bdm)


def _retention_kernel(chunks, rq_ref, rk_ref, rv_ref, rg_ref, hm_ref, dst_ref, qd_ref, kd_ref, cd_ref,
                      bm_ref, g_ref, o_ref, state_ref):
    @pl.when(pl.program_id(1) == 0)
    def _():
        state_ref[...] = jnp.zeros_like(state_ref)

    C = RET_CHUNK
    for ci in range(chunks):
        rows = slice(ci * C, (ci + 1) * C)
        qc = rq_ref[rows, :]
        kc = rk_ref[rows, :]
        vc = rv_ref[rows, :]
        qs = jnp.concatenate([qc * hm_ref[hh:hh + 1, :] for hh in range(RET_HEADS)], axis=0)
        sc = lax.dot_general(qs, kc, (((1,), (1,)), ((), ())), preferred_element_type=F32)
        sc = (sc * dst_ref[...]).astype(BF16)
        state = state_ref[...]
        qdec = (qc.astype(F32) * qd_ref[...]).astype(BF16)
        cross = jnp.dot(qdec, state.astype(BF16), preferred_element_type=F32)
        kdec = (kc.astype(F32) * kd_ref[...]).astype(BF16)
        kv = lax.dot_general(kdec, vc, (((0,), (0,)), ((), ())), preferred_element_type=F32)
        state_ref[...] = state * cd_ref[...] + kv * bm_ref[...]
        for hh in range(RET_HEADS):
            cols = slice(hh * RET_DV, (hh + 1) * RET_DV)
            o = jnp.dot(sc[hh * C:(hh + 1) * C, :], vc[:, cols], preferred_element_type=F32) + cross[:, cols]
            msq = jnp.mean(o * o, axis=-1, keepdims=True)
            y = o * lax.rsqrt(msq + EPS) * g_ref[:, cols]
            o_ref[rows, cols] = (y * _silu(rg_ref[rows, cols].astype(F32))).astype(BF16)


def _retention(rq, rk, rv, rg, tables, g_ret, batch, seq):
    T = rq.shape[0]
    tm = TOK_TILE
    tps = seq // tm
    row = lambda b, i: (b * tps + i, 0)
    fix = lambda b, i: (0, 0)
    full = lambda a: pl.BlockSpec(a.shape, fix)
    return pl.pallas_call(
        functools.partial(_retention_kernel, tm // RET_CHUNK),
        grid=(batch, tps),
        in_specs=[pl.BlockSpec((tm, 256), row), pl.BlockSpec((tm, 256), row),
                  pl.BlockSpec((tm, 512), row), pl.BlockSpec((tm, 512), row)]
                 + [full(t) for t in tables] + [full(g_ret)],
        out_specs=pl.BlockSpec((tm, 512), row),
        out_shape=jax.ShapeDtypeStruct((T, 512), BF16),
        scratch_shapes=[pltpu.VMEM((RET_HEADS * RET_DK, RET_HEADS * RET_DV), F32)],
        compiler_params=pltpu.CompilerParams(dimension_semantics=("arbitrary", "arbitrary"),
                                             vmem_limit_bytes=VMEM_LIMIT),
        name="retention",
    )(rq, rk, rv, rg, *tables, g_ret)


def _fox_kernel(tq, tk, qt_ref, k_ref, vt_ref, g_ref, o_ref, m_ref, acc_ref, s0_ref, s1_ref):
    i = pl.program_id(2)
    qt = qt_ref[...]
    m_ref[...] = jnp.full_like(m_ref, NEG)
    acc_ref[...] = jnp.zeros_like(acc_ref)
    assert tq == 2 * tk
    s_refs = (s0_ref, s1_ref)

    def scores(j, slot):
        start = pl.multiple_of(j * tk, tk)
        s_refs[slot][...] = jnp.dot(k_ref[pl.ds(start, tk), :], qt, preferred_element_type=F32)

    def update(j, slot, diag):
        start = pl.multiple_of(j * tk, tk)
        st = s_refs[slot][...]
        if diag is not None:
            key = lax.broadcasted_iota(jnp.int32, st.shape, 0) + diag * tk
            qry = lax.broadcasted_iota(jnp.int32, st.shape, 1)
            st = jnp.where(key <= qry, st, NEG)
        m_prev = m_ref[...]
        m_new = jnp.maximum(m_prev, jnp.max(st, axis=0, keepdims=True))
        alpha = jnp.exp2(m_prev - m_new)
        p = jnp.exp2(st - m_new).astype(BF16)
        vt = vt_ref[:, pl.ds(start, tk)]
        acc_ref[...] = alpha * acc_ref[...] + jnp.dot(vt, p, preferred_element_type=F32)
        m_ref[...] = m_new

    scores(0, 0)

    def body(t, carry):
        scores(2 * t + 1, 1)
        update(2 * t, 0, None)
        scores(2 * t + 2, 0)
        update(2 * t + 1, 1, None)
        return carry

    lax.fori_loop(0, i, body, 0)
    scores(2 * i + 1, 1)
    update(2 * i, 0, 0)
    update(2 * i + 1, 1, 1)
    acc = acc_ref[...]
    row = lax.broadcasted_iota(jnp.int32, acc.shape, 0)
    o = jnp.where(row < FOX_DH, acc / acc[V_ONE:V_ONE + 1, :], 0.0)
    msq = jnp.sum(o * o, axis=0, keepdims=True) * (1.0 / FOX_DH)
    o_ref[...] = ((o * lax.rsqrt(msq + EPS)).T * g_ref[...]).astype(BF16)


def _fox(fqt, fka, fvt, g_fox, batch, seq):
    T = fka.shape[0]
    tq = FOX_Q_TILE
    tk = FOX_K_TILE
    nq = seq // tq
    return pl.pallas_call(
        functools.partial(_fox_kernel, tq, tk),
        grid=(batch, FOX_HEADS, nq),
        in_specs=[pl.BlockSpec((LANE, tq), lambda b, h, i: (h, b * nq + i)),
                  pl.BlockSpec((seq, LANE), lambda b, h, i: (b, h)),
                  pl.BlockSpec((LANE, seq), lambda b, h, i: (h, b)),
                  pl.BlockSpec((1, LANE), lambda b, h, i: (0, h))],
        out_specs=pl.BlockSpec((tq, LANE), lambda b, h, i: (b * nq + i, h)),
        out_shape=jax.ShapeDtypeStruct((T, FOX_HEADS * LANE), BF16),
        scratch_shapes=[pltpu.VMEM((1, tq), F32), pltpu.VMEM((LANE, tq), F32),
                        pltpu.VMEM((tk, tq), F32), pltpu.VMEM((tk, tq), F32)],
        compiler_params=pltpu.CompilerParams(dimension_semantics=("arbitrary", "arbitrary", "arbitrary"),
                                             vmem_limit_bytes=VMEM_LIMIT),
        name="fox",
    )(fqt, fka, fvt, g_fox)


def _mem_prep_kernel(mem_ref, g_ref, w_ref, kg_ref, bd_ref, shift_ref, mkt_ref, mvs_ref):
    x = mem_ref[0]
    ms = jnp.mean(x * x, axis=-1, keepdims=True)
    mn = (x * lax.rsqrt(ms + EPS) * g_ref[0]).astype(BF16)
    mkv = jnp.dot(mn, w_ref[0], preferred_element_type=F32)
    width = MEM_HEADS * MEM_DH
    mk = mkv[:, :width]
    mv = mkv[:, width:]
    mk = mk * lax.rsqrt(_group_mean_sq(mk, bd_ref[...]) + EPS) * kg_ref[0]
    mk = jnp.dot(mk.astype(BF16), shift_ref[...], preferred_element_type=F32)
    mkt = mk.T
    feat = (lax.broadcasted_iota(jnp.int32, mkt.shape, 0) - MQ_OFF) // MEM_DH
    lane_head = lax.broadcasted_iota(jnp.int32, mv.shape, 1) // MEM_DH
    M = x.shape[0]
    for hh in range(MEM_HEADS):
        mkt_ref[0, 0, :, hh * M:(hh + 1) * M] = jnp.where(feat == hh, mkt, 0.0).astype(BF16)
        mvs_ref[0, 0, hh * M:(hh + 1) * M, :] = jnp.where(lane_head == hh, mv, 0.0).astype(BF16)


def _mem_prep(mem, mem_norm_g, w_mem_kv, mem_k_g, bd, shift):
    B, M, _ = mem.shape
    depth = w_mem_kv.shape[0]
    width = MEM_HEADS * MEM_DH
    return pl.pallas_call(
        _mem_prep_kernel,
        grid=(depth, B),
        in_specs=[pl.BlockSpec((1, M, D_MODEL), lambda l, b: (b, 0, 0)),
                  pl.BlockSpec((1, 1, D_MODEL), lambda l, b: (l, 0, 0)),
                  pl.BlockSpec((1, D_MODEL, 2 * width), lambda l, b: (l, 0, 0)),
                  pl.BlockSpec((1, 1, width), lambda l, b: (l, 0, 0)),
                  pl.BlockSpec(bd.shape, lambda l, b: (0, 0)),
                  pl.BlockSpec(shift.shape, lambda l, b: (0, 0))],
        out_specs=[pl.BlockSpec((1, 1, MQ_SLAB, MEM_HEADS * M), lambda l, b: (l, b, 0, 0)),
                   pl.BlockSpec((1, 1, MEM_HEADS * M, width), lambda l, b: (l, b, 0, 0))],
        out_shape=[jax.ShapeDtypeStruct((depth, B, MQ_SLAB, MEM_HEADS * M), BF16),
                   jax.ShapeDtypeStruct((depth, B, MEM_HEADS * M, width), BF16)],
        compiler_params=pltpu.CompilerParams(dimension_semantics=("arbitrary", "arbitrary"),
                                             vmem_limit_bytes=VMEM_LIMIT),
        name="mem_prep",
    )(mem, mem_norm_g, w_mem_kv, mem_k_g, bd, shift)


def _out_proj_body(ret_ref, fox_ref, mq_ref, mkt_ref, mvs_ref, bd_ref, gm_ref, w_ref, x_ref, fg_ref):
    M = mkt_ref.shape[-1] // MEM_HEADS
    s = jnp.dot(mq_ref[...], mkt_ref[0, 0], preferred_element_type=F32)
    mem_o = None
    for hh in range(MEM_HEADS):
        sh = s[:, hh * M:(hh + 1) * M]
        p = jnp.exp(sh - jnp.max(sh, axis=-1, keepdims=True))
        p = (p / jnp.sum(p, axis=-1, keepdims=True)).astype(BF16)
        part = jnp.dot(p, mvs_ref[0, 0, hh * M:(hh + 1) * M, :], preferred_element_type=F32)
        mem_o = part if mem_o is None else mem_o + part
    mem_o = (mem_o * lax.rsqrt(_group_mean_sq(mem_o, bd_ref[...]) + EPS) * gm_ref[...]).astype(BF16)
    y = (jnp.dot(ret_ref[...], w_ref[0:512, :], preferred_element_type=F32)
         + jnp.dot(fox_ref[...], w_ref[512:1024, :], preferred_element_type=F32)
         + jnp.dot(mem_o, w_ref[1024:1280, :], preferred_element_type=F32))
    xm = x_ref[...] + y
    ms = jnp.mean(xm * xm, axis=-1, keepdims=True)
    return xm, xm * lax.rsqrt(ms + EPS) * fg_ref[...]


def _out_proj_dense_kernel(ret_ref, fox_ref, mq_ref, mkt_ref, mvs_ref, bd_ref, gm_ref, w_ref, x_ref, fg_ref,
                           wg_ref, wu_ref, wd_ref, o_ref):
    xm, hn = _out_proj_body(ret_ref, fox_ref, mq_ref, mkt_ref, mvs_ref, bd_ref, gm_ref, w_ref, x_ref, fg_ref)
    o_ref[...] = _swiglu(hn.astype(BF16), wg_ref.at[0], wu_ref.at[0], wd_ref.at[0], xm)


def _out_proj_moe_kernel(ret_ref, fox_ref, mq_ref, mkt_ref, mvs_ref, bd_ref, gm_ref, w_ref, x_ref, fg_ref,
                         rwt_ref, tri_ref, xm_ref, h_ref, routet_ref, route_ref, cnt_ref, carry_ref):
    @pl.when(pl.program_id(0) == 0)
    def _():
        carry_ref[...] = jnp.zeros_like(carry_ref)

    xm, hn = _out_proj_body(ret_ref, fox_ref, mq_ref, mkt_ref, mvs_ref, bd_ref, gm_ref, w_ref, x_ref, fg_ref)
    xm_ref[...] = xm
    _to_row_tiles(h_ref, hn)
    tm = hn.shape[0]
    h_hi = hn.astype(BF16)
    h_lo = (hn - h_hi.astype(F32)).astype(BF16)
    rwt = rwt_ref[...]
    nt = (((1,), (1,)), ((), ()))
    a = lax.dot_general(rwt, h_hi, nt, preferred_element_type=F32)
    b = lax.dot_general(rwt[0:16], h_lo, nt, preferred_element_type=F32)
    lg = a[0:N_EXPERTS] + a[16:16 + N_EXPERTS] + b[0:N_EXPERTS]
    row = lax.broadcasted_iota(jnp.int32, lg.shape, 0).astype(F32)
    m1 = jnp.max(lg, axis=0, keepdims=True)
    i1 = jnp.min(jnp.where(lg == m1, row, float(N_EXPERTS)), axis=0, keepdims=True)
    oh1 = row == i1
    lg2 = jnp.where(oh1, NEG, lg)
    m2 = jnp.max(lg2, axis=0, keepdims=True)
    i2 = jnp.min(jnp.where(lg2 == m2, row, float(N_EXPERTS)), axis=0, keepdims=True)
    oh2 = row == i2
    e = jnp.exp(m2 - m1)
    g1 = 1.0 / (1.0 + e)
    g2 = e / (1.0 + e)
    oh = jnp.where(oh1 | oh2, 1.0, 0.0)
    oh16 = jnp.concatenate([oh, jnp.zeros_like(oh)], axis=0).astype(BF16)
    before = carry_ref[:, 0:1] + jnp.dot(oh16, tri_ref[...], preferred_element_type=F32)[0:N_EXPERTS]
    r1 = jnp.sum(jnp.where(oh1, before, 0.0), axis=0, keepdims=True)
    r2 = jnp.sum(jnp.where(oh2, before, 0.0), axis=0, keepdims=True)
    total = before[:, tm - 1:tm] + oh[:, tm - 1:tm]
    carry_ref[...] = jnp.broadcast_to(total, carry_ref.shape)
    cnt_ref[...] = jnp.broadcast_to(total, cnt_ref.shape)
    rt = jnp.zeros_like(lg)
    for r, val in enumerate((i1, i2, r1, r2, g1, g2)):
        rt = jnp.where(row == r, val, rt)
    routet_ref[...] = rt
    route_ref[...] = jnp.concatenate([rt, jnp.zeros((LANE - N_EXPERTS, tm), F32)], axis=0).T


def _out_proj(ret_o, fox_o, mq, mkt, mvs, bd, g_mem, w_out, x, ffn_g, layer, seq, dense=None, router=None):
    T = x.shape[0]
    tm = TOK_TILE
    tps = seq // tm
    row = lambda i: (i, 0)
    fix = lambda i: (0, 0)
    full = lambda a: pl.BlockSpec(a.shape, fix)
    M4 = mkt.shape[-1]
    in_specs = [pl.BlockSpec((tm, 512), row), pl.BlockSpec((tm, 512), row), pl.BlockSpec((tm, MQ_SLAB), row),
                pl.BlockSpec((1, 1, MQ_SLAB, M4), lambda i: (layer, i // tps, 0, 0)),
                pl.BlockSpec((1, 1, M4, 256), lambda i: (layer, i // tps, 0, 0)),
                full(bd), full(g_mem), full(w_out), pl.BlockSpec((tm, D_MODEL), row), full(ffn_g)]
    args = [ret_o, fox_o, mq, mkt, mvs, bd, g_mem, w_out, x, ffn_g]
    params = pltpu.CompilerParams(dimension_semantics=("arbitrary",), vmem_limit_bytes=VMEM_LIMIT)
    if router is None:
        wg, wu, wd, j = dense
        once = pl.Buffered(1)
        w_specs = [pl.BlockSpec((1,) + w.shape[1:], lambda i: (j, 0, 0), pipeline_mode=once) for w in (wg, wu, wd)]
        return pl.pallas_call(
            _out_proj_dense_kernel, grid=(T // tm,), in_specs=in_specs + w_specs,
            out_specs=pl.BlockSpec((tm, D_MODEL), row),
            out_shape=jax.ShapeDtypeStruct((T, D_MODEL), F32),
            compiler_params=params, name="out_proj_dense",
        )(*args, wg, wu, wd)
    rw, tri = router
    return pl.pallas_call(
        _out_proj_moe_kernel, grid=(T // tm,), in_specs=in_specs + [full(rw), full(tri)],
        out_specs=[pl.BlockSpec((tm, D_MODEL), row), pl.BlockSpec((tm * ROW_TILE, LANE), row),
                   pl.BlockSpec((N_EXPERTS, tm), lambda i: (0, i)),
                   pl.BlockSpec((tm, LANE), row), pl.BlockSpec((8, LANE), fix)],
        out_shape=[jax.ShapeDtypeStruct((T, D_MODEL), F32), jax.ShapeDtypeStruct((T * ROW_TILE, LANE), F32),
                   jax.ShapeDtypeStruct((N_EXPERTS, T), F32),
                   jax.ShapeDtypeStruct((T, LANE), F32), jax.ShapeDtypeStruct((8, LANE), F32)],
        scratch_shapes=[pltpu.VMEM((8, LANE), F32)],
        compiler_params=params, name="out_proj_moe",
    )(*args, rw, tri)


def _swiglu(x, wg_ref, wu_ref, wd_ref, y):
    for lo, hi in FF_CHUNKS:
        cols = slice(lo, hi)
        g = jnp.dot(x, wg_ref[:, cols], preferred_element_type=F32)
        u = jnp.dot(x, wu_ref[:, cols], preferred_element_type=F32)
        a = (_silu(g) * u).astype(BF16)
        part = jnp.dot(a, wd_ref[cols, :], preferred_element_type=F32)
        y = part if y is None else y + part
    return y


def _to_row_tiles(ref, x):
    for s in range(ROW_TILE):
        ref[pl.ds(s, x.shape[0], stride=ROW_TILE), :] = x[:, s * LANE:(s + 1) * LANE]


def _from_row_tiles(ref, n):
    return jnp.concatenate([ref[pl.ds(s, n, stride=ROW_TILE), :] for s in range(ROW_TILE)], axis=1)


def _row_copy(src, s, dst, d, sem):
    return pltpu.make_async_copy(src.at[pl.ds(pl.multiple_of(s * ROW_TILE, ROW_TILE), ROW_TILE)],
                                 dst.at[pl.ds(pl.multiple_of(d * ROW_TILE, ROW_TILE), ROW_TILE)], sem)


def _dispatch_kernel(tc, tm, d1_ref, d2_ref, lo_ref, hi_ref, nu_ref, h_ref, xs_ref, zero_ref, sem):
    i = pl.program_id(0)
    n_tiles = xs_ref.shape[0] // (tm * ROW_TILE)

    @pl.when(i == 0)
    def _():
        zero_ref[...] = jnp.zeros_like(zero_ref)
        for e in range(N_EXPERTS):
            def fill(r, carry):
                _row_copy(zero_ref, 0, xs_ref, r, sem.at[1]).start()
                return carry

            def drain(r, carry):
                _row_copy(zero_ref, 0, xs_ref, 0, sem.at[1]).wait()
                return carry

            lax.fori_loop(lo_ref[e], hi_ref[e], fill, 0)
            lax.fori_loop(lo_ref[e], hi_ref[e], drain, 0)

        def tile_copy(t):
            start = pl.multiple_of(t * (tm * ROW_TILE), tm * ROW_TILE)
            return pltpu.make_async_copy(zero_ref, xs_ref.at[pl.ds(start, tm * ROW_TILE)], sem.at[1])

        def fill_tile(t, carry):
            tile_copy(t).start()
            return carry

        def drain_tile(t, carry):
            tile_copy(t).wait()
            return carry

        lax.fori_loop(nu_ref[0], n_tiles, fill_tile, 0)
        lax.fori_loop(nu_ref[0], n_tiles, drain_tile, 0)

    base = i * tc

    def issue(r, carry):
        t = base + r
        _row_copy(h_ref, r, xs_ref, d1_ref[t], sem.at[0]).start()
        _row_copy(h_ref, r, xs_ref, d2_ref[t], sem.at[0]).start()
        return carry

    lax.fori_loop(0, tc, issue, 0, unroll=8)
    for _ in range(2):
        pltpu.make_async_copy(h_ref, xs_ref.at[pl.ds(0, tc * ROW_TILE)], sem.at[0]).wait()


def _dispatch(h, d1, d2, pad_lo, pad_hi, n_used, n_tiles):
    tc = TOK_TILE
    tm = MOE_TILE
    T = d1.shape[0]
    return pl.pallas_call(
        functools.partial(_dispatch_kernel, tc, tm),
        grid_spec=pltpu.PrefetchScalarGridSpec(
            num_scalar_prefetch=5, grid=(T // tc,),
            in_specs=[pl.BlockSpec((tc * ROW_TILE, LANE), lambda i, *_: (i, 0))],
            out_specs=pl.BlockSpec(memory_space=pl.ANY),
            scratch_shapes=[pltpu.VMEM((tm * ROW_TILE, LANE), F32), pltpu.SemaphoreType.DMA((2,))]),
        out_shape=jax.ShapeDtypeStruct((n_tiles * tm * ROW_TILE, LANE), F32),
        compiler_params=pltpu.CompilerParams(dimension_semantics=("arbitrary",), has_side_effects=True,
                                             vmem_limit_bytes=VMEM_LIMIT),
        name="moe_dispatch",
    )(d1, d2, pad_lo, pad_hi, n_used, h)


def _moe_ffn_kernel(te_ref, nu_ref, x_ref, wg_ref, wu_ref, wd_ref, o_ref):
    i = pl.program_id(0)
    tm = x_ref.shape[0] // ROW_TILE

    @pl.when(i < nu_ref[0])
    def _():
        x = _from_row_tiles(x_ref, tm).astype(BF16)
        _to_row_tiles(o_ref, _swiglu(x, wg_ref.at[0, 0], wu_ref.at[0, 0], wd_ref.at[0, 0], None))

    @pl.when(i >= nu_ref[0])
    def _():
        o_ref[...] = jnp.zeros_like(o_ref)


def _moe_ffn(xs, tile_expert, n_used, wg, wu, wd, layer):
    tm = MOE_TILE
    n_tiles = tile_expert.shape[0]
    w_block = lambda i, te, nu: (layer, te[i], 0, 0)
    tile = lambda i, te, nu: (jnp.minimum(i, nu[0] - 1), 0)
    return pl.pallas_call(
        _moe_ffn_kernel,
        grid_spec=pltpu.PrefetchScalarGridSpec(
            num_scalar_prefetch=2, grid=(n_tiles,),
            in_specs=[pl.BlockSpec((tm * ROW_TILE, LANE), tile),
                      pl.BlockSpec((1, 1, D_MODEL, D_FF), w_block),
                      pl.BlockSpec((1, 1, D_MODEL, D_FF), w_block),
                      pl.BlockSpec((1, 1, D_FF, D_MODEL), w_block)],
            out_specs=pl.BlockSpec((tm * ROW_TILE, LANE), lambda i, te, nu: (i, 0))),
        out_shape=jax.ShapeDtypeStruct((n_tiles * tm * ROW_TILE, LANE), F32),
        compiler_params=pltpu.CompilerParams(dimension_semantics=("arbitrary",),
                                             vmem_limit_bytes=VMEM_LIMIT),
        name="moe_ffn",
    )(tile_expert, n_used, xs, wg, wu, wd)


def _combine_kernel(tc, d1_ref, d2_ref, o_hbm, x_ref, route_ref, out_ref, b1_ref, b2_ref, sem):
    base = pl.program_id(0) * tc

    def issue(r, carry):
        t = base + r
        _row_copy(o_hbm, d1_ref[t], b1_ref, r, sem.at[0]).start(priority=0)
        _row_copy(o_hbm, d2_ref[t], b2_ref, r, sem.at[1]).start(priority=1)
        return carry

    lax.fori_loop(0, tc, issue, 0, unroll=8)
    pltpu.make_async_copy(o_hbm.at[pl.ds(0, tc * ROW_TILE)], b1_ref, sem.at[0]).wait()
    pltpu.make_async_copy(o_hbm.at[pl.ds(0, tc * ROW_TILE)], b2_ref, sem.at[1]).wait()
    route = route_ref[...]
    g1 = route[:, 4:5]
    g2 = route[:, 5:6]
    out_ref[...] = x_ref[...] + g1 * _from_row_tiles(b1_ref, tc) + g2 * _from_row_tiles(b2_ref, tc)


def _combine(o, d1, d2, x, route):
    T = x.shape[0]
    tc = 256
    return pl.pallas_call(
        functools.partial(_combine_kernel, tc),
        grid_spec=pltpu.PrefetchScalarGridSpec(
            num_scalar_prefetch=2, grid=(T // tc,),
            in_specs=[pl.BlockSpec(memory_space=pl.ANY),
                      pl.BlockSpec((tc, D_MODEL), lambda i, a, b: (i, 0)),
                      pl.BlockSpec((tc, LANE), lambda i, a, b: (i, 0))],
            out_specs=pl.BlockSpec((tc, D_MODEL), lambda i, a, b: (i, 0)),
            scratch_shapes=[pltpu.VMEM((tc * ROW_TILE, LANE), F32), pltpu.VMEM((tc * ROW_TILE, LANE), F32),
                            pltpu.SemaphoreType.DMA((2,))]),
        out_shape=jax.ShapeDtypeStruct((T, D_MODEL), F32),
        compiler_params=pltpu.CompilerParams(dimension_semantics=("arbitrary",),
                                             vmem_limit_bytes=VMEM_LIMIT),
        name="moe_combine",
    )(d1, d2, o, x, route)


def _moe_layer(h, xm, routet, route, counts, wg, wu, wd, layer):
    T = xm.shape[0]
    tm = MOE_TILE
    n_tiles = (2 * T) // tm + N_EXPERTS
    e1 = routet[0].astype(jnp.int32)
    e2 = routet[1].astype(jnp.int32)
    r1 = routet[2].astype(jnp.int32)
    r2 = routet[3].astype(jnp.int32)
    cnt = counts[:, 0].astype(jnp.int32)
    tiles = (cnt + tm - 1) // tm
    tile_end = jnp.cumsum(tiles)
    offs = (tile_end - tiles) * tm
    d1 = offs[e1] + r1
    d2 = offs[e2] + r2
    n_used = tile_end[-1:]
    tile_id = jnp.arange(n_tiles, dtype=jnp.int32)
    tile_expert = jnp.sum(tile_id[:, None] >= tile_end[None, :], axis=1).astype(jnp.int32)
    tile_expert = jnp.minimum(tile_expert, tile_expert[jnp.maximum(n_used[0] - 1, 0)])
    n_used = n_used.astype(jnp.int32)
    xs = _dispatch(h, d1, d2, offs + cnt, offs + tiles * tm, n_used, n_tiles)
    o = _moe_ffn(xs, tile_expert, n_used, wg, wu, wd, layer)
    return _combine(o, d1, d2, xm, route)


def _constants(seq):
    tm = TOK_TILE
    r = np.arange(tm)
    tri_upper = (r[:, None] <= r[None, :]).astype(np.float32)
    tri_strict = (r[:, None] < r[None, :]).astype(np.float32)
    d = np.arange(256)
    bd = (d[:, None] // 64 == d[None, :] // 64).astype(np.float32) / 64.0
    m = np.arange(MQ_SLAB) - MQ_OFF
    valid = (m >= 0) & (m < MEM_HEADS * MEM_DH)
    bdm = ((m[:, None] // MEM_DH == m[None, :] // MEM_DH) & valid[:, None] & valid[None, :]).astype(np.float32) / MEM_DH
    shift = (d[:, None] + MQ_OFF == np.arange(MQ_SLAB)[None, :]).astype(np.float32)
    in_consts = (jnp.asarray(tri_upper, BF16), jnp.asarray(bd, BF16), jnp.asarray(bdm, BF16))

    half = RET_DK // 2
    pos = jnp.arange(seq, dtype=F32)
    inv_freq = ROPE_BASE ** (-jnp.arange(half, dtype=F32) / half)
    ang = pos[:, None] * inv_freq[None, :]
    cos = jnp.tile(jnp.cos(ang), (1, LANE // half))
    sin = jnp.tile(jnp.concatenate([-jnp.sin(ang), jnp.sin(ang)], axis=1), (1, LANE // RET_DK))

    C = RET_CHUNK
    hh = jnp.arange(RET_HEADS, dtype=F32)
    log_g = jnp.log(1.0 - 2.0 ** (-5.0 - hh))
    idx = jnp.arange(C, dtype=F32)
    diff = idx[:, None] - idx[None, :]
    dmat = jnp.where(diff >= 0, jnp.exp(log_g[:, None, None] * jnp.maximum(diff, 0.0)), 0.0)
    dstack = dmat.reshape(RET_HEADS * C, C)
    k_head = np.arange(RET_HEADS * RET_DK) // RET_DK
    v_head = np.arange(RET_HEADS * RET_DV) // RET_DV
    qd = jnp.exp(log_g[k_head][None, :] * (idx[:, None] + 1.0))
    kd = jnp.exp(log_g[k_head][None, :] * (C - 1.0 - idx[:, None]))
    cd = jnp.exp(log_g[v_head] * C)[None, :]
    bm = jnp.asarray((k_head[:, None] == v_head[None, :]).astype(np.float32))
    hm = jnp.asarray((np.arange(RET_HEADS)[:, None] == k_head[None, :]).astype(np.float32), BF16)
    ret_tables = (hm, dstack, qd, kd, cd, bm)
    return in_consts, cos, sin, ret_tables, jnp.asarray(tri_strict, BF16), jnp.asarray(shift, BF16)


def _pad_heads(g):
    depth = g.shape[0]
    g = g.reshape(depth, FOX_HEADS, FOX_DH)
    return jnp.pad(g, ((0, 0), (0, 0), (0, LANE - FOX_DH))).reshape(depth, 1, FOX_HEADS * LANE)


def kernel(x, mem, attn_norm_g, w_in, forget_bias, fox_q_norm_g, fox_k_norm_g, mem_q_norm_g, mem_k_norm_g,
           mem_norm_g, w_mem_kv, out_norm_g, w_out, ffn_norm_g, dense_w_gate, dense_w_up, dense_w_down,
           router_w, expert_w_gate, expert_w_up, expert_w_down):
    B, S, D = x.shape
    depth = w_in.shape[0]
    T = B * S
    in_consts, cos, sin, ret_tables, tri_strict, shift = _constants(S)
    bd = in_consts[1]

    w_in_p = jnp.pad(w_in, ((0, 0), (0, 0), (0, P_IN - w_in.shape[-1]))).astype(BF16)
    fqg = jnp.tile(fox_q_norm_g, (1, FOX_HEADS))[:, None, :]
    fkg = jnp.tile(fox_k_norm_g, (1, FOX_HEADS))[:, None, :]
    fb = jnp.pad(forget_bias, ((0, 0), (0, 8 - FOX_HEADS)))[:, :, None]
    mqg = jnp.tile(mem_q_norm_g, (1, MEM_HEADS))
    mqg = jnp.pad(mqg, ((0, 0), (MQ_OFF, MQ_SLAB - MQ_OFF - mqg.shape[-1])))[:, None, :]
    mkg = jnp.tile(mem_k_norm_g, (1, MEM_HEADS))[:, None, :]
    n_ret = RET_HEADS * RET_DV
    n_fox = FOX_HEADS * FOX_DH
    g_ret = out_norm_g[:, None, :n_ret]
    g_fox = _pad_heads(out_norm_g[:, n_ret:n_ret + n_fox])
    g_mem = out_norm_g[:, None, n_ret + n_fox:]
    w_fox = w_out[:, n_ret:n_ret + n_fox].reshape(depth, FOX_HEADS, FOX_DH, D)
    w_fox = jnp.pad(w_fox, ((0, 0), (0, 0), (0, LANE - FOX_DH), (0, 0))).reshape(depth, FOX_HEADS * LANE, D)
    w_out_p = jnp.concatenate([w_out[:, :n_ret], w_fox, w_out[:, n_ret + n_fox:]], axis=1).astype(BF16)
    dwg, dwu, dwd = dense_w_gate.astype(BF16), dense_w_up.astype(BF16), dense_w_down.astype(BF16)
    ewg, ewu, ewd = expert_w_gate.astype(BF16), expert_w_up.astype(BF16), expert_w_down.astype(BF16)
    rwt = jnp.swapaxes(router_w, 1, 2)
    rwt_hi = rwt.astype(BF16)
    rwt_lo = (rwt - rwt_hi.astype(F32)).astype(BF16)
    pad16 = lambda w: jnp.pad(w, ((0, 0), (0, 16 - N_EXPERTS), (0, 0)))
    rw = jnp.concatenate([pad16(rwt_hi), pad16(rwt_lo)], axis=1)

    mkt, mvs = _mem_prep(mem, mem_norm_g[:, None, :], w_mem_kv.astype(BF16), mkg, bd, shift)

    xt = x.reshape(T, D)
    for l in range(depth):
        rq, rk, rv, rg, fqt, fka, fvt, mq = _in_proj(
            xt, attn_norm_g[l][None], w_in_p[l], cos, sin, fqg[l], fkg[l], mqg[l], fb[l], in_consts, S)
        ret_o = _retention(rq, rk, rv, rg, ret_tables, g_ret[l], B, S)
        fox_o = _fox(fqt, fka, fvt, g_fox[l], B, S)
        j = l // 2
        if l % 2 == 0:
            xt = _out_proj(ret_o, fox_o, mq, mkt, mvs, bd, g_mem[l], w_out_p[l], xt, ffn_norm_g[l][None], l, S,
                           dense=(dwg, dwu, dwd, j))
        else:
            xm, h2, routet, route, counts = _out_proj(ret_o, fox_o, mq, mkt, mvs, bd, g_mem[l], w_out_p[l], xt,
                                                      ffn_norm_g[l][None], l, S, router=(rw[j], tri_strict))
            xt = _moe_layer(h2, xm, routet, route, counts, ewg, ewu, ewd, j)
    return xt.reshape(B, S, D)
```

```python
import functools

import numpy as np
import jax
import jax.numpy as jnp
from jax import lax
from jax.experimental import pallas as pl
from jax.experimental.pallas import tpu as pltpu

F32 = jnp.float32
BF16 = jnp.bfloat16

D_MODEL = 1024
RET_HEADS = 4
RET_DK = 64
RET_DV = 128
FOX_HEADS = 4
FOX_DH = 64
MEM_HEADS = 4
MEM_DH = 64
RET_CHUNK = 128
D_FF = 2752
N_EXPERTS = 8
ROPE_BASE = 10000.0
EPS = 1e-6

LANE = 128
FF_CHUNKS = ((0, 1408), (1408, 2688), (2688, D_FF))
TOK_TILE = 512
FOX_Q_TILE = 1024
FOX_K_TILE = 512
LOG2E = 1.4426950408889634
ROW_TILE = 8
assert D_MODEL == ROW_TILE * LANE
MOE_TILE = 512
VMEM_LIMIT = 56 * 1024 * 1024
NEG = -1e30

RQ, RK, RV, RG = 0, 256, 512, 1024
FQ, FK, FV = 1536, 1792, 2048
FF_COL = 2304
MQ_OFF = FOX_HEADS
MQ_SLAB = 384
P_IN = FF_COL + MQ_SLAB
N_PIECE = 3
V_ONE = 64


def _split3(x):
    hi = x.astype(BF16)
    r1 = x - hi.astype(F32)
    mid = r1.astype(BF16)
    lo = (r1 - mid.astype(F32)).astype(BF16)
    return hi, mid, lo


def _group_mean_sq(x, bd):
    sq = x * x
    hi = sq.astype(BF16)
    lo = (sq - hi.astype(F32)).astype(BF16)
    return (jnp.dot(hi, bd, preferred_element_type=F32)
            + jnp.dot(lo, bd, preferred_element_type=F32))


def _silu(x):
    return x / (1.0 + jnp.exp(-x))


def _in_proj_kernel(tiles_per_seq, x_ref, g_ref, w_ref, cos_ref, sin_ref, fqg_ref, fkg_ref, mqg_ref,
                    fb_ref, tri_ref, bd_ref, bdm_ref,
                    rq_ref, rk_ref, rv_ref, rg_ref, fqt_ref, fka_ref, fvt_ref, mq_ref, carry_ref):
    i = pl.program_id(0)

    @pl.when(i % tiles_per_seq == 0)
    def _():
        carry_ref[...] = jnp.zeros_like(carry_ref)

    x = x_ref[...]
    ms = jnp.mean(x * x, axis=-1, keepdims=True)
    h = (x * lax.rsqrt(ms + EPS) * g_ref[...]).astype(BF16)

    def proj(a, b):
        return jnp.dot(h, w_ref[:, a:b], preferred_element_type=F32)

    tm = x.shape[0]
    cos = cos_ref[...]
    sin = sin_ref[...]
    first = (lax.broadcasted_iota(jnp.int32, (tm, LANE), 1) % RET_DK) < RET_DK // 2
    for base, ref, scale in ((RQ, rq_ref, RET_DK ** -0.5), (RK, rk_ref, 1.0)):
        for a in (0, LANE):
            t = proj(base + a, base + a + LANE)
            partner = jnp.where(first, pltpu.roll(t, LANE - RET_DK // 2, 1), pltpu.roll(t, RET_DK // 2, 1))
            ref[:, a:a + LANE] = ((t * cos + partner * sin) * scale).astype(BF16)
    rv_ref[...] = proj(RV, RV + 512).astype(BF16)
    rg_ref[...] = proj(RG, RG + 512).astype(BF16)

    zm = proj(FF_COL, FF_COL + MQ_SLAB)
    mq = zm * lax.rsqrt(_group_mean_sq(zm, bdm_ref[...]) + EPS) * mqg_ref[...] * (MEM_DH ** -0.5)
    mq_ref[...] = mq.astype(BF16)

    z = zm[:, 0:LANE].T[0:8, :] + fb_ref[...]
    logf = jnp.minimum(z, 0.0) - jnp.log(1.0 + jnp.exp(-jnp.abs(z)))
    hrow = lax.broadcasted_iota(jnp.int32, logf.shape, 0)
    logf = jnp.where(hrow < FOX_HEADS, logf, 0.0)
    tri = tri_ref[...]
    c = carry_ref[:, 0:1]
    for piece in _split3(logf):
        c = c + jnp.dot(piece, tri, preferred_element_type=F32)
    carry_ref[...] = jnp.broadcast_to(c[:, tm - 1:tm], carry_ref.shape)

    pieces = [p.astype(F32) for p in _split3(c * LOG2E)]
    bd = bd_ref[...]
    fq = proj(FQ, FQ + 256)
    fk = proj(FK, FK + 256)
    qt = (fq * lax.rsqrt(_group_mean_sq(fq, bd) + EPS) * fqg_ref[...] * (FOX_DH ** -0.5 * LOG2E)).T
    kt = (fk * lax.rsqrt(_group_mean_sq(fk, bd) + EPS) * fkg_ref[...]).T
    vt = proj(FV, FV + 256).T
    erow = lax.broadcasted_iota(jnp.int32, (8, tm), 0)
    zeros = jnp.zeros((LANE - FOX_DH - 8, tm), F32)
    ones_row = jnp.where(erow == V_ONE - FOX_DH, 1.0, 0.0)
    for hh in range(FOX_HEADS):
        feat = slice(hh * FOX_DH, (hh + 1) * FOX_DH)
        rows = slice(hh * LANE, (hh + 1) * LANE)
        eq = jnp.where(erow < 2 * N_PIECE, 1.0, 0.0)
        ek = eq
        for p in range(N_PIECE):
            c_row = pieces[p][hh:hh + 1, :]
            eq = jnp.where(erow == p, c_row, eq)
            ek = jnp.where(erow == N_PIECE + p, -c_row, ek)
        fqt_ref[rows, :] = jnp.concatenate([qt[feat], eq, zeros], axis=0).astype(BF16)
        fka_ref[:, rows] = jnp.concatenate([kt[feat], ek, zeros], axis=0).T.astype(BF16)
        fvt_ref[rows, :] = jnp.concatenate([vt[feat], ones_row, zeros], axis=0).astype(BF16)


def _in_proj(x, g, w, cos, sin, fqg, fkg, mqg, fb, consts, seq):
    T = x.shape[0]
    tm = TOK_TILE
    nt = T // tm
    tps = seq // tm
    row = lambda i: (i, 0)
    fix = lambda i: (0, 0)
    full = lambda a: pl.BlockSpec(a.shape, fix)
    tri, bd, bdm = consts
    widths = (256, 256, 512, 512, -512, 512, -512, MQ_SLAB)
    outs = [jax.ShapeDtypeStruct((T, n) if n > 0 else (-n, T), BF16) for n in widths]
    out_specs = [pl.BlockSpec((tm, n), row) if n > 0 else pl.BlockSpec((-n, tm), lambda i: (0, i))
                 for n in widths]
    return pl.pallas_call(
        functools.partial(_in_proj_kernel, tps),
        grid=(nt,),
        in_specs=[pl.BlockSpec((tm, D_MODEL), row), full(g), full(w),
                  pl.BlockSpec((tm, LANE), lambda i: (i % tps, 0)),
                  pl.BlockSpec((tm, LANE), lambda i: (i % tps, 0)),
                  full(fqg), full(fkg), full(mqg), full(fb), full(tri), full(bd), full(bdm)],
        out_specs=out_specs,
        out_shape=outs,
        scratch_shapes=[pltpu.VMEM((8, LANE), F32)],
        compiler_params=pltpu.CompilerParams(dimension_semantics=("arbitrary",),
                                             vmem_limit_bytes=VMEM_LIMIT),
        name="in_proj",
    )(x, g, w, cos, sin, fqg, fkg, mqg, fb, tri, bd, bdm)


def _retention_kernel(chunks, rq_ref, rk_ref, rv_ref, rg_ref, hm_ref, dst_ref, qd_ref, kd_ref, cd_ref,
                      bm_ref, g_ref, o_ref, state_ref):
    @pl.when(pl.program_id(1) == 0)
    def _():
        state_ref[...] = jnp.zeros_like(state_ref)

    C = RET_CHUNK
    for ci in range(chunks):
        rows = slice(ci * C, (ci + 1) * C)
        qc = rq_ref[rows, :]
        kc = rk_ref[rows, :]
        vc = rv_ref[rows, :]
        qs = jnp.concatenate([qc * hm_ref[hh:hh + 1, :] for hh in range(RET_HEADS)], axis=0)
        sc = lax.dot_general(qs, kc, (((1,), (1,)), ((), ())), preferred_element_type=F32)
        sc = (sc * dst_ref[...]).astype(BF16)
        state = state_ref[...]
        qdec = (qc.astype(F32) * qd_ref[...]).astype(BF16)
        cross = jnp.dot(qdec, state.astype(BF16), preferred_element_type=F32)
        kdec = (kc.astype(F32) * kd_ref[...]).astype(BF16)
        kv = lax.dot_general(kdec, vc, (((0,), (0,)), ((), ())), preferred_element_type=F32)
        state_ref[...] = state * cd_ref[...] + kv * bm_ref[...]
        for hh in range(RET_HEADS):
            cols = slice(hh * RET_DV, (hh + 1) * RET_DV)
            o = jnp.dot(sc[hh * C:(hh + 1) * C, :], vc[:, cols], preferred_element_type=F32) + cross[:, cols]
            msq = jnp.mean(o * o, axis=-1, keepdims=True)
            y = o * lax.rsqrt(msq + EPS) * g_ref[:, cols]
            o_ref[rows, cols] = (y * _silu(rg_ref[rows, cols].astype(F32))).astype(BF16)


def _retention(rq, rk, rv, rg, tables, g_ret, batch, seq):
    T = rq.shape[0]
    tm = TOK_TILE
    tps = seq // tm
    row = lambda b, i: (b * tps + i, 0)
    fix = lambda b, i: (0, 0)
    full = lambda a: pl.BlockSpec(a.shape, fix)
    return pl.pallas_call(
        functools.partial(_retention_kernel, tm // RET_CHUNK),
        grid=(batch, tps),
        in_specs=[pl.BlockSpec((tm, 256), row), pl.BlockSpec((tm, 256), row),
                  pl.BlockSpec((tm, 512), row), pl.BlockSpec((tm, 512), row)]
                 + [full(t) for t in tables] + [full(g_ret)],
        out_specs=pl.BlockSpec((tm, 512), row),
        out_shape=jax.ShapeDtypeStruct((T, 512), BF16),
        scratch_shapes=[pltpu.VMEM((RET_HEADS * RET_DK, RET_HEADS * RET_DV), F32)],
        compiler_params=pltpu.CompilerParams(dimension_semantics=("arbitrary", "arbitrary"),
                                             vmem_limit_bytes=VMEM_LIMIT),
        name="retention",
    )(rq, rk, rv, rg, *tables, g_ret)


def _fox_kernel(tq, tk, qt_ref, k_ref, vt_ref, g_ref, o_ref, m_ref, acc_ref, s0_ref, s1_ref):
    i = pl.program_id(2)
    qt = qt_ref[...]
    m_ref[...] = jnp.full_like(m_ref, NEG)
    acc_ref[...] = jnp.zeros_like(acc_ref)
    assert tq == 2 * tk
    s_refs = (s0_ref, s1_ref)

    def scores(j, slot, qlo=0):
        start = pl.multiple_of(j * tk, tk)
        s_refs[slot][:, qlo:] = jnp.dot(k_ref[pl.ds(start, tk), :], qt[:, qlo:], preferred_element_type=F32)

    def update(j, slot, diag, qlo=0):
        start = pl.multiple_of(j * tk, tk)
        st = s_refs[slot][:, qlo:]
        if diag is not None:
            key = lax.broadcasted_iota(jnp.int32, st.shape, 0) + diag * tk
            qry = lax.broadcasted_iota(jnp.int32, st.shape, 1) + qlo
            st = jnp.where(key <= qry, st, NEG)
        m_prev = m_ref[:, qlo:]
        m_new = jnp.maximum(m_prev, jnp.max(st, axis=0, keepdims=True))
        alpha = jnp.exp2(m_prev - m_new)
        p = jnp.exp2(st - m_new).astype(BF16)
        vt = vt_ref[:, pl.ds(start, tk)]
        acc_ref[:, qlo:] = alpha * acc_ref[:, qlo:] + jnp.dot(vt, p, preferred_element_type=F32)
        m_ref[:, qlo:] = m_new

    scores(0, 0)

    def body(t, carry):
        scores(2 * t + 1, 1)
        update(2 * t, 0, None)
        scores(2 * t + 2, 0)
        update(2 * t + 1, 1, None)
        return carry

    lax.fori_loop(0, i, body, 0)
    scores(2 * i + 1, 1, tk)
    update(2 * i, 0, 0)
    update(2 * i + 1, 1, 1, tk)
    acc = acc_ref[...]
    row = lax.broadcasted_iota(jnp.int32, acc.shape, 0)
    o = jnp.where(row < FOX_DH, acc / acc[V_ONE:V_ONE + 1, :], 0.0)
    msq = jnp.sum(o * o, axis=0, keepdims=True) * (1.0 / FOX_DH)
    o_ref[...] = ((o * lax.rsqrt(msq + EPS)).T * g_ref[...]).astype(BF16)


def _fox(fqt, fka, fvt, g_fox, batch, seq):
    T = fka.shape[0]
    tq = FOX_Q_TILE
    tk = FOX_K_TILE
    nq = seq // tq
    return pl.pallas_call(
        functools.partial(_fox_kernel, tq, tk),
        grid=(batch, FOX_HEADS, nq),
        in_specs=[pl.BlockSpec((LANE, tq), lambda b, h, i: (h, b * nq + i)),
                  pl.BlockSpec((seq, LANE), lambda b, h, i: (b, h)),
                  pl.BlockSpec((LANE, seq), lambda b, h, i: (h, b)),
                  pl.BlockSpec((1, LANE), lambda b, h, i: (0, h))],
        out_specs=pl.BlockSpec((tq, LANE), lambda b, h, i: (b * nq + i, h)),
        out_shape=jax.ShapeDtypeStruct((T, FOX_HEADS * LANE), BF16),
        scratch_shapes=[pltpu.VMEM((1, tq), F32), pltpu.VMEM((LANE, tq), F32),
                        pltpu.VMEM((tk, tq), F32), pltpu.VMEM((tk, tq), F32)],
        compiler_params=pltpu.CompilerParams(dimension_semantics=("arbitrary", "arbitrary", "arbitrary"),
                                             vmem_limit_bytes=VMEM_LIMIT),
        name="fox",
    )(fqt, fka, fvt, g_fox)


def _mem_prep_kernel(mem_ref, g_ref, w_ref, kg_ref, bd_ref, shift_ref, mkt_ref, mvs_ref):
    x = mem_ref[0]
    ms = jnp.mean(x * x, axis=-1, keepdims=True)
    mn = (x * lax.rsqrt(ms + EPS) * g_ref[0]).astype(BF16)
    mkv = jnp.dot(mn, w_ref[0], preferred_element_type=F32)
    width = MEM_HEADS * MEM_DH
    mk = mkv[:, :width]
    mv = mkv[:, width:]
    mk = mk * lax.rsqrt(_group_mean_sq(mk, bd_ref[...]) + EPS) * kg_ref[0]
    mk = jnp.dot(mk.astype(BF16), shift_ref[...], preferred_element_type=F32)
    mkt = mk.T
    feat = (lax.broadcasted_iota(jnp.int32, mkt.shape, 0) - MQ_OFF) // MEM_DH
    lane_head = lax.broadcasted_iota(jnp.int32, mv.shape, 1) // MEM_DH
    M = x.shape[0]
    for hh in range(MEM_HEADS):
        mkt_ref[0, 0, :, hh * M:(hh + 1) * M] = jnp.where(feat == hh, mkt, 0.0).astype(BF16)
        mvs_ref[0, 0, hh * M:(hh + 1) * M, :] = jnp.where(lane_head == hh, mv, 0.0).astype(BF16)


def _mem_prep(mem, mem_norm_g, w_mem_kv, mem_k_g, bd, shift):
    B, M, _ = mem.shape
    depth = w_mem_kv.shape[0]
    width = MEM_HEADS * MEM_DH
    return pl.pallas_call(
        _mem_prep_kernel,
        grid=(depth, B),
        in_specs=[pl.BlockSpec((1, M, D_MODEL), lambda l, b: (b, 0, 0)),
                  pl.BlockSpec((1, 1, D_MODEL), lambda l, b: (l, 0, 0)),
                  pl.BlockSpec((1, D_MODEL, 2 * width), lambda l, b: (l, 0, 0)),
                  pl.BlockSpec((1, 1, width), lambda l, b: (l, 0, 0)),
                  pl.BlockSpec(bd.shape, lambda l, b: (0, 0)),
                  pl.BlockSpec(shift.shape, lambda l, b: (0, 0))],
        out_specs=[pl.BlockSpec((1, 1, MQ_SLAB, MEM_HEADS * M), lambda l, b: (l, b, 0, 0)),
                   pl.BlockSpec((1, 1, MEM_HEADS * M, width), lambda l, b: (l, b, 0, 0))],
        out_shape=[jax.ShapeDtypeStruct((depth, B, MQ_SLAB, MEM_HEADS * M), BF16),
                   jax.ShapeDtypeStruct((depth, B, MEM_HEADS * M, width), BF16)],
        compiler_params=pltpu.CompilerParams(dimension_semantics=("arbitrary", "arbitrary"),
                                             vmem_limit_bytes=VMEM_LIMIT),
        name="mem_prep",
    )(mem, mem_norm_g, w_mem_kv, mem_k_g, bd, shift)


def _out_proj_body(ret_ref, fox_ref, mq_ref, mkt_ref, mvs_ref, bd_ref, gm_ref, w_ref, x_ref, fg_ref):
    M = mkt_ref.shape[-1] // MEM_HEADS
    s = jnp.dot(mq_ref[...], mkt_ref[0, 0], preferred_element_type=F32)
    mem_o = None
    for hh in range(MEM_HEADS):
        sh = s[:, hh * M:(hh + 1) * M]
        p = jnp.exp(sh - jnp.max(sh, axis=-1, keepdims=True))
        p = (p / jnp.sum(p, axis=-1, keepdims=True)).astype(BF16)
        part = jnp.dot(p, mvs_ref[0, 0, hh * M:(hh + 1) * M, :], preferred_element_type=F32)
        mem_o = part if mem_o is None else mem_o + part
    mem_o = (mem_o * lax.rsqrt(_group_mean_sq(mem_o, bd_ref[...]) + EPS) * gm_ref[...]).astype(BF16)
    y = (jnp.dot(ret_ref[...], w_ref[0:512, :], preferred_element_type=F32)
         + jnp.dot(fox_ref[...], w_ref[512:1024, :], preferred_element_type=F32)
         + jnp.dot(mem_o, w_ref[1024:1280, :], preferred_element_type=F32))
    xm = x_ref[...] + y
    ms = jnp.mean(xm * xm, axis=-1, keepdims=True)
    return xm, xm * lax.rsqrt(ms + EPS) * fg_ref[...]


def _out_proj_dense_kernel(ret_ref, fox_ref, mq_ref, mkt_ref, mvs_ref, bd_ref, gm_ref, w_ref, x_ref, fg_ref,
                           wg_ref, wu_ref, wd_ref, o_ref):
    xm, hn = _out_proj_body(ret_ref, fox_ref, mq_ref, mkt_ref, mvs_ref, bd_ref, gm_ref, w_ref, x_ref, fg_ref)
    o_ref[...] = _swiglu(hn.astype(BF16), wg_ref.at[0], wu_ref.at[0], wd_ref.at[0], xm)


def _out_proj_moe_kernel(ret_ref, fox_ref, mq_ref, mkt_ref, mvs_ref, bd_ref, gm_ref, w_ref, x_ref, fg_ref,
                         rwt_ref, tri_ref, xm_ref, h_ref, routet_ref, route_ref, cnt_ref, carry_ref):
    @pl.when(pl.program_id(0) == 0)
    def _():
        carry_ref[...] = jnp.zeros_like(carry_ref)

    xm, hn = _out_proj_body(ret_ref, fox_ref, mq_ref, mkt_ref, mvs_ref, bd_ref, gm_ref, w_ref, x_ref, fg_ref)
    xm_ref[...] = xm
    _to_row_tiles(h_ref, hn)
    tm = hn.shape[0]
    h_hi = hn.astype(BF16)
    h_lo = (hn - h_hi.astype(F32)).astype(BF16)
    rwt = rwt_ref[...]
    nt = (((1,), (1,)), ((), ()))
    a = lax.dot_general(rwt, h_hi, nt, preferred_element_type=F32)
    b = lax.dot_general(rwt[0:16], h_lo, nt, preferred_element_type=F32)
    lg = a[0:N_EXPERTS] + a[16:16 + N_EXPERTS] + b[0:N_EXPERTS]
    row = lax.broadcasted_iota(jnp.int32, lg.shape, 0).astype(F32)
    m1 = jnp.max(lg, axis=0, keepdims=True)
    i1 = jnp.min(jnp.where(lg == m1, row, float(N_EXPERTS)), axis=0, keepdims=True)
    oh1 = row == i1
    lg2 = jnp.where(oh1, NEG, lg)
    m2 = jnp.max(lg2, axis=0, keepdims=True)
    i2 = jnp.min(jnp.where(lg2 == m2, row, float(N_EXPERTS)), axis=0, keepdims=True)
    oh2 = row == i2
    e = jnp.exp(m2 - m1)
    g1 = 1.0 / (1.0 + e)
    g2 = e / (1.0 + e)
    oh = jnp.where(oh1 | oh2, 1.0, 0.0)
    oh16 = jnp.concatenate([oh, jnp.zeros_like(oh)], axis=0).astype(BF16)
    before = carry_ref[:, 0:1] + jnp.dot(oh16, tri_ref[...], preferred_element_type=F32)[0:N_EXPERTS]
    r1 = jnp.sum(jnp.where(oh1, before, 0.0), axis=0, keepdims=True)
    r2 = jnp.sum(jnp.where(oh2, before, 0.0), axis=0, keepdims=True)
    total = before[:, tm - 1:tm] + oh[:, tm - 1:tm]
    carry_ref[...] = jnp.broadcast_to(total, carry_ref.shape)
    cnt_ref[...] = jnp.broadcast_to(total, cnt_ref.shape)
    rt = jnp.zeros_like(lg)
    for r, val in enumerate((i1, i2, r1, r2, g1, g2)):
        rt = jnp.where(row == r, val, rt)
    routet_ref[...] = rt
    route_ref[...] = jnp.concatenate([rt, jnp.zeros((LANE - N_EXPERTS, tm), F32)], axis=0).T


def _out_proj(ret_o, fox_o, mq, mkt, mvs, bd, g_mem, w_out, x, ffn_g, layer, seq, dense=None, router=None):
    T = x.shape[0]
    tm = TOK_TILE
    tps = seq // tm
    row = lambda i: (i, 0)
    fix = lambda i: (0, 0)
    full = lambda a: pl.BlockSpec(a.shape, fix)
    M4 = mkt.shape[-1]
    in_specs = [pl.BlockSpec((tm, 512), row), pl.BlockSpec((tm, 512), row), pl.BlockSpec((tm, MQ_SLAB), row),
                pl.BlockSpec((1, 1, MQ_SLAB, M4), lambda i: (layer, i // tps, 0, 0)),
                pl.BlockSpec((1, 1, M4, 256), lambda i: (layer, i // tps, 0, 0)),
                full(bd), full(g_mem), full(w_out), pl.BlockSpec((tm, D_MODEL), row), full(ffn_g)]
    args = [ret_o, fox_o, mq, mkt, mvs, bd, g_mem, w_out, x, ffn_g]
    params = pltpu.CompilerParams(dimension_semantics=("arbitrary",), vmem_limit_bytes=VMEM_LIMIT)
    if router is None:
        wg, wu, wd, j = dense
        once = pl.Buffered(1)
        w_specs = [pl.BlockSpec((1,) + w.shape[1:], lambda i: (j, 0, 0), pipeline_mode=once) for w in (wg, wu, wd)]
        return pl.pallas_call(
            _out_proj_dense_kernel, grid=(T // tm,), in_specs=in_specs + w_specs,
            out_specs=pl.BlockSpec((tm, D_MODEL), row),
            out_shape=jax.ShapeDtypeStruct((T, D_MODEL), F32),
            compiler_params=params, name="out_proj_dense",
        )(*args, wg, wu, wd)
    rw, tri = router
    return pl.pallas_call(
        _out_proj_moe_kernel, grid=(T // tm,), in_specs=in_specs + [full(rw), full(tri)],
        out_specs=[pl.BlockSpec((tm, D_MODEL), row), pl.BlockSpec((tm * ROW_TILE, LANE), row),
                   pl.BlockSpec((N_EXPERTS, tm), lambda i: (0, i)),
                   pl.BlockSpec((tm, LANE), row), pl.BlockSpec((8, LANE), fix)],
        out_shape=[jax.ShapeDtypeStruct((T, D_MODEL), F32), jax.ShapeDtypeStruct((T * ROW_TILE, LANE), F32),
                   jax.ShapeDtypeStruct((N_EXPERTS, T), F32),
                   jax.ShapeDtypeStruct((T, LANE), F32), jax.ShapeDtypeStruct((8, LANE), F32)],
        scratch_shapes=[pltpu.VMEM((8, LANE), F32)],
        compiler_params=params, name="out_proj_moe",
    )(*args, rw, tri)


def _swiglu(x, wg_ref, wu_ref, wd_ref, y):
    for lo, hi in FF_CHUNKS:
        cols = slice(lo, hi)
        g = jnp.dot(x, wg_ref[:, cols], preferred_element_type=F32)
        u = jnp.dot(x, wu_ref[:, cols], preferred_element_type=F32)
        a = (_silu(g) * u).astype(BF16)
        part = jnp.dot(a, wd_ref[cols, :], preferred_element_type=F32)
        y = part if y is None else y + part
    return y


def _to_row_tiles(ref, x):
    for s in range(ROW_TILE):
        ref[pl.ds(s, x.shape[0], stride=ROW_TILE), :] = x[:, s * LANE:(s + 1) * LANE]


def _from_row_tiles(ref, n):
    return jnp.concatenate([ref[pl.ds(s, n, stride=ROW_TILE), :] for s in range(ROW_TILE)], axis=1)


def _row_copy(src, s, dst, d, sem):
    return pltpu.make_async_copy(src.at[pl.ds(pl.multiple_of(s * ROW_TILE, ROW_TILE), ROW_TILE)],
                                 dst.at[pl.ds(pl.multiple_of(d * ROW_TILE, ROW_TILE), ROW_TILE)], sem)


def _dispatch_kernel(tc, tm, d1_ref, d2_ref, lo_ref, hi_ref, nu_ref, h_ref, xs_ref, zero_ref, sem):
    i = pl.program_id(0)
    n_tiles = xs_ref.shape[0] // (tm * ROW_TILE)

    @pl.when(i == 0)
    def _():
        zero_ref[...] = jnp.zeros_like(zero_ref)
        for e in range(N_EXPERTS):
            def fill(r, carry):
                _row_copy(zero_ref, 0, xs_ref, r, sem.at[1]).start()
                return carry

            def drain(r, carry):
                _row_copy(zero_ref, 0, xs_ref, 0, sem.at[1]).wait()
                return carry

            lax.fori_loop(lo_ref[e], hi_ref[e], fill, 0)
            lax.fori_loop(lo_ref[e], hi_ref[e], drain, 0)

        def tile_copy(t):
            start = pl.multiple_of(t * (tm * ROW_TILE), tm * ROW_TILE)
            return pltpu.make_async_copy(zero_ref, xs_ref.at[pl.ds(start, tm * ROW_TILE)], sem.at[1])

        def fill_tile(t, carry):
            tile_copy(t).start()
            return carry

        def drain_tile(t, carry):
            tile_copy(t).wait()
            return carry

        lax.fori_loop(nu_ref[0], n_tiles, fill_tile, 0)
        lax.fori_loop(nu_ref[0], n_tiles, drain_tile, 0)

    base = i * tc

    def issue(r, carry):
        t = base + r
        _row_copy(h_ref, r, xs_ref, d1_ref[t], sem.at[0]).start()
        _row_copy(h_ref, r, xs_ref, d2_ref[t], sem.at[0]).start()
        return carry

    lax.fori_loop(0, tc, issue, 0, unroll=8)
    for _ in range(2):
        pltpu.make_async_copy(h_ref, xs_ref.at[pl.ds(0, tc * ROW_TILE)], sem.at[0]).wait()


def _dispatch(h, d1, d2, pad_lo, pad_hi, n_used, n_tiles):
    tc = TOK_TILE
    tm = MOE_TILE
    T = d1.shape[0]
    return pl.pallas_call(
        functools.partial(_dispatch_kernel, tc, tm),
        grid_spec=pltpu.PrefetchScalarGridSpec(
            num_scalar_prefetch=5, grid=(T // tc,),
            in_specs=[pl.BlockSpec((tc * ROW_TILE, LANE), lambda i, *_: (i, 0))],
            out_specs=pl.BlockSpec(memory_space=pl.ANY),
            scratch_shapes=[pltpu.VMEM((tm * ROW_TILE, LANE), F32), pltpu.SemaphoreType.DMA((2,))]),
        out_shape=jax.ShapeDtypeStruct((n_tiles * tm * ROW_TILE, LANE), F32),
        compiler_params=pltpu.CompilerParams(dimension_semantics=("arbitrary",), has_side_effects=True,
                                             vmem_limit_bytes=VMEM_LIMIT),
        name="moe_dispatch",
    )(d1, d2, pad_lo, pad_hi, n_used, h)


def _moe_ffn_kernel(te_ref, nu_ref, x_ref, wg_ref, wu_ref, wd_ref, o_ref):
    i = pl.program_id(0)
    tm = x_ref.shape[0] // ROW_TILE

    @pl.when(i < nu_ref[0])
    def _():
        x = _from_row_tiles(x_ref, tm).astype(BF16)
        _to_row_tiles(o_ref, _swiglu(x, wg_ref.at[0, 0], wu_ref.at[0, 0], wd_ref.at[0, 0], None))

    @pl.when(i >= nu_ref[0])
    def _():
        o_ref[...] = jnp.zeros_like(o_ref)


def _moe_ffn(xs, tile_expert, n_used, wg, wu, wd, layer):
    tm = MOE_TILE
    n_tiles = tile_expert.shape[0]
    w_block = lambda i, te, nu: (layer, te[i], 0, 0)
    tile = lambda i, te, nu: (jnp.minimum(i, nu[0] - 1), 0)
    return pl.pallas_call(
        _moe_ffn_kernel,
        grid_spec=pltpu.PrefetchScalarGridSpec(
            num_scalar_prefetch=2, grid=(n_tiles,),
            in_specs=[pl.BlockSpec((tm * ROW_TILE, LANE), tile),
                      pl.BlockSpec((1, 1) + wg.shape[2:], w_block),
                      pl.BlockSpec((1, 1) + wu.shape[2:], w_block),
                      pl.BlockSpec((1, 1) + wd.shape[2:], w_block)],
            out_specs=pl.BlockSpec((tm * ROW_TILE, LANE), lambda i, te, nu: (i, 0))),
        out_shape=jax.ShapeDtypeStruct((n_tiles * tm * ROW_TILE, LANE), F32),
        compiler_params=pltpu.CompilerParams(dimension_semantics=("arbitrary",),
                                             vmem_limit_bytes=VMEM_LIMIT),
        name="moe_ffn",
    )(tile_expert, n_used, xs, wg, wu, wd)


def _combine_kernel(tc, d1_ref, d2_ref, o_hbm, x_ref, route_ref, out_ref, b1_ref, b2_ref, sem):
    base = pl.program_id(0) * tc

    def issue(r, carry):
        t = base + r
        _row_copy(o_hbm, d1_ref[t], b1_ref, r, sem.at[0]).start(priority=0)
        _row_copy(o_hbm, d2_ref[t], b2_ref, r, sem.at[1]).start(priority=1)
        return carry

    lax.fori_loop(0, tc, issue, 0, unroll=8)
    pltpu.make_async_copy(o_hbm.at[pl.ds(0, tc * ROW_TILE)], b1_ref, sem.at[0]).wait()
    pltpu.make_async_copy(o_hbm.at[pl.ds(0, tc * ROW_TILE)], b2_ref, sem.at[1]).wait()
    route = route_ref[...]
    g1 = route[:, 4:5]
    g2 = route[:, 5:6]
    out_ref[...] = x_ref[...] + g1 * _from_row_tiles(b1_ref, tc) + g2 * _from_row_tiles(b2_ref, tc)


def _combine(o, d1, d2, x, route):
    T = x.shape[0]
    tc = 256
    return pl.pallas_call(
        functools.partial(_combine_kernel, tc),
        grid_spec=pltpu.PrefetchScalarGridSpec(
            num_scalar_prefetch=2, grid=(T // tc,),
            in_specs=[pl.BlockSpec(memory_space=pl.ANY),
                      pl.BlockSpec((tc, D_MODEL), lambda i, a, b: (i, 0)),
                      pl.BlockSpec((tc, LANE), lambda i, a, b: (i, 0))],
            out_specs=pl.BlockSpec((tc, D_MODEL), lambda i, a, b: (i, 0)),
            scratch_shapes=[pltpu.VMEM((tc * ROW_TILE, LANE), F32), pltpu.VMEM((tc * ROW_TILE, LANE), F32),
                            pltpu.SemaphoreType.DMA((2,))]),
        out_shape=jax.ShapeDtypeStruct((T, D_MODEL), F32),
        compiler_params=pltpu.CompilerParams(dimension_semantics=("arbitrary",),
                                             vmem_limit_bytes=VMEM_LIMIT),
        name="moe_combine",
    )(d1, d2, o, x, route)


def _moe_layer(h, xm, routet, route, counts, wg, wu, wd, layer):
    T = xm.shape[0]
    tm = MOE_TILE
    n_tiles = (2 * T) // tm + N_EXPERTS
    e1 = routet[0].astype(jnp.int32)
    e2 = routet[1].astype(jnp.int32)
    r1 = routet[2].astype(jnp.int32)
    r2 = routet[3].astype(jnp.int32)
    cnt = counts[:, 0].astype(jnp.int32)
    tiles = (cnt + tm - 1) // tm
    tile_end = jnp.cumsum(tiles)
    offs = (tile_end - tiles) * tm
    d1 = offs[e1] + r1
    d2 = offs[e2] + r2
    n_used = tile_end[-1:]
    tile_id = jnp.arange(n_tiles, dtype=jnp.int32)
    tile_expert = jnp.sum(tile_id[:, None] >= tile_end[None, :], axis=1).astype(jnp.int32)
    tile_expert = jnp.minimum(tile_expert, tile_expert[jnp.maximum(n_used[0] - 1, 0)])
    n_used = n_used.astype(jnp.int32)
    xs = _dispatch(h, d1, d2, offs + cnt, offs + tiles * tm, n_used, n_tiles)
    o = _moe_ffn(xs, tile_expert, n_used, wg, wu, wd, layer)
    return _combine(o, d1, d2, xm, route)


def _constants(seq):
    tm = TOK_TILE
    r = np.arange(tm)
    tri_upper = (r[:, None] <= r[None, :]).astype(np.float32)
    tri_strict = (r[:, None] < r[None, :]).astype(np.float32)
    d = np.arange(256)
    bd = (d[:, None] // 64 == d[None, :] // 64).astype(np.float32) / 64.0
    m = np.arange(MQ_SLAB) - MQ_OFF
    valid = (m >= 0) & (m < MEM_HEADS * MEM_DH)
    bdm = ((m[:, None] // MEM_DH == m[None, :] // MEM_DH) & valid[:, None] & valid[None, :]).astype(np.float32) / MEM_DH
    shift = (d[:, None] + MQ_OFF == np.arange(MQ_SLAB)[None, :]).astype(np.float32)
    in_consts = (jnp.asarray(tri_upper, BF16), jnp.asarray(bd, BF16), jnp.asarray(bdm, BF16))

    half = RET_DK // 2
    pos = jnp.arange(seq, dtype=F32)
    inv_freq = ROPE_BASE ** (-jnp.arange(half, dtype=F32) / half)
    ang = pos[:, None] * inv_freq[None, :]
    cos = jnp.tile(jnp.cos(ang), (1, LANE // half))
    sin = jnp.tile(jnp.concatenate([-jnp.sin(ang), jnp.sin(ang)], axis=1), (1, LANE // RET_DK))

    C = RET_CHUNK
    hh = jnp.arange(RET_HEADS, dtype=F32)
    log_g = jnp.log(1.0 - 2.0 ** (-5.0 - hh))
    idx = jnp.arange(C, dtype=F32)
    diff = idx[:, None] - idx[None, :]
    dmat = jnp.where(diff >= 0, jnp.exp(log_g[:, None, None] * jnp.maximum(diff, 0.0)), 0.0)
    dstack = dmat.reshape(RET_HEADS * C, C)
    k_head = np.arange(RET_HEADS * RET_DK) // RET_DK
    v_head = np.arange(RET_HEADS * RET_DV) // RET_DV
    qd = jnp.exp(log_g[k_head][None, :] * (idx[:, None] + 1.0))
    kd = jnp.exp(log_g[k_head][None, :] * (C - 1.0 - idx[:, None]))
    cd = jnp.exp(log_g[v_head] * C)[None, :]
    bm = jnp.asarray((k_head[:, None] == v_head[None, :]).astype(np.float32))
    hm = jnp.asarray((np.arange(RET_HEADS)[:, None] == k_head[None, :]).astype(np.float32), BF16)
    ret_tables = (hm, dstack, qd, kd, cd, bm)
    return in_consts, cos, sin, ret_tables, jnp.asarray(tri_strict, BF16), jnp.asarray(shift, BF16)


def _pad_heads(g):
    depth = g.shape[0]
    g = g.reshape(depth, FOX_HEADS, FOX_DH)
    return jnp.pad(g, ((0, 0), (0, 0), (0, LANE - FOX_DH))).reshape(depth, 1, FOX_HEADS * LANE)


def kernel(x, mem, attn_norm_g, w_in, forget_bias, fox_q_norm_g, fox_k_norm_g, mem_q_norm_g, mem_k_norm_g,
           mem_norm_g, w_mem_kv, out_norm_g, w_out, ffn_norm_g, dense_w_gate, dense_w_up, dense_w_down,
           router_w, expert_w_gate, expert_w_up, expert_w_down):
    B, S, D = x.shape
    depth = w_in.shape[0]
    T = B * S
    in_consts, cos, sin, ret_tables, tri_strict, shift = _constants(S)
    bd = in_consts[1]

    w_in_p = jnp.pad(w_in, ((0, 0), (0, 0), (0, P_IN - w_in.shape[-1]))).astype(BF16)
    fqg = jnp.tile(fox_q_norm_g, (1, FOX_HEADS))[:, None, :]
    fkg = jnp.tile(fox_k_norm_g, (1, FOX_HEADS))[:, None, :]
    fb = jnp.pad(forget_bias, ((0, 0), (0, 8 - FOX_HEADS)))[:, :, None]
    mqg = jnp.tile(mem_q_norm_g, (1, MEM_HEADS))
    mqg = jnp.pad(mqg, ((0, 0), (MQ_OFF, MQ_SLAB - MQ_OFF - mqg.shape[-1])))[:, None, :]
    mkg = jnp.tile(mem_k_norm_g, (1, MEM_HEADS))[:, None, :]
    n_ret = RET_HEADS * RET_DV
    n_fox = FOX_HEADS * FOX_DH
    g_ret = out_norm_g[:, None, :n_ret]
    g_fox = _pad_heads(out_norm_g[:, n_ret:n_ret + n_fox])
    g_mem = out_norm_g[:, None, n_ret + n_fox:]
    w_fox = w_out[:, n_ret:n_ret + n_fox].reshape(depth, FOX_HEADS, FOX_DH, D)
    w_fox = jnp.pad(w_fox, ((0, 0), (0, 0), (0, LANE - FOX_DH), (0, 0))).reshape(depth, FOX_HEADS * LANE, D)
    w_out_p = jnp.concatenate([w_out[:, :n_ret], w_fox, w_out[:, n_ret + n_fox:]], axis=1).astype(BF16)
    dwg, dwu, dwd = dense_w_gate.astype(BF16), dense_w_up.astype(BF16), dense_w_down.astype(BF16)
    ff_pad = -D_FF % LANE
    pad_cols = lambda w: jnp.pad(w, ((0, 0), (0, 0), (0, 0), (0, ff_pad))).astype(BF16)
    ewg, ewu, ewd = pad_cols(expert_w_gate), pad_cols(expert_w_up), expert_w_down.astype(BF16)
    rwt = jnp.swapaxes(router_w, 1, 2)
    rwt_hi = rwt.astype(BF16)
    rwt_lo = (rwt - rwt_hi.astype(F32)).astype(BF16)
    pad16 = lambda w: jnp.pad(w, ((0, 0), (0, 16 - N_EXPERTS), (0, 0)))
    rw = jnp.concatenate([pad16(rwt_hi), pad16(rwt_lo)], axis=1)

    mkt, mvs = _mem_prep(mem, mem_norm_g[:, None, :], w_mem_kv.astype(BF16), mkg, bd, shift)

    xt = x.reshape(T, D)
    for l in range(depth):
        rq, rk, rv, rg, fqt, fka, fvt, mq = _in_proj(
            xt, attn_norm_g[l][None], w_in_p[l], cos, sin, fqg[l], fkg[l], mqg[l], fb[l], in_consts, S)
        ret_o = _retention(rq, rk, rv, rg, ret_tables, g_ret[l], B, S)
        fox_o = _fox(fqt, fka, fvt, g_fox[l], B, S)
        j = l // 2
        if l % 2 == 0:
            xt = _out_proj(ret_o, fox_o, mq, mkt, mvs, bd, g_mem[l], w_out_p[l], xt, ffn_norm_g[l][None], l, S,
                           dense=(dwg, dwu, dwd, j))
        else:
            xm, h2, routet, route, counts = _out_proj(ret_o, fox_o, mq, mkt, mvs, bd, g_mem[l], w_out_p[l], xt,
                                                      ffn_norm_g[l][None], l, S, router=(rw[j], tri_strict))
            xt = _moe_layer(h2, xm, routet, route, counts, ewg, ewu, ewd, j)
    return xt.reshape(B, S, D)
```

```python
import functools

import numpy as np
import jax
import jax.numpy as jnp
from jax import lax
from jax.experimental import pallas as pl
from jax.experimental.pallas import tpu as pltpu

F32 = jnp.float32
BF16 = jnp.bfloat16

D_MODEL = 1024
RET_HEADS = 4
RET_DK = 64
RET_DV = 128
FOX_HEADS = 4
FOX_DH = 64
MEM_HEADS = 4
MEM_DH = 64
RET_CHUNK = 128
D_FF = 2752
N_EXPERTS = 8
ROPE_BASE = 10000.0
EPS = 1e-6

LANE = 128
FF_CHUNKS = ((0, 1408), (1408, 2688), (2688, D_FF))
TOK_TILE = 512
FOX_Q_TILE = 1024
FOX_K_TILE = 512
LOG2E = 1.4426950408889634
ROW_TILE = 8
assert D_MODEL == ROW_TILE * LANE
MOE_TILE = 512
VMEM_LIMIT = 56 * 1024 * 1024
NEG = -1e30

RQ, RK, RV, RG = 0, 256, 512, 1024
FQ, FK, FV = 1536, 1792, 2048
FF_COL = 2304
MQ_OFF = FOX_HEADS
MQ_SLAB = 384
P_IN = FF_COL + MQ_SLAB
N_PIECE = 3
V_ONE = 64


def _split3(x):
    hi = x.astype(BF16)
    r1 = x - hi.astype(F32)
    mid = r1.astype(BF16)
    lo = (r1 - mid.astype(F32)).astype(BF16)
    return hi, mid, lo


def _group_mean_sq(x, bd):
    sq = x * x
    hi = sq.astype(BF16)
    lo = (sq - hi.astype(F32)).astype(BF16)
    return (jnp.dot(hi, bd, preferred_element_type=F32)
            + jnp.dot(lo, bd, preferred_element_type=F32))


def _silu(x):
    return x / (1.0 + jnp.exp(-x))


def _in_proj_kernel(tiles_per_seq, x_ref, g_ref, w_ref, cos_ref, sin_ref, fqg_ref, fkg_ref, mqg_ref,
                    fb_ref, tri_ref, bd_ref, bdm_ref,
                    rq_ref, rk_ref, rv_ref, rg_ref, fqt_ref, fka_ref, fvt_ref, mq_ref, carry_ref):
    i = pl.program_id(0)

    @pl.when(i % tiles_per_seq == 0)
    def _():
        carry_ref[...] = jnp.zeros_like(carry_ref)

    x = x_ref[...]
    ms = jnp.mean(x * x, axis=-1, keepdims=True)
    h = (x * lax.rsqrt(ms + EPS) * g_ref[...]).astype(BF16)

    def proj(a, b):
        return jnp.dot(h, w_ref[:, a:b], preferred_element_type=F32)

    tm = x.shape[0]
    cos = cos_ref[...]
    sin = sin_ref[...]
    first = (lax.broadcasted_iota(jnp.int32, (tm, LANE), 1) % RET_DK) < RET_DK // 2
    for base, ref, scale in ((RQ, rq_ref, RET_DK ** -0.5), (RK, rk_ref, 1.0)):
        for a in (0, LANE):
            t = proj(base + a, base + a + LANE)
            partner = jnp.where(first, pltpu.roll(t, LANE - RET_DK // 2, 1), pltpu.roll(t, RET_DK // 2, 1))
            ref[:, a:a + LANE] = ((t * cos + partner * sin) * scale).astype(BF16)
    rv_ref[...] = proj(RV, RV + 512).astype(BF16)
    rg_ref[...] = proj(RG, RG + 512).astype(BF16)

    zm = proj(FF_COL, FF_COL + MQ_SLAB)
    mq = zm * lax.rsqrt(_group_mean_sq(zm, bdm_ref[...]) + EPS) * mqg_ref[...] * (MEM_DH ** -0.5)
    mq_ref[...] = mq.astype(BF16)

    z = zm[:, 0:LANE].T[0:8, :] + fb_ref[...]
    logf = jnp.minimum(z, 0.0) - jnp.log(1.0 + jnp.exp(-jnp.abs(z)))
    hrow = lax.broadcasted_iota(jnp.int32, logf.shape, 0)
    logf = jnp.where(hrow < FOX_HEADS, logf, 0.0)
    tri = tri_ref[...]
    c = carry_ref[:, 0:1]
    for piece in _split3(logf):
        c = c + jnp.dot(piece, tri, preferred_element_type=F32)
    carry_ref[...] = jnp.broadcast_to(c[:, tm - 1:tm], carry_ref.shape)

    pieces = [p.astype(F32) for p in _split3(c * LOG2E)]
    bd = bd_ref[...]
    fq = proj(FQ, FQ + 256)
    fk = proj(FK, FK + 256)
    qt = (fq * lax.rsqrt(_group_mean_sq(fq, bd) + EPS) * fqg_ref[...] * (FOX_DH ** -0.5 * LOG2E)).T
    kt = (fk * lax.rsqrt(_group_mean_sq(fk, bd) + EPS) * fkg_ref[...]).T
    vt = proj(FV, FV + 256).T
    erow = lax.broadcasted_iota(jnp.int32, (8, tm), 0)
    zeros = jnp.zeros((LANE - FOX_DH - 8, tm), F32)
    ones_row = jnp.where(erow == V_ONE - FOX_DH, 1.0, 0.0)
    for hh in range(FOX_HEADS):
        feat = slice(hh * FOX_DH, (hh + 1) * FOX_DH)
        rows = slice(hh * LANE, (hh + 1) * LANE)
        eq = jnp.where(erow < 2 * N_PIECE, 1.0, 0.0)
        ek = eq
        for p in range(N_PIECE):
            c_row = pieces[p][hh:hh + 1, :]
            eq = jnp.where(erow == p, c_row, eq)
            ek = jnp.where(erow == N_PIECE + p, -c_row, ek)
        fqt_ref[rows, :] = jnp.concatenate([qt[feat], eq, zeros], axis=0).astype(BF16)
        fka_ref[:, rows] = jnp.concatenate([kt[feat], ek, zeros], axis=0).T.astype(BF16)
        fvt_ref[rows, :] = jnp.concatenate([vt[feat], ones_row, zeros], axis=0).astype(BF16)


def _in_proj(x, g, w, cos, sin, fqg, fkg, mqg, fb, consts, seq):
    T = x.shape[0]
    tm = TOK_TILE
    nt = T // tm
    tps = seq // tm
    row = lambda i: (i, 0)
    fix = lambda i: (0, 0)
    full = lambda a: pl.BlockSpec(a.shape, fix)
    tri, bd, bdm = consts
    widths = (256, 256, 512, 512, -512, 512, -512, MQ_SLAB)
    outs = [jax.ShapeDtypeStruct((T, n) if n > 0 else (-n, T), BF16) for n in widths]
    out_specs = [pl.BlockSpec((tm, n), row) if n > 0 else pl.BlockSpec((-n, tm), lambda i: (0, i))
                 for n in widths]
    return pl.pallas_call(
        functools.partial(_in_proj_kernel, tps),
        grid=(nt,),
        in_specs=[pl.BlockSpec((tm, D_MODEL), row), full(g), full(w),
                  pl.BlockSpec((tm, LANE), lambda i: (i % tps, 0)),
                  pl.BlockSpec((tm, LANE), lambda i: (i % tps, 0)),
                  full(fqg), full(fkg), full(mqg), full(fb), full(tri), full(bd), full(bdm)],
        out_specs=out_specs,
        out_shape=outs,
        scratch_shapes=[pltpu.VMEM((8, LANE), F32)],
        compiler_params=pltpu.CompilerParams(dimension_semantics=("arbitrary",),
                                             vmem_limit_bytes=VMEM_LIMIT),
        name="in_proj",
    )(x, g, w, cos, sin, fqg, fkg, mqg, fb, tri, bd, bdm)


def _retention_kernel(chunks, rq_ref, rk_ref, rv_ref, rg_ref, hm_ref, dst_ref, qd_ref, kd_ref, cd_ref,
                      bm_ref, g_ref, o_ref, state_ref):
    @pl.when(pl.program_id(1) == 0)
    def _():
        state_ref[...] = jnp.zeros_like(state_ref)

    C = RET_CHUNK
    for ci in range(chunks):
        rows = slice(ci * C, (ci + 1) * C)
        qc = rq_ref[rows, :]
        kc = rk_ref[rows, :]
        vc = rv_ref[rows, :]
        qs = jnp.concatenate([qc * hm_ref[hh:hh + 1, :] for hh in range(RET_HEADS)], axis=0)
        sc = lax.dot_general(qs, kc, (((1,), (1,)), ((), ())), preferred_element_type=F32)
        sc = (sc * dst_ref[...]).astype(BF16)
        state = state_ref[...]
        qdec = (qc.astype(F32) * qd_ref[...]).astype(BF16)
        cross = jnp.dot(qdec, state.astype(BF16), preferred_element_type=F32)
        kdec = (kc.astype(F32) * kd_ref[...]).astype(BF16)
        kv = lax.dot_general(kdec, vc, (((0,), (0,)), ((), ())), preferred_element_type=F32)
        state_ref[...] = state * cd_ref[...] + kv * bm_ref[...]
        for hh in range(RET_HEADS):
            cols = slice(hh * RET_DV, (hh + 1) * RET_DV)
            o = jnp.dot(sc[hh * C:(hh + 1) * C, :], vc[:, cols], preferred_element_type=F32) + cross[:, cols]
            msq = jnp.mean(o * o, axis=-1, keepdims=True)
            y = o * lax.rsqrt(msq + EPS) * g_ref[:, cols]
            o_ref[rows, cols] = (y * _silu(rg_ref[rows, cols].astype(F32))).astype(BF16)


def _retention(rq, rk, rv, rg, tables, g_ret, batch, seq):
    T = rq.shape[0]
    tm = TOK_TILE
    tps = seq // tm
    row = lambda b, i: (b * tps + i, 0)
    fix = lambda b, i: (0, 0)
    full = lambda a: pl.BlockSpec(a.shape, fix)
    return pl.pallas_call(
        functools.partial(_retention_kernel, tm // RET_CHUNK),
        grid=(batch, tps),
        in_specs=[pl.BlockSpec((tm, 256), row), pl.BlockSpec((tm, 256), row),
                  pl.BlockSpec((tm, 512), row), pl.BlockSpec((tm, 512), row)]
                 + [full(t) for t in tables] + [full(g_ret)],
        out_specs=pl.BlockSpec((tm, 512), row),
        out_shape=jax.ShapeDtypeStruct((T, 512), BF16),
        scratch_shapes=[pltpu.VMEM((RET_HEADS * RET_DK, RET_HEADS * RET_DV), F32)],
        compiler_params=pltpu.CompilerParams(dimension_semantics=("arbitrary", "arbitrary"),
                                             vmem_limit_bytes=VMEM_LIMIT),
        name="retention",
    )(rq, rk, rv, rg, *tables, g_ret)


def _fox_kernel(tq, tk, qt_ref, k_ref, vt_ref, g_ref, o_ref, m_ref, acc_ref, s0_ref, s1_ref):
    i = pl.program_id(2)
    qt = qt_ref[...]
    m_ref[...] = jnp.full_like(m_ref, NEG)
    acc_ref[...] = jnp.zeros_like(acc_ref)
    assert tq == 2 * tk
    s_refs = (s0_ref, s1_ref)

    def scores(j, slot):
        start = pl.multiple_of(j * tk, tk)
        s_refs[slot][...] = jnp.dot(k_ref[pl.ds(start, tk), :], qt, preferred_element_type=F32)

    def update(j, slot, diag):
        start = pl.multiple_of(j * tk, tk)
        st = s_refs[slot][...]
        if diag is not None:
            key = lax.broadcasted_iota(jnp.int32, st.shape, 0) + diag * tk
            qry = lax.broadcasted_iota(jnp.int32, st.shape, 1)
            st = jnp.where(key <= qry, st, NEG)
        m_prev = m_ref[...]
        m_new = jnp.maximum(m_prev, jnp.max(st, axis=0, keepdims=True))
        alpha = jnp.exp2(m_prev - m_new)
        p = jnp.exp2(st - m_new).astype(BF16)
        vt = vt_ref[:, pl.ds(start, tk)]
        acc_ref[...] = alpha * acc_ref[...] + jnp.dot(vt, p, preferred_element_type=F32)
        m_ref[...] = m_new

    scores(0, 0)

    def body(t, carry):
        scores(2 * t + 1, 1)
        update(2 * t, 0, None)
        scores(2 * t + 2, 0)
        update(2 * t + 1, 1, None)
        return carry

    lax.fori_loop(0, i, body, 0)
    scores(2 * i + 1, 1)
    update(2 * i, 0, 0)
    update(2 * i + 1, 1, 1)
    acc = acc_ref[...]
    row = lax.broadcasted_iota(jnp.int32, acc.shape, 0)
    o = jnp.where(row < FOX_DH, acc / acc[V_ONE:V_ONE + 1, :], 0.0)
    msq = jnp.sum(o * o, axis=0, keepdims=True) * (1.0 / FOX_DH)
    o_ref[...] = ((o * lax.rsqrt(msq + EPS)).T * g_ref[...]).astype(BF16)


def _fox(fqt, fka, fvt, g_fox, batch, seq):
    T = fka.shape[0]
    tq = FOX_Q_TILE
    tk = FOX_K_TILE
    nq = seq // tq
    return pl.pallas_call(
        functools.partial(_fox_kernel, tq, tk),
        grid=(batch, FOX_HEADS, nq),
        in_specs=[pl.BlockSpec((LANE, tq), lambda b, h, i: (h, b * nq + i)),
                  pl.BlockSpec((seq, LANE), lambda b, h, i: (b, h)),
                  pl.BlockSpec((LANE, seq), lambda b, h, i: (h, b)),
                  pl.BlockSpec((1, LANE), lambda b, h, i: (0, h))],
        out_specs=pl.BlockSpec((tq, LANE), lambda b, h, i: (b * nq + i, h)),
        out_shape=jax.ShapeDtypeStruct((T, FOX_HEADS * LANE), BF16),
        scratch_shapes=[pltpu.VMEM((1, tq), F32), pltpu.VMEM((LANE, tq), F32),
                        pltpu.VMEM((tk, tq), F32), pltpu.VMEM((tk, tq), F32)],
        compiler_params=pltpu.CompilerParams(dimension_semantics=("arbitrary", "arbitrary", "arbitrary"),
                                             vmem_limit_bytes=VMEM_LIMIT),
        name="fox",
    )(fqt, fka, fvt, g_fox)


def _mem_prep_kernel(mem_ref, g_ref, w_ref, kg_ref, bd_ref, shift_ref, mkt_ref, mvs_ref):
    x = mem_ref[0]
    ms = jnp.mean(x * x, axis=-1, keepdims=True)
    mn = (x * lax.rsqrt(ms + EPS) * g_ref[0]).astype(BF16)
    mkv = jnp.dot(mn, w_ref[0], preferred_element_type=F32)
    width = MEM_HEADS * MEM_DH
    mk = mkv[:, :width]
    mv = mkv[:, width:]
    mk = mk * lax.rsqrt(_group_mean_sq(mk, bd_ref[...]) + EPS) * kg_ref[0]
    mk = jnp.dot(mk.astype(BF16), shift_ref[...], preferred_element_type=F32)
    mkt = mk.T
    feat = (lax.broadcasted_iota(jnp.int32, mkt.shape, 0) - MQ_OFF) // MEM_DH
    lane_head = lax.broadcasted_iota(jnp.int32, mv.shape, 1) // MEM_DH
    M = x.shape[0]
    for hh in range(MEM_HEADS):
        mkt_ref[0, 0, :, hh * M:(hh + 1) * M] = jnp.where(feat == hh, mkt, 0.0).astype(BF16)
        mvs_ref[0, 0, hh * M:(hh + 1) * M, :] = jnp.where(lane_head == hh, mv, 0.0).astype(BF16)


def _mem_prep(mem, mem_norm_g, w_mem_kv, mem_k_g, bd, shift):
    B, M, _ = mem.shape
    depth = w_mem_kv.shape[0]
    width = MEM_HEADS * MEM_DH
    return pl.pallas_call(
        _mem_prep_kernel,
        grid=(depth, B),
        in_specs=[pl.BlockSpec((1, M, D_MODEL), lambda l, b: (b, 0, 0)),
                  pl.BlockSpec((1, 1, D_MODEL), lambda l, b: (l, 0, 0)),
                  pl.BlockSpec((1, D_MODEL, 2 * width), lambda l, b: (l, 0, 0)),
                  pl.BlockSpec((1, 1, width), lambda l, b: (l, 0, 0)),
                  pl.BlockSpec(bd.shape, lambda l, b: (0, 0)),
                  pl.BlockSpec(shift.shape, lambda l, b: (0, 0))],
        out_specs=[pl.BlockSpec((1, 1, MQ_SLAB, MEM_HEADS * M), lambda l, b: (l, b, 0, 0)),
                   pl.BlockSpec((1, 1, MEM_HEADS * M, width), lambda l, b: (l, b, 0, 0))],
        out_shape=[jax.ShapeDtypeStruct((depth, B, MQ_SLAB, MEM_HEADS * M), BF16),
                   jax.ShapeDtypeStruct((depth, B, MEM_HEADS * M, width), BF16)],
        compiler_params=pltpu.CompilerParams(dimension_semantics=("arbitrary", "arbitrary"),
                                             vmem_limit_bytes=VMEM_LIMIT),
        name="mem_prep",
    )(mem, mem_norm_g, w_mem_kv, mem_k_g, bd, shift)


def _out_proj_body(ret_ref, fox_ref, mq_ref, mkt_ref, mvs_ref, bd_ref, gm_ref, w_ref, x_ref, fg_ref):
    M = mkt_ref.shape[-1] // MEM_HEADS
    s = jnp.dot(mq_ref[...], mkt_ref[0, 0], preferred_element_type=F32)
    mem_o = None
    for hh in range(MEM_HEADS):
        sh = s[:, hh * M:(hh + 1) * M]
        p = jnp.exp(sh - jnp.max(sh, axis=-1, keepdims=True))
        p = (p / jnp.sum(p, axis=-1, keepdims=True)).astype(BF16)
        part = jnp.dot(p, mvs_ref[0, 0, hh * M:(hh + 1) * M, :], preferred_element_type=F32)
        mem_o = part if mem_o is None else mem_o + part
    mem_o = (mem_o * lax.rsqrt(_group_mean_sq(mem_o, bd_ref[...]) + EPS) * gm_ref[...]).astype(BF16)
    y = (jnp.dot(ret_ref[...], w_ref[0:512, :], preferred_element_type=F32)
         + jnp.dot(fox_ref[...], w_ref[512:1024, :], preferred_element_type=F32)
         + jnp.dot(mem_o, w_ref[1024:1280, :], preferred_element_type=F32))
    xm = x_ref[...] + y
    ms = jnp.mean(xm * xm, axis=-1, keepdims=True)
    return xm, xm * lax.rsqrt(ms + EPS) * fg_ref[...]


def _out_proj_dense_kernel(ret_ref, fox_ref, mq_ref, mkt_ref, mvs_ref, bd_ref, gm_ref, w_ref, x_ref, fg_ref,
                           wg_ref, wu_ref, wd_ref, o_ref):
    xm, hn = _out_proj_body(ret_ref, fox_ref, mq_ref, mkt_ref, mvs_ref, bd_ref, gm_ref, w_ref, x_ref, fg_ref)
    o_ref[...] = _swiglu(hn.astype(BF16), wg_ref.at[0], wu_ref.at[0], wd_ref.at[0], xm)


def _out_proj_moe_kernel(ret_ref, fox_ref, mq_ref, mkt_ref, mvs_ref, bd_ref, gm_ref, w_ref, x_ref, fg_ref,
                         rwt_ref, tri_ref, xm_ref, h_ref, routet_ref, route_ref, cnt_ref, carry_ref):
    @pl.when(pl.program_id(0) == 0)
    def _():
        carry_ref[...] = jnp.zeros_like(carry_ref)

    xm, hn = _out_proj_body(ret_ref, fox_ref, mq_ref, mkt_ref, mvs_ref, bd_ref, gm_ref, w_ref, x_ref, fg_ref)
    xm_ref[...] = xm
    _to_row_tiles(h_ref, hn)
    tm = hn.shape[0]
    h_hi = hn.astype(BF16)
    h_lo = (hn - h_hi.astype(F32)).astype(BF16)
    rwt = rwt_ref[...]
    nt = (((1,), (1,)), ((), ()))
    a = lax.dot_general(rwt, h_hi, nt, preferred_element_type=F32)
    b = lax.dot_general(rwt[0:16], h_lo, nt, preferred_element_type=F32)
    lg = a[0:N_EXPERTS] + a[16:16 + N_EXPERTS] + b[0:N_EXPERTS]
    row = lax.broadcasted_iota(jnp.int32, lg.shape, 0).astype(F32)
    m1 = jnp.max(lg, axis=0, keepdims=True)
    i1 = jnp.min(jnp.where(lg == m1, row, float(N_EXPERTS)), axis=0, keepdims=True)
    oh1 = row == i1
    lg2 = jnp.where(oh1, NEG, lg)
    m2 = jnp.max(lg2, axis=0, keepdims=True)
    i2 = jnp.min(jnp.where(lg2 == m2, row, float(N_EXPERTS)), axis=0, keepdims=True)
    oh2 = row == i2
    e = jnp.exp(m2 - m1)
    g1 = 1.0 / (1.0 + e)
    g2 = e / (1.0 + e)
    oh = jnp.where(oh1 | oh2, 1.0, 0.0)
    oh16 = jnp.concatenate([oh, jnp.zeros_like(oh)], axis=0).astype(BF16)
    before = carry_ref[:, 0:1] + jnp.dot(oh16, tri_ref[...], preferred_element_type=F32)[0:N_EXPERTS]
    r1 = jnp.sum(jnp.where(oh1, before, 0.0), axis=0, keepdims=True)
    r2 = jnp.sum(jnp.where(oh2, before, 0.0), axis=0, keepdims=True)
    total = before[:, tm - 1:tm] + oh[:, tm - 1:tm]
    carry_ref[...] = jnp.broadcast_to(total, carry_ref.shape)
    cnt_ref[...] = jnp.broadcast_to(total, cnt_ref.shape)
    rt = jnp.zeros_like(lg)
    for r, val in enumerate((i1, i2, r1, r2, g1, g2)):
        rt = jnp.where(row == r, val, rt)
    routet_ref[...] = rt
    route_ref[...] = jnp.concatenate([rt, jnp.zeros((LANE - N_EXPERTS, tm), F32)], axis=0).T


def _out_proj(ret_o, fox_o, mq, mkt, mvs, bd, g_mem, w_out, x, ffn_g, layer, seq, dense=None, router=None):
    T = x.shape[0]
    tm = TOK_TILE
    tps = seq // tm
    row = lambda i: (i, 0)
    fix = lambda i: (0, 0)
    full = lambda a: pl.BlockSpec(a.shape, fix)
    M4 = mkt.shape[-1]
    in_specs = [pl.BlockSpec((tm, 512), row), pl.BlockSpec((tm, 512), row), pl.BlockSpec((tm, MQ_SLAB), row),
                pl.BlockSpec((1, 1, MQ_SLAB, M4), lambda i: (layer, i // tps, 0, 0)),
                pl.BlockSpec((1, 1, M4, 256), lambda i: (layer, i // tps, 0, 0)),
                full(bd), full(g_mem), full(w_out), pl.BlockSpec((tm, D_MODEL), row), full(ffn_g)]
    args = [ret_o, fox_o, mq, mkt, mvs, bd, g_mem, w_out, x, ffn_g]
    params = pltpu.CompilerParams(dimension_semantics=("arbitrary",), vmem_limit_bytes=VMEM_LIMIT)
    if router is None:
        wg, wu, wd, j = dense
        once = pl.Buffered(1)
        w_specs = [pl.BlockSpec((1,) + w.shape[1:], lambda i: (j, 0, 0), pipeline_mode=once) for w in (wg, wu, wd)]
        return pl.pallas_call(
            _out_proj_dense_kernel, grid=(T // tm,), in_specs=in_specs + w_specs,
            out_specs=pl.BlockSpec((tm, D_MODEL), row),
            out_shape=jax.ShapeDtypeStruct((T, D_MODEL), F32),
            compiler_params=params, name="out_proj_dense",
        )(*args, wg, wu, wd)
    rw, tri = router
    return pl.pallas_call(
        _out_proj_moe_kernel, grid=(T // tm,), in_specs=in_specs + [full(rw), full(tri)],
        out_specs=[pl.BlockSpec((tm, D_MODEL), row), pl.BlockSpec((tm * ROW_TILE, LANE), row),
                   pl.BlockSpec((N_EXPERTS, tm), lambda i: (0, i)),
                   pl.BlockSpec((tm, LANE), row), pl.BlockSpec((8, LANE), fix)],
        out_shape=[jax.ShapeDtypeStruct((T, D_MODEL), F32), jax.ShapeDtypeStruct((T * ROW_TILE, LANE), F32),
                   jax.ShapeDtypeStruct((N_EXPERTS, T), F32),
                   jax.ShapeDtypeStruct((T, LANE), F32), jax.ShapeDtypeStruct((8, LANE), F32)],
        scratch_shapes=[pltpu.VMEM((8, LANE), F32)],
        compiler_params=params, name="out_proj_moe",
    )(*args, rw, tri)


def _swiglu(x, wg_ref, wu_ref, wd_ref, y):
    for lo, hi in FF_CHUNKS:
        cols = slice(lo, hi)
        g = jnp.dot(x, wg_ref[:, cols], preferred_element_type=F32)
        u = jnp.dot(x, wu_ref[:, cols], preferred_element_type=F32)
        a = (_silu(g) * u).astype(BF16)
        part = jnp.dot(a, wd_ref[cols, :], preferred_element_type=F32)
        y = part if y is None else y + part
    return y


def _to_row_tiles(ref, x):
    for s in range(ROW_TILE):
        ref[pl.ds(s, x.shape[0], stride=ROW_TILE), :] = x[:, s * LANE:(s + 1) * LANE]


def _from_row_tiles(ref, n):
    return jnp.concatenate([ref[pl.ds(s, n, stride=ROW_TILE), :] for s in range(ROW_TILE)], axis=1)


def _row_copy(src, s, dst, d, sem):
    return pltpu.make_async_copy(src.at[pl.ds(pl.multiple_of(s * ROW_TILE, ROW_TILE), ROW_TILE)],
                                 dst.at[pl.ds(pl.multiple_of(d * ROW_TILE, ROW_TILE), ROW_TILE)], sem)


def _dispatch_kernel(tc, tm, d1_ref, d2_ref, lo_ref, hi_ref, nu_ref, h_ref, xs_ref, zero_ref, sem):
    i = pl.program_id(0)
    n_tiles = xs_ref.shape[0] // (tm * ROW_TILE)

    @pl.when(i == 0)
    def _():
        zero_ref[...] = jnp.zeros_like(zero_ref)
        for e in range(N_EXPERTS):
            def fill(r, carry):
                _row_copy(zero_ref, 0, xs_ref, r, sem.at[1]).start()
                return carry

            def drain(r, carry):
                _row_copy(zero_ref, 0, xs_ref, 0, sem.at[1]).wait()
                return carry

            lax.fori_loop(lo_ref[e], hi_ref[e], fill, 0)
            lax.fori_loop(lo_ref[e], hi_ref[e], drain, 0)

        def tile_copy(t):
            start = pl.multiple_of(t * (tm * ROW_TILE), tm * ROW_TILE)
            return pltpu.make_async_copy(zero_ref, xs_ref.at[pl.ds(start, tm * ROW_TILE)], sem.at[1])

        def fill_tile(t, carry):
            tile_copy(t).start()
            return carry

        def drain_tile(t, carry):
            tile_copy(t).wait()
            return carry

        lax.fori_loop(nu_ref[0], n_tiles, fill_tile, 0)
        lax.fori_loop(nu_ref[0], n_tiles, drain_tile, 0)

    base = i * tc

    def issue(r, carry):
        t = base + r
        _row_copy(h_ref, r, xs_ref, d1_ref[t], sem.at[0]).start()
        _row_copy(h_ref, r, xs_ref, d2_ref[t], sem.at[0]).start()
        return carry

    lax.fori_loop(0, tc, issue, 0, unroll=8)
    for _ in range(2):
        pltpu.make_async_copy(h_ref, xs_ref.at[pl.ds(0, tc * ROW_TILE)], sem.at[0]).wait()


def _dispatch(h, d1, d2, pad_lo, pad_hi, n_used, n_tiles):
    tc = TOK_TILE
    tm = MOE_TILE
    T = d1.shape[0]
    return pl.pallas_call(
        functools.partial(_dispatch_kernel, tc, tm),
        grid_spec=pltpu.PrefetchScalarGridSpec(
            num_scalar_prefetch=5, grid=(T // tc,),
            in_specs=[pl.BlockSpec((tc * ROW_TILE, LANE), lambda i, *_: (i, 0))],
            out_specs=pl.BlockSpec(memory_space=pl.ANY),
            scratch_shapes=[pltpu.VMEM((tm * ROW_TILE, LANE), F32), pltpu.SemaphoreType.DMA((2,))]),
        out_shape=jax.ShapeDtypeStruct((n_tiles * tm * ROW_TILE, LANE), F32),
        compiler_params=pltpu.CompilerParams(dimension_semantics=("arbitrary",), has_side_effects=True,
                                             vmem_limit_bytes=VMEM_LIMIT),
        name="moe_dispatch",
    )(d1, d2, pad_lo, pad_hi, n_used, h)


def _moe_ffn_kernel(te_ref, nu_ref, x_ref, wg_ref, wu_ref, wd_ref, o_ref):
    i = pl.program_id(0)
    tm = x_ref.shape[0] // ROW_TILE

    @pl.when(i < nu_ref[0])
    def _():
        x = _from_row_tiles(x_ref, tm).astype(BF16)
        _to_row_tiles(o_ref, _swiglu(x, wg_ref.at[0, 0], wu_ref.at[0, 0], wd_ref.at[0, 0], None))

    @pl.when(i >= nu_ref[0])
    def _():
        o_ref[...] = jnp.zeros_like(o_ref)


def _moe_ffn(xs, tile_expert, n_used, wg, wu, wd, layer):
    tm = MOE_TILE
    n_tiles = tile_expert.shape[0]
    w_block = lambda i, te, nu: (layer, te[i], 0, 0)
    tile = lambda i, te, nu: (jnp.minimum(i, nu[0] - 1), 0)
    return pl.pallas_call(
        _moe_ffn_kernel,
        grid_spec=pltpu.PrefetchScalarGridSpec(
            num_scalar_prefetch=2, grid=(n_tiles,),
            in_specs=[pl.BlockSpec((tm * ROW_TILE, LANE), tile),
                      pl.BlockSpec((1, 1, D_MODEL, D_FF), w_block),
                      pl.BlockSpec((1, 1, D_MODEL, D_FF), w_block),
                      pl.BlockSpec((1, 1, D_FF, D_MODEL), w_block)],
            out_specs=pl.BlockSpec((tm * ROW_TILE, LANE), lambda i, te, nu: (i, 0))),
        out_shape=jax.ShapeDtypeStruct((n_tiles * tm * ROW_TILE, LANE), F32),
        compiler_params=pltpu.CompilerParams(dimension_semantics=("arbitrary",),
                                             vmem_limit_bytes=VMEM_LIMIT),
        name="moe_ffn",
    )(tile_expert, n_used, xs, wg, wu, wd)


def _combine_kernel(tc, d1_ref, d2_ref, o_hbm, x_ref, route_ref, out_ref, b1_ref, b2_ref, sem):
    base = pl.program_id(0) * tc

    def issue(r, carry):
        t = base + r
        _row_copy(o_hbm, d1_ref[t], b1_ref, r, sem.at[0]).start(priority=0)
        _row_copy(o_hbm, d2_ref[t], b2_ref, r, sem.at[1]).start(priority=1)
        return carry

    lax.fori_loop(0, tc, issue, 0, unroll=8)
    pltpu.make_async_copy(o_hbm.at[pl.ds(0, tc * ROW_TILE)], b1_ref, sem.at[0]).wait()
    pltpu.make_async_copy(o_hbm.at[pl.ds(0, tc * ROW_TILE)], b2_ref, sem.at[1]).wait()
    route = route_ref[...]
    g1 = route[:, 4:5]
    g2 = route[:, 5:6]
    out_ref[...] = x_ref[...] + g1 * _from_row_tiles(b1_ref, tc) + g2 * _from_row_tiles(b2_ref, tc)


def _combine(o, d1, d2, x, route):
    T = x.shape[0]
    tc = 256
    return pl.pallas_call(
        functools.partial(_combine_kernel, tc),
        grid_spec=pltpu.PrefetchScalarGridSpec(
            num_scalar_prefetch=2, grid=(T // tc,),
            in_specs=[pl.BlockSpec(memory_space=pl.ANY),
                      pl.BlockSpec((tc, D_MODEL), lambda i, a, b: (i, 0)),
                      pl.BlockSpec((tc, LANE), lambda i, a, b: (i, 0))],
            out_specs=pl.BlockSpec((tc, D_MODEL), lambda i, a, b: (i, 0)),
            scratch_shapes=[pltpu.VMEM((tc * ROW_TILE, LANE), F32), pltpu.VMEM((tc * ROW_TILE, LANE), F32),
                            pltpu.SemaphoreType.DMA((2,))]),
        out_shape=jax.ShapeDtypeStruct((T, D_MODEL), F32),
        compiler_params=pltpu.CompilerParams(dimension_semantics=("arbitrary",),
                                             vmem_limit_bytes=VMEM_LIMIT),
        name="moe_combine",
    )(d1, d2, o, x, route)


def _moe_layer(h, xm, routet, route, counts, wg, wu, wd, layer):
    T = xm.shape[0]
    tm = MOE_TILE
    n_tiles = (2 * T) // tm + N_EXPERTS
    e1 = routet[0].astype(jnp.int32)
    e2 = routet[1].astype(jnp.int32)
    r1 = routet[2].astype(jnp.int32)
    r2 = routet[3].astype(jnp.int32)
    cnt = counts[:, 0].astype(jnp.int32)
    tiles = (cnt + tm - 1) // tm
    tile_end = jnp.cumsum(tiles)
    offs = (tile_end - tiles) * tm
    d1 = offs[e1] + r1
    d2 = offs[e2] + r2
    n_used = tile_end[-1:]
    tile_id = jnp.arange(n_tiles, dtype=jnp.int32)
    tile_expert = jnp.sum(tile_id[:, None] >= tile_end[None, :], axis=1).astype(jnp.int32)
    tile_expert = jnp.minimum(tile_expert, tile_expert[jnp.maximum(n_used[0] - 1, 0)])
    n_used = n_used.astype(jnp.int32)
    xs = _dispatch(h, d1, d2, offs + cnt, offs + tiles * tm, n_used, n_tiles)
    o = _moe_ffn(xs, tile_expert, n_used, wg, wu, wd, layer)
    return _combine(o, d1, d2, xm, route)


def _constants(seq):
    tm = TOK_TILE
    r = np.arange(tm)
    tri_upper = (r[:, None] <= r[None, :]).astype(np.float32)
    tri_strict = (r[:, None] < r[None, :]).astype(np.float32)
    d = np.arange(256)
    bd = (d[:, None] // 64 == d[None, :] // 64).astype(np.float32) / 64.0
    m = np.arange(MQ_SLAB) - MQ_OFF
    valid = (m >= 0) & (m < MEM_HEADS * MEM_DH)
    bdm = ((m[:, None] // MEM_DH == m[None, :] // MEM_DH) & valid[:, None] & valid[None, :]).astype(np.float32) / MEM_DH
    shift = (d[:, None] + MQ_OFF == np.arange(MQ_SLAB)[None, :]).astype(np.float32)
    in_consts = (jnp.asarray(tri_upper, BF16), jnp.asarray(bd, BF16), jnp.asarray(bdm, BF16))

    half = RET_DK // 2
    pos = jnp.arange(seq, dtype=F32)
    inv_freq = ROPE_BASE ** (-jnp.arange(half, dtype=F32) / half)
    ang = pos[:, None] * inv_freq[None, :]
    cos = jnp.tile(jnp.cos(ang), (1, LANE // half))
    sin = jnp.tile(jnp.concatenate([-jnp.sin(ang), jnp.sin(ang)], axis=1), (1, LANE // RET_DK))

    C = RET_CHUNK
    hh = jnp.arange(RET_HEADS, dtype=F32)
    log_g = jnp.log(1.0 - 2.0 ** (-5.0 - hh))
    idx = jnp.arange(C, dtype=F32)
    diff = idx[:, None] - idx[None, :]
    dmat = jnp.where(diff >= 0, jnp.exp(log_g[:, None, None] * jnp.maximum(diff, 0.0)), 0.0)
    dstack = dmat.reshape(RET_HEADS * C, C)
    k_head = np.arange(RET_HEADS * RET_DK) // RET_DK
    v_head = np.arange(RET_HEADS * RET_DV) // RET_DV
    qd = jnp.exp(log_g[k_head][None, :] * (idx[:, None] + 1.0))
    kd = jnp.exp(log_g[k_head][None, :] * (C - 1.0 - idx[:, None]))
    cd = jnp.exp(log_g[v_head] * C)[None, :]
    bm = jnp.asarray((k_head[:, None] == v_head[None, :]).astype(np.float32))
    hm = jnp.asarray((np.arange(RET_HEADS)[:, None] == k_head[None, :]).astype(np.float32), BF16)
    ret_tables = (hm, dstack, qd, kd, cd, bm)
    return in_consts, cos, sin, ret_tables, jnp.asarray(tri_strict, BF16), jnp.asarray(shift, BF16)


def _cast_kernel(w_ref, o_ref):
    o_ref[...] = w_ref[...].astype(BF16)


def _cast_bf16(w):
    n_l, n_e, k, n = w.shape
    kb = TOK_TILE
    spec = pl.BlockSpec((1, 1, kb, n), lambda i, j: (i // n_e, i % n_e, j, 0))
    return pl.pallas_call(
        _cast_kernel, grid=(n_l * n_e, k // kb), in_specs=[spec], out_specs=spec,
        out_shape=jax.ShapeDtypeStruct(w.shape, BF16),
        compiler_params=pltpu.CompilerParams(dimension_semantics=("arbitrary", "arbitrary"),
                                             vmem_limit_bytes=VMEM_LIMIT),
        name="cast_bf16",
    )(w)


def _pad_heads(g):
    depth = g.shape[0]
    g = g.reshape(depth, FOX_HEADS, FOX_DH)
    return jnp.pad(g, ((0, 0), (0, 0), (0, LANE - FOX_DH))).reshape(depth, 1, FOX_HEADS * LANE)


def kernel(x, mem, attn_norm_g, w_in, forget_bias, fox_q_norm_g, fox_k_norm_g, mem_q_norm_g, mem_k_norm_g,
           mem_norm_g, w_mem_kv, out_norm_g, w_out, ffn_norm_g, dense_w_gate, dense_w_up, dense_w_down,
           router_w, expert_w_gate, expert_w_up, expert_w_down):
    B, S, D = x.shape
    depth = w_in.shape[0]
    T = B * S
    in_consts, cos, sin, ret_tables, tri_strict, shift = _constants(S)
    bd = in_consts[1]

    w_in_p = jnp.pad(w_in, ((0, 0), (0, 0), (0, P_IN - w_in.shape[-1]))).astype(BF16)
    fqg = jnp.tile(fox_q_norm_g, (1, FOX_HEADS))[:, None, :]
    fkg = jnp.tile(fox_k_norm_g, (1, FOX_HEADS))[:, None, :]
    fb = jnp.pad(forget_bias, ((0, 0), (0, 8 - FOX_HEADS)))[:, :, None]
    mqg = jnp.tile(mem_q_norm_g, (1, MEM_HEADS))
    mqg = jnp.pad(mqg, ((0, 0), (MQ_OFF, MQ_SLAB - MQ_OFF - mqg.shape[-1])))[:, None, :]
    mkg = jnp.tile(mem_k_norm_g, (1, MEM_HEADS))[:, None, :]
    n_ret = RET_HEADS * RET_DV
    n_fox = FOX_HEADS * FOX_DH
    g_ret = out_norm_g[:, None, :n_ret]
    g_fox = _pad_heads(out_norm_g[:, n_ret:n_ret + n_fox])
    g_mem = out_norm_g[:, None, n_ret + n_fox:]
    w_fox = w_out[:, n_ret:n_ret + n_fox].reshape(depth, FOX_HEADS, FOX_DH, D)
    w_fox = jnp.pad(w_fox, ((0, 0), (0, 0), (0, LANE - FOX_DH), (0, 0))).reshape(depth, FOX_HEADS * LANE, D)
    w_out_p = jnp.concatenate([w_out[:, :n_ret], w_fox, w_out[:, n_ret + n_fox:]], axis=1).astype(BF16)
    dwg, dwu, dwd = dense_w_gate.astype(BF16), dense_w_up.astype(BF16), dense_w_down.astype(BF16)
    ewg, ewu, ewd = _cast_bf16(expert_w_gate), _cast_bf16(expert_w_up), expert_w_down.astype(BF16)
    rwt = jnp.swapaxes(router_w, 1, 2)
    rwt_hi = rwt.astype(BF16)
    rwt_lo = (rwt - rwt_hi.astype(F32)).astype(BF16)
    pad16 = lambda w: jnp.pad(w, ((0, 0), (0, 16 - N_EXPERTS), (0, 0)))
    rw = jnp.concatenate([pad16(rwt_hi), pad16(rwt_lo)], axis=1)

    mkt, mvs = _mem_prep(mem, mem_norm_g[:, None, :], w_mem_kv.astype(BF16), mkg, bd, shift)

    xt = x.reshape(T, D)
    for l in range(depth):
        rq, rk, rv, rg, fqt, fka, fvt, mq = _in_proj(
            xt, attn_norm_g[l][None], w_in_p[l], cos, sin, fqg[l], fkg[l], mqg[l], fb[l], in_consts, S)
        ret_o = _retention(rq, rk, rv, rg, ret_tables, g_ret[l], B, S)
        fox_o = _fox(fqt, fka, fvt, g_fox[l], B, S)
        j = l // 2
        if l % 2 == 0:
            xt = _out_proj(ret_o, fox_o, mq, mkt, mvs, bd, g_mem[l], w_out_p[l], xt, ffn_norm_g[l][None], l, S,
                           dense=(dwg, dwu, dwd, j))
        else:
            xm, h2, routet, route, counts = _out_proj(ret_o, fox_o, mq, mkt, mvs, bd, g_mem[l], w_out_p[l], xt,
                                                      ffn_norm_g[l][None], l, S, router=(rw[j], tri_strict))
            xt = _moe_layer(h2, xm, routet, route, counts, ewg, ewu, ewd, j)
    return xt.reshape(B, S, D)
```

```python
import functools

import numpy as np
import jax
import jax.numpy as jnp
from jax import lax
from jax.experimental import pallas as pl
from jax.experimental.pallas import tpu as pltpu

F32 = jnp.float32
BF16 = jnp.bfloat16

D_MODEL = 1024
RET_HEADS = 4
RET_DK = 64
RET_DV = 128
FOX_HEADS = 4
FOX_DH = 64
MEM_HEADS = 4
MEM_DH = 64
RET_CHUNK = 128
D_FF = 2752
N_EXPERTS = 8
ROPE_BASE = 10000.0
EPS = 1e-6

LANE = 128
FF_CHUNKS = ((0, 1408), (1408, 2688), (2688, D_FF))
TOK_TILE = 512
FOX_Q_TILE = 1024
FOX_K_TILE = 512
LOG2E = 1.4426950408889634
ROW_TILE = 8
assert D_MODEL == ROW_TILE * LANE
MOE_TILE = 512
VMEM_LIMIT = 56 * 1024 * 1024
NEG = -1e30

RQ, RK, RV, RG = 0, 256, 512, 1024
FQ, FK, FV = 1536, 1792, 2048
MQ_COL = 2304
FF_COL = 2560
MQ_W = MEM_HEADS * MEM_DH
P_IN = FF_COL + LANE
N_PIECE = 3
V_ONE = 64


def _split3(x):
    hi = x.astype(BF16)
    r1 = x - hi.astype(F32)
    mid = r1.astype(BF16)
    lo = (r1 - mid.astype(F32)).astype(BF16)
    return hi, mid, lo


def _group_mean_sq(x, bd):
    sq = x * x
    hi = sq.astype(BF16)
    lo = (sq - hi.astype(F32)).astype(BF16)
    return (jnp.dot(hi, bd, preferred_element_type=F32)
            + jnp.dot(lo, bd, preferred_element_type=F32))


def _silu(x):
    return x / (1.0 + jnp.exp(-x))


def _in_proj_kernel(tiles_per_seq, x_ref, g_ref, w_ref, cos_ref, sin_ref, fqg_ref, fkg_ref, mqg_ref,
                    fb_ref, tri_ref, bd_ref,
                    rq_ref, rk_ref, rv_ref, rg_ref, fqt_ref, fka_ref, fvt_ref, mq_ref, carry_ref):
    i = pl.program_id(0)

    @pl.when(i % tiles_per_seq == 0)
    def _():
        carry_ref[...] = jnp.zeros_like(carry_ref)

    x = x_ref[...]
    ms = jnp.mean(x * x, axis=-1, keepdims=True)
    h = (x * lax.rsqrt(ms + EPS) * g_ref[...]).astype(BF16)

    def proj(a, b):
        return jnp.dot(h, w_ref[:, a:b], preferred_element_type=F32)

    tm = x.shape[0]
    cos = cos_ref[...]
    sin = sin_ref[...]
    first = (lax.broadcasted_iota(jnp.int32, (tm, LANE), 1) % RET_DK) < RET_DK // 2
    for base, ref, scale in ((RQ, rq_ref, RET_DK ** -0.5), (RK, rk_ref, 1.0)):
        for a in (0, LANE):
            t = proj(base + a, base + a + LANE)
            partner = jnp.where(first, pltpu.roll(t, LANE - RET_DK // 2, 1), pltpu.roll(t, RET_DK // 2, 1))
            ref[:, a:a + LANE] = ((t * cos + partner * sin) * scale).astype(BF16)
    rv_ref[...] = proj(RV, RV + 512).astype(BF16)
    rg_ref[...] = proj(RG, RG + 512).astype(BF16)

    bd = bd_ref[...]
    mq = proj(MQ_COL, MQ_COL + MQ_W)
    mq = mq * lax.rsqrt(_group_mean_sq(mq, bd) + EPS) * mqg_ref[...] * (MEM_DH ** -0.5)
    mq_ref[...] = mq.astype(BF16)

    z = proj(FF_COL, FF_COL + LANE).T[0:8, :] + fb_ref[...]
    logf = jnp.minimum(z, 0.0) - jnp.log(1.0 + jnp.exp(-jnp.abs(z)))
    hrow = lax.broadcasted_iota(jnp.int32, logf.shape, 0)
    logf = jnp.where(hrow < FOX_HEADS, logf, 0.0)
    tri = tri_ref[...]
    c = carry_ref[:, 0:1]
    for piece in _split3(logf):
        c = c + jnp.dot(piece, tri, preferred_element_type=F32)
    carry_ref[...] = jnp.broadcast_to(c[:, tm - 1:tm], carry_ref.shape)

    pieces = [p.astype(F32) for p in _split3(c * LOG2E)]
    fq = proj(FQ, FQ + 256)
    fk = proj(FK, FK + 256)
    qt = (fq * lax.rsqrt(_group_mean_sq(fq, bd) + EPS) * fqg_ref[...] * (FOX_DH ** -0.5 * LOG2E)).T
    kt = (fk * lax.rsqrt(_group_mean_sq(fk, bd) + EPS) * fkg_ref[...]).T
    vt = proj(FV, FV + 256).T
    erow = lax.broadcasted_iota(jnp.int32, (8, tm), 0)
    zeros = jnp.zeros((LANE - FOX_DH - 8, tm), F32)
    ones_row = jnp.where(erow == V_ONE - FOX_DH, 1.0, 0.0)
    for hh in range(FOX_HEADS):
        feat = slice(hh * FOX_DH, (hh + 1) * FOX_DH)
        rows = slice(hh * LANE, (hh + 1) * LANE)
        eq = jnp.where(erow < 2 * N_PIECE, 1.0, 0.0)
        ek = eq
        for p in range(N_PIECE):
            c_row = pieces[p][hh:hh + 1, :]
            eq = jnp.where(erow == p, c_row, eq)
            ek = jnp.where(erow == N_PIECE + p, -c_row, ek)
        fqt_ref[rows, :] = jnp.concatenate([qt[feat], eq, zeros], axis=0).astype(BF16)
        fka_ref[:, rows] = jnp.concatenate([kt[feat], ek, zeros], axis=0).T.astype(BF16)
        fvt_ref[rows, :] = jnp.concatenate([vt[feat], ones_row, zeros], axis=0).astype(BF16)


def _in_proj(x, g, w, cos, sin, fqg, fkg, mqg, fb, consts, seq):
    T = x.shape[0]
    tm = TOK_TILE
    nt = T // tm
    tps = seq // tm
    row = lambda i: (i, 0)
    fix = lambda i: (0, 0)
    full = lambda a: pl.BlockSpec(a.shape, fix)
    tri, bd = consts
    widths = (256, 256, 512, 512, -512, 512, -512, MQ_W)
    outs = [jax.ShapeDtypeStruct((T, n) if n > 0 else (-n, T), BF16) for n in widths]
    out_specs = [pl.BlockSpec((tm, n), row) if n > 0 else pl.BlockSpec((-n, tm), lambda i: (0, i))
                 for n in widths]
    return pl.pallas_call(
        functools.partial(_in_proj_kernel, tps),
        grid=(nt,),
        in_specs=[pl.BlockSpec((tm, D_MODEL), row), full(g), full(w),
                  pl.BlockSpec((tm, LANE), lambda i: (i % tps, 0)),
                  pl.BlockSpec((tm, LANE), lambda i: (i % tps, 0)),
                  full(fqg), full(fkg), full(mqg), full(fb), full(tri), full(bd)],
        out_specs=out_specs,
        out_shape=outs,
        scratch_shapes=[pltpu.VMEM((8, LANE), F32)],
        compiler_params=pltpu.CompilerParams(dimension_semantics=("arbitrary",),
                                             vmem_limit_bytes=VMEM_LIMIT),
        name="in_proj",
    )(x, g, w, cos, sin, fqg, fkg, mqg, fb, tri, bd)


def _retention_kernel(chunks, rq_ref, rk_ref, rv_ref, rg_ref, hm_ref, dst_ref, qd_ref, kd_ref, cd_ref,
                      bm_ref, g_ref, o_ref, state_ref):
    @pl.when(pl.program_id(1) == 0)
    def _():
        state_ref[...] = jnp.zeros_like(state_ref)

    C = RET_CHUNK
    for ci in range(chunks):
        rows = slice(ci * C, (ci + 1) * C)
        qc = rq_ref[rows, :]
        kc = rk_ref[rows, :]
        vc = rv_ref[rows, :]
        qs = jnp.concatenate([qc * hm_ref[hh:hh + 1, :] for hh in range(RET_HEADS)], axis=0)
        sc = lax.dot_general(qs, kc, (((1,), (1,)), ((), ())), preferred_element_type=F32)
        sc = (sc * dst_ref[...]).astype(BF16)
        state = state_ref[...]
        qdec = (qc.astype(F32) * qd_ref[...]).astype(BF16)
        cross = jnp.dot(qdec, state.astype(BF16), preferred_element_type=F32)
        kdec = (kc.astype(F32) * kd_ref[...]).astype(BF16)
        kv = lax.dot_general(kdec, vc, (((0,), (0,)), ((), ())), preferred_element_type=F32)
        state_ref[...] = state * cd_ref[...] + kv * bm_ref[...]
        for hh in range(RET_HEADS):
            cols = slice(hh * RET_DV, (hh + 1) * RET_DV)
            o = jnp.dot(sc[hh * C:(hh + 1) * C, :], vc[:, cols], preferred_element_type=F32) + cross[:, cols]
            msq = jnp.mean(o * o, axis=-1, keepdims=True)
            y = o * lax.rsqrt(msq + EPS) * g_ref[:, cols]
            o_ref[rows, cols] = (y * _silu(rg_ref[rows, cols].astype(F32))).astype(BF16)


def _retention(rq, rk, rv, rg, tables, g_ret, batch, seq):
    T = rq.shape[0]
    tm = TOK_TILE
    tps = seq // tm
    row = lambda b, i: (b * tps + i, 0)
    fix = lambda b, i: (0, 0)
    full = lambda a: pl.BlockSpec(a.shape, fix)
    return pl.pallas_call(
        functools.partial(_retention_kernel, tm // RET_CHUNK),
        grid=(batch, tps),
        in_specs=[pl.BlockSpec((tm, 256), row), pl.BlockSpec((tm, 256), row),
                  pl.BlockSpec((tm, 512), row), pl.BlockSpec((tm, 512), row)]
                 + [full(t) for t in tables] + [full(g_ret)],
        out_specs=pl.BlockSpec((tm, 512), row),
        out_shape=jax.ShapeDtypeStruct((T, 512), BF16),
        scratch_shapes=[pltpu.VMEM((RET_HEADS * RET_DK, RET_HEADS * RET_DV), F32)],
        compiler_params=pltpu.CompilerParams(dimension_semantics=("arbitrary", "arbitrary"),
                                             vmem_limit_bytes=VMEM_LIMIT),
        name="retention",
    )(rq, rk, rv, rg, *tables, g_ret)


def _fox_kernel(tq, tk, qt_ref, k_ref, vt_ref, g_ref, o_ref, m_ref, acc_ref, s0_ref, s1_ref):
    i = pl.program_id(2)
    qt = qt_ref[...]
    m_ref[...] = jnp.full_like(m_ref, NEG)
    acc_ref[...] = jnp.zeros_like(acc_ref)
    assert tq == 2 * tk
    s_refs = (s0_ref, s1_ref)

    def scores(j, slot):
        start = pl.multiple_of(j * tk, tk)
        s_refs[slot][...] = jnp.dot(k_ref[pl.ds(start, tk), :], qt, preferred_element_type=F32)

    def update(j, slot, diag):
        start = pl.multiple_of(j * tk, tk)
        st = s_refs[slot][...]
        if diag is not None:
            key = lax.broadcasted_iota(jnp.int32, st.shape, 0) + diag * tk
            qry = lax.broadcasted_iota(jnp.int32, st.shape, 1)
            st = jnp.where(key <= qry, st, NEG)
        m_prev = m_ref[...]
        m_new = jnp.maximum(m_prev, jnp.max(st, axis=0, keepdims=True))
        alpha = jnp.exp2(m_prev - m_new)
        p = jnp.exp2(st - m_new).astype(BF16)
        vt = vt_ref[:, pl.ds(start, tk)]
        acc_ref[...] = alpha * acc_ref[...] + jnp.dot(vt, p, preferred_element_type=F32)
        m_ref[...] = m_new

    scores(0, 0)

    def body(t, carry):
        scores(2 * t + 1, 1)
        update(2 * t, 0, None)
        scores(2 * t + 2, 0)
        update(2 * t + 1, 1, None)
        return carry

    lax.fori_loop(0, i, body, 0)
    scores(2 * i + 1, 1)
    update(2 * i, 0, 0)
    update(2 * i + 1, 1, 1)
    acc = acc_ref[...]
    row = lax.broadcasted_iota(jnp.int32, acc.shape, 0)
    o = jnp.where(row < FOX_DH, acc / acc[V_ONE:V_ONE + 1, :], 0.0)
    msq = jnp.sum(o * o, axis=0, keepdims=True) * (1.0 / FOX_DH)
    o_ref[...] = ((o * lax.rsqrt(msq + EPS)).T * g_ref[...]).astype(BF16)


def _fox(fqt, fka, fvt, g_fox, batch, seq):
    T = fka.shape[0]
    tq = FOX_Q_TILE
    tk = FOX_K_TILE
    nq = seq // tq
    return pl.pallas_call(
        functools.partial(_fox_kernel, tq, tk),
        grid=(batch, FOX_HEADS, nq),
        in_specs=[pl.BlockSpec((LANE, tq), lambda b, h, i: (h, b * nq + i)),
                  pl.BlockSpec((seq, LANE), lambda b, h, i: (b, h)),
                  pl.BlockSpec((LANE, seq), lambda b, h, i: (h, b)),
                  pl.BlockSpec((1, LANE), lambda b, h, i: (0, h))],
        out_specs=pl.BlockSpec((tq, LANE), lambda b, h, i: (b * nq + i, h)),
        out_shape=jax.ShapeDtypeStruct((T, FOX_HEADS * LANE), BF16),
        scratch_shapes=[pltpu.VMEM((1, tq), F32), pltpu.VMEM((LANE, tq), F32),
                        pltpu.VMEM((tk, tq), F32), pltpu.VMEM((tk, tq), F32)],
        compiler_params=pltpu.CompilerParams(dimension_semantics=("arbitrary", "arbitrary", "arbitrary"),
                                             vmem_limit_bytes=VMEM_LIMIT),
        name="fox",
    )(fqt, fka, fvt, g_fox)


def _mem_prep_kernel(mem_ref, g_ref, w_ref, kg_ref, bd_ref, mkt_ref, mvs_ref):
    x = mem_ref[0]
    ms = jnp.mean(x * x, axis=-1, keepdims=True)
    mn = (x * lax.rsqrt(ms + EPS) * g_ref[0]).astype(BF16)
    mkv = jnp.dot(mn, w_ref[0], preferred_element_type=F32)
    width = MEM_HEADS * MEM_DH
    mk = mkv[:, :width]
    mv = mkv[:, width:]
    mk = mk * lax.rsqrt(_group_mean_sq(mk, bd_ref[...]) + EPS) * kg_ref[0]
    mkt = mk.T
    feat = lax.broadcasted_iota(jnp.int32, mkt.shape, 0) // MEM_DH
    lane_head = lax.broadcasted_iota(jnp.int32, mv.shape, 1) // MEM_DH
    M = x.shape[0]
    for hh in range(MEM_HEADS):
        mkt_ref[0, 0, :, hh * M:(hh + 1) * M] = jnp.where(feat == hh, mkt, 0.0).astype(BF16)
        mvs_ref[0, 0, hh * M:(hh + 1) * M, :] = jnp.where(lane_head == hh, mv, 0.0).astype(BF16)


def _mem_prep(mem, mem_norm_g, w_mem_kv, mem_k_g, bd):
    B, M, _ = mem.shape
    depth = w_mem_kv.shape[0]
    width = MEM_HEADS * MEM_DH
    return pl.pallas_call(
        _mem_prep_kernel,
        grid=(depth, B),
        in_specs=[pl.BlockSpec((1, M, D_MODEL), lambda l, b: (b, 0, 0)),
                  pl.BlockSpec((1, 1, D_MODEL), lambda l, b: (l, 0, 0)),
                  pl.BlockSpec((1, D_MODEL, 2 * width), lambda l, b: (l, 0, 0)),
                  pl.BlockSpec((1, 1, width), lambda l, b: (l, 0, 0)),
                  pl.BlockSpec(bd.shape, lambda l, b: (0, 0))],
        out_specs=[pl.BlockSpec((1, 1, MQ_W, MEM_HEADS * M), lambda l, b: (l, b, 0, 0)),
                   pl.BlockSpec((1, 1, MEM_HEADS * M, width), lambda l, b: (l, b, 0, 0))],
        out_shape=[jax.ShapeDtypeStruct((depth, B, MQ_W, MEM_HEADS * M), BF16),
                   jax.ShapeDtypeStruct((depth, B, MEM_HEADS * M, width), BF16)],
        compiler_params=pltpu.CompilerParams(dimension_semantics=("arbitrary", "arbitrary"),
                                             vmem_limit_bytes=VMEM_LIMIT),
        name="mem_prep",
    )(mem, mem_norm_g, w_mem_kv, mem_k_g, bd)


def _out_proj_body(ret_ref, fox_ref, mq_ref, mkt_ref, mvs_ref, bd_ref, gm_ref, w_ref, x_ref, fg_ref):
    M = mkt_ref.shape[-1] // MEM_HEADS
    s = jnp.dot(mq_ref[...], mkt_ref[0, 0], preferred_element_type=F32)
    mem_o = None
    for hh in range(MEM_HEADS):
        sh = s[:, hh * M:(hh + 1) * M]
        p = jnp.exp(sh - jnp.max(sh, axis=-1, keepdims=True))
        p = (p / jnp.sum(p, axis=-1, keepdims=True)).astype(BF16)
        part = jnp.dot(p, mvs_ref[0, 0, hh * M:(hh + 1) * M, :], preferred_element_type=F32)
        mem_o = part if mem_o is None else mem_o + part
    mem_o = (mem_o * lax.rsqrt(_group_mean_sq(mem_o, bd_ref[...]) + EPS) * gm_ref[...]).astype(BF16)
    y = (jnp.dot(ret_ref[...], w_ref[0:512, :], preferred_element_type=F32)
         + jnp.dot(fox_ref[...], w_ref[512:1024, :], preferred_element_type=F32)
         + jnp.dot(mem_o, w_ref[1024:1280, :], preferred_element_type=F32))
    xm = x_ref[...] + y
    ms = jnp.mean(xm * xm, axis=-1, keepdims=True)
    return xm, xm * lax.rsqrt(ms + EPS) * fg_ref[...]


def _out_proj_dense_kernel(ret_ref, fox_ref, mq_ref, mkt_ref, mvs_ref, bd_ref, gm_ref, w_ref, x_ref, fg_ref,
                           wg_ref, wu_ref, wd_ref, o_ref):
    xm, hn = _out_proj_body(ret_ref, fox_ref, mq_ref, mkt_ref, mvs_ref, bd_ref, gm_ref, w_ref, x_ref, fg_ref)
    o_ref[...] = _swiglu(hn.astype(BF16), wg_ref.at[0], wu_ref.at[0], wd_ref.at[0], xm)


def _out_proj_moe_kernel(ret_ref, fox_ref, mq_ref, mkt_ref, mvs_ref, bd_ref, gm_ref, w_ref, x_ref, fg_ref,
                         rwt_ref, tri_ref, xm_ref, h_ref, routet_ref, route_ref, cnt_ref, carry_ref):
    @pl.when(pl.program_id(0) == 0)
    def _():
        carry_ref[...] = jnp.zeros_like(carry_ref)

    xm, hn = _out_proj_body(ret_ref, fox_ref, mq_ref, mkt_ref, mvs_ref, bd_ref, gm_ref, w_ref, x_ref, fg_ref)
    xm_ref[...] = xm
    _to_row_tiles(h_ref, hn)
    tm = hn.shape[0]
    h_hi = hn.astype(BF16)
    h_lo = (hn - h_hi.astype(F32)).astype(BF16)
    rwt = rwt_ref[...]
    nt = (((1,), (1,)), ((), ()))
    a = lax.dot_general(rwt, h_hi, nt, preferred_element_type=F32)
    b = lax.dot_general(rwt[0:16], h_lo, nt, preferred_element_type=F32)
    lg = a[0:N_EXPERTS] + a[16:16 + N_EXPERTS] + b[0:N_EXPERTS]
    row = lax.broadcasted_iota(jnp.int32, lg.shape, 0).astype(F32)
    m1 = jnp.max(lg, axis=0, keepdims=True)
    i1 = jnp.min(jnp.where(lg == m1, row, float(N_EXPERTS)), axis=0, keepdims=True)
    oh1 = row == i1
    lg2 = jnp.where(oh1, NEG, lg)
    m2 = jnp.max(lg2, axis=0, keepdims=True)
    i2 = jnp.min(jnp.where(lg2 == m2, row, float(N_EXPERTS)), axis=0, keepdims=True)
    oh2 = row == i2
    e = jnp.exp(m2 - m1)
    g1 = 1.0 / (1.0 + e)
    g2 = e / (1.0 + e)
    oh = jnp.where(oh1 | oh2, 1.0, 0.0)
    oh16 = jnp.concatenate([oh, jnp.zeros_like(oh)], axis=0).astype(BF16)
    before = carry_ref[:, 0:1] + jnp.dot(oh16, tri_ref[...], preferred_element_type=F32)[0:N_EXPERTS]
    r1 = jnp.sum(jnp.where(oh1, before, 0.0), axis=0, keepdims=True)
    r2 = jnp.sum(jnp.where(oh2, before, 0.0), axis=0, keepdims=True)
    total = before[:, tm - 1:tm] + oh[:, tm - 1:tm]
    carry_ref[...] = jnp.broadcast_to(total, carry_ref.shape)
    cnt_ref[...] = jnp.broadcast_to(total, cnt_ref.shape)
    rt = jnp.zeros_like(lg)
    for r, val in enumerate((i1, i2, r1, r2, g1, g2)):
        rt = jnp.where(row == r, val, rt)
    routet_ref[...] = rt
    route_ref[...] = jnp.concatenate([rt, jnp.zeros((LANE - N_EXPERTS, tm), F32)], axis=0).T


def _out_proj(ret_o, fox_o, mq, mkt, mvs, bd, g_mem, w_out, x, ffn_g, layer, seq, dense=None, router=None):
    T = x.shape[0]
    tm = TOK_TILE
    tps = seq // tm
    row = lambda i: (i, 0)
    fix = lambda i: (0, 0)
    full = lambda a: pl.BlockSpec(a.shape, fix)
    M4 = mkt.shape[-1]
    in_specs = [pl.BlockSpec((tm, 512), row), pl.BlockSpec((tm, 512), row), pl.BlockSpec((tm, MQ_W), row),
                pl.BlockSpec((1, 1, MQ_W, M4), lambda i: (layer, i // tps, 0, 0)),
                pl.BlockSpec((1, 1, M4, 256), lambda i: (layer, i // tps, 0, 0)),
                full(bd), full(g_mem), full(w_out), pl.BlockSpec((tm, D_MODEL), row), full(ffn_g)]
    args = [ret_o, fox_o, mq, mkt, mvs, bd, g_mem, w_out, x, ffn_g]
    params = pltpu.CompilerParams(dimension_semantics=("arbitrary",), vmem_limit_bytes=VMEM_LIMIT)
    if router is None:
        wg, wu, wd, j = dense
        once = pl.Buffered(1)
        w_specs = [pl.BlockSpec((1,) + w.shape[1:], lambda i: (j, 0, 0), pipeline_mode=once) for w in (wg, wu, wd)]
        return pl.pallas_call(
            _out_proj_dense_kernel, grid=(T // tm,), in_specs=in_specs + w_specs,
            out_specs=pl.BlockSpec((tm, D_MODEL), row),
            out_shape=jax.ShapeDtypeStruct((T, D_MODEL), F32),
            compiler_params=params, name="out_proj_dense",
        )(*args, wg, wu, wd)
    rw, tri = router
    return pl.pallas_call(
        _out_proj_moe_kernel, grid=(T // tm,), in_specs=in_specs + [full(rw), full(tri)],
        out_specs=[pl.BlockSpec((tm, D_MODEL), row), pl.BlockSpec((tm * ROW_TILE, LANE), row),
                   pl.BlockSpec((N_EXPERTS, tm), lambda i: (0, i)),
                   pl.BlockSpec((tm, LANE), row), pl.BlockSpec((8, LANE), fix)],
        out_shape=[jax.ShapeDtypeStruct((T, D_MODEL), F32), jax.ShapeDtypeStruct((T * ROW_TILE, LANE), F32),
                   jax.ShapeDtypeStruct((N_EXPERTS, T), F32),
                   jax.ShapeDtypeStruct((T, LANE), F32), jax.ShapeDtypeStruct((8, LANE), F32)],
        scratch_shapes=[pltpu.VMEM((8, LANE), F32)],
        compiler_params=params, name="out_proj_moe",
    )(*args, rw, tri)


def _swiglu(x, wg_ref, wu_ref, wd_ref, y):
    for lo, hi in FF_CHUNKS:
        cols = slice(lo, hi)
        g = jnp.dot(x, wg_ref[:, cols], preferred_element_type=F32)
        u = jnp.dot(x, wu_ref[:, cols], preferred_element_type=F32)
        a = (_silu(g) * u).astype(BF16)
        part = jnp.dot(a, wd_ref[cols, :], preferred_element_type=F32)
        y = part if y is None else y + part
    return y


def _to_row_tiles(ref, x):
    for s in range(ROW_TILE):
        ref[pl.ds(s, x.shape[0], stride=ROW_TILE), :] = x[:, s * LANE:(s + 1) * LANE]


def _from_row_tiles(ref, n):
    return jnp.concatenate([ref[pl.ds(s, n, stride=ROW_TILE), :] for s in range(ROW_TILE)], axis=1)


def _row_copy(src, s, dst, d, sem):
    return pltpu.make_async_copy(src.at[pl.ds(pl.multiple_of(s * ROW_TILE, ROW_TILE), ROW_TILE)],
                                 dst.at[pl.ds(pl.multiple_of(d * ROW_TILE, ROW_TILE), ROW_TILE)], sem)


def _dispatch_kernel(tc, tm, d1_ref, d2_ref, lo_ref, hi_ref, nu_ref, h_ref, xs_ref, zero_ref, sem):
    i = pl.program_id(0)
    n_tiles = xs_ref.shape[0] // (tm * ROW_TILE)

    @pl.when(i == 0)
    def _():
        zero_ref[...] = jnp.zeros_like(zero_ref)
        for e in range(N_EXPERTS):
            def fill(r, carry):
                _row_copy(zero_ref, 0, xs_ref, r, sem.at[1]).start()
                return carry

            def drain(r, carry):
                _row_copy(zero_ref, 0, xs_ref, 0, sem.at[1]).wait()
                return carry

            lax.fori_loop(lo_ref[e], hi_ref[e], fill, 0)
            lax.fori_loop(lo_ref[e], hi_ref[e], drain, 0)

        def tile_copy(t):
            start = pl.multiple_of(t * (tm * ROW_TILE), tm * ROW_TILE)
            return pltpu.make_async_copy(zero_ref, xs_ref.at[pl.ds(start, tm * ROW_TILE)], sem.at[1])

        def fill_tile(t, carry):
            tile_copy(t).start()
            return carry

        def drain_tile(t, carry):
            tile_copy(t).wait()
            return carry

        lax.fori_loop(nu_ref[0], n_tiles, fill_tile, 0)
        lax.fori_loop(nu_ref[0], n_tiles, drain_tile, 0)

    base = i * tc

    def issue(r, carry):
        t = base + r
        _row_copy(h_ref, r, xs_ref, d1_ref[t], sem.at[0]).start()
        _row_copy(h_ref, r, xs_ref, d2_ref[t], sem.at[0]).start()
        return carry

    lax.fori_loop(0, tc, issue, 0, unroll=8)
    for _ in range(2):
        pltpu.make_async_copy(h_ref, xs_ref.at[pl.ds(0, tc * ROW_TILE)], sem.at[0]).wait()


def _dispatch(h, d1, d2, pad_lo, pad_hi, n_used, n_tiles):
    tc = TOK_TILE
    tm = MOE_TILE
    T = d1.shape[0]
    return pl.pallas_call(
        functools.partial(_dispatch_kernel, tc, tm),
        grid_spec=pltpu.PrefetchScalarGridSpec(
            num_scalar_prefetch=5, grid=(T // tc,),
            in_specs=[pl.BlockSpec((tc * ROW_TILE, LANE), lambda i, *_: (i, 0))],
            out_specs=pl.BlockSpec(memory_space=pl.ANY),
            scratch_shapes=[pltpu.VMEM((tm * ROW_TILE, LANE), F32), pltpu.SemaphoreType.DMA((2,))]),
        out_shape=jax.ShapeDtypeStruct((n_tiles * tm * ROW_TILE, LANE), F32),
        compiler_params=pltpu.CompilerParams(dimension_semantics=("arbitrary",), has_side_effects=True,
                                             vmem_limit_bytes=VMEM_LIMIT),
        name="moe_dispatch",
    )(d1, d2, pad_lo, pad_hi, n_used, h)


def _moe_ffn_kernel(te_ref, nu_ref, x_ref, wg_ref, wu_ref, wd_ref, o_ref):
    i = pl.program_id(0)
    tm = x_ref.shape[0] // ROW_TILE

    @pl.when(i < nu_ref[0])
    def _():
        x = _from_row_tiles(x_ref, tm).astype(BF16)
        _to_row_tiles(o_ref, _swiglu(x, wg_ref.at[0, 0], wu_ref.at[0, 0], wd_ref.at[0, 0], None))

    @pl.when(i >= nu_ref[0])
    def _():
        o_ref[...] = jnp.zeros_like(o_ref)


def _moe_ffn(xs, tile_expert, n_used, wg, wu, wd, layer):
    tm = MOE_TILE
    n_tiles = tile_expert.shape[0]
    w_block = lambda i, te, nu: (layer, te[i], 0, 0)
    tile = lambda i, te, nu: (jnp.minimum(i, nu[0] - 1), 0)
    return pl.pallas_call(
        _moe_ffn_kernel,
        grid_spec=pltpu.PrefetchScalarGridSpec(
            num_scalar_prefetch=2, grid=(n_tiles,),
            in_specs=[pl.BlockSpec((tm * ROW_TILE, LANE), tile),
                      pl.BlockSpec((1, 1, D_MODEL, D_FF), w_block),
                      pl.BlockSpec((1, 1, D_MODEL, D_FF), w_block),
                      pl.BlockSpec((1, 1, D_FF, D_MODEL), w_block)],
            out_specs=pl.BlockSpec((tm * ROW_TILE, LANE), lambda i, te, nu: (i, 0))),
        out_shape=jax.ShapeDtypeStruct((n_tiles * tm * ROW_TILE, LANE), F32),
        compiler_params=pltpu.CompilerParams(dimension_semantics=("arbitrary",),
                                             vmem_limit_bytes=VMEM_LIMIT),
        name="moe_ffn",
    )(tile_expert, n_used, xs, wg, wu, wd)


def _combine_kernel(tc, d1_ref, d2_ref, o_hbm, x_ref, route_ref, out_ref, b1_ref, b2_ref, sem):
    base = pl.program_id(0) * tc

    def issue(r, carry):
        t = base + r
        _row_copy(o_hbm, d1_ref[t], b1_ref, r, sem.at[0]).start(priority=0)
        _row_copy(o_hbm, d2_ref[t], b2_ref, r, sem.at[1]).start(priority=1)
        return carry

    lax.fori_loop(0, tc, issue, 0, unroll=8)
    pltpu.make_async_copy(o_hbm.at[pl.ds(0, tc * ROW_TILE)], b1_ref, sem.at[0]).wait()
    pltpu.make_async_copy(o_hbm.at[pl.ds(0, tc * ROW_TILE)], b2_ref, sem.at[1]).wait()
    route = route_ref[...]
    g1 = route[:, 4:5]
    g2 = route[:, 5:6]
    out_ref[...] = x_ref[...] + g1 * _from_row_tiles(b1_ref, tc) + g2 * _from_row_tiles(b2_ref, tc)


def _combine(o, d1, d2, x, route):
    T = x.shape[0]
    tc = 256
    return pl.pallas_call(
        functools.partial(_combine_kernel, tc),
        grid_spec=pltpu.PrefetchScalarGridSpec(
            num_scalar_prefetch=2, grid=(T // tc,),
            in_specs=[pl.BlockSpec(memory_space=pl.ANY),
                      pl.BlockSpec((tc, D_MODEL), lambda i, a, b: (i, 0)),
                      pl.BlockSpec((tc, LANE), lambda i, a, b: (i, 0))],
            out_specs=pl.BlockSpec((tc, D_MODEL), lambda i, a, b: (i, 0)),
            scratch_shapes=[pltpu.VMEM((tc * ROW_TILE, LANE), F32), pltpu.VMEM((tc * ROW_TILE, LANE), F32),
                            pltpu.SemaphoreType.DMA((2,))]),
        out_shape=jax.ShapeDtypeStruct((T, D_MODEL), F32),
        compiler_params=pltpu.CompilerParams(dimension_semantics=("arbitrary",),
                                             vmem_limit_bytes=VMEM_LIMIT),
        name="moe_combine",
    )(d1, d2, o, x, route)


def _moe_layer(h, xm, routet, route, counts, wg, wu, wd, layer):
    T = xm.shape[0]
    tm = MOE_TILE
    n_tiles = (2 * T) // tm + N_EXPERTS
    e1 = routet[0].astype(jnp.int32)
    e2 = routet[1].astype(jnp.int32)
    r1 = routet[2].astype(jnp.int32)
    r2 = routet[3].astype(jnp.int32)
    cnt = counts[:, 0].astype(jnp.int32)
    tiles = (cnt + tm - 1) // tm
    tile_end = jnp.cumsum(tiles)
    offs = (tile_end - tiles) * tm
    d1 = offs[e1] + r1
    d2 = offs[e2] + r2
    n_used = tile_end[-1:]
    tile_id = jnp.arange(n_tiles, dtype=jnp.int32)
    tile_expert = jnp.sum(tile_id[:, None] >= tile_end[None, :], axis=1).astype(jnp.int32)
    tile_expert = jnp.minimum(tile_expert, tile_expert[jnp.maximum(n_used[0] - 1, 0)])
    n_used = n_used.astype(jnp.int32)
    xs = _dispatch(h, d1, d2, offs + cnt, offs + tiles * tm, n_used, n_tiles)
    o = _moe_ffn(xs, tile_expert, n_used, wg, wu, wd, layer)
    return _combine(o, d1, d2, xm, route)


def _constants(seq):
    tm = TOK_TILE
    r = np.arange(tm)
    tri_upper = (r[:, None] <= r[None, :]).astype(np.float32)
    tri_strict = (r[:, None] < r[None, :]).astype(np.float32)
    d = np.arange(256)
    bd = (d[:, None] // 64 == d[None, :] // 64).astype(np.float32) / 64.0
    in_consts = (jnp.asarray(tri_upper, BF16), jnp.asarray(bd, BF16))

    half = RET_DK // 2
    pos = jnp.arange(seq, dtype=F32)
    inv_freq = ROPE_BASE ** (-jnp.arange(half, dtype=F32) / half)
    ang = pos[:, None] * inv_freq[None, :]
    cos = jnp.tile(jnp.cos(ang), (1, LANE // half))
    sin = jnp.tile(jnp.concatenate([-jnp.sin(ang), jnp.sin(ang)], axis=1), (1, LANE // RET_DK))

    C = RET_CHUNK
    hh = jnp.arange(RET_HEADS, dtype=F32)
    log_g = jnp.log(1.0 - 2.0 ** (-5.0 - hh))
    idx = jnp.arange(C, dtype=F32)
    diff = idx[:, None] - idx[None, :]
    dmat = jnp.where(diff >= 0, jnp.exp(log_g[:, None, None] * jnp.maximum(diff, 0.0)), 0.0)
    dstack = dmat.reshape(RET_HEADS * C, C)
    k_head = np.arange(RET_HEADS * RET_DK) // RET_DK
    v_head = np.arange(RET_HEADS * RET_DV) // RET_DV
    qd = jnp.exp(log_g[k_head][None, :] * (idx[:, None] + 1.0))
    kd = jnp.exp(log_g[k_head][None, :] * (C - 1.0 - idx[:, None]))
    cd = jnp.exp(log_g[v_head] * C)[None, :]
    bm = jnp.asarray((k_head[:, None] == v_head[None, :]).astype(np.float32))
    hm = jnp.asarray((np.arange(RET_HEADS)[:, None] == k_head[None, :]).astype(np.float32), BF16)
    ret_tables = (hm, dstack, qd, kd, cd, bm)
    return in_consts, cos, sin, ret_tables, jnp.asarray(tri_strict, BF16)


def _pad_heads(g):
    depth = g.shape[0]
    g = g.reshape(depth, FOX_HEADS, FOX_DH)
    return jnp.pad(g, ((0, 0), (0, 0), (0, LANE - FOX_DH))).reshape(depth, 1, FOX_HEADS * LANE)


def kernel(x, mem, attn_norm_g, w_in, forget_bias, fox_q_norm_g, fox_k_norm_g, mem_q_norm_g, mem_k_norm_g,
           mem_norm_g, w_mem_kv, out_norm_g, w_out, ffn_norm_g, dense_w_gate, dense_w_up, dense_w_down,
           router_w, expert_w_gate, expert_w_up, expert_w_down):
    B, S, D = x.shape
    depth = w_in.shape[0]
    T = B * S
    in_consts, cos, sin, ret_tables, tri_strict = _constants(S)
    bd = in_consts[1]

    n_in = w_in.shape[-1]
    w_in_p = jnp.concatenate([w_in[:, :, :MQ_COL], w_in[:, :, n_in - MQ_W:], w_in[:, :, MQ_COL:n_in - MQ_W],
                              jnp.zeros((depth, D, P_IN - n_in), w_in.dtype)], axis=-1).astype(BF16)
    fqg = jnp.tile(fox_q_norm_g, (1, FOX_HEADS))[:, None, :]
    fkg = jnp.tile(fox_k_norm_g, (1, FOX_HEADS))[:, None, :]
    fb = jnp.pad(forget_bias, ((0, 0), (0, 8 - FOX_HEADS)))[:, :, None]
    mqg = jnp.tile(mem_q_norm_g, (1, MEM_HEADS))[:, None, :]
    mkg = jnp.tile(mem_k_norm_g, (1, MEM_HEADS))[:, None, :]
    n_ret = RET_HEADS * RET_DV
    n_fox = FOX_HEADS * FOX_DH
    g_ret = out_norm_g[:, None, :n_ret]
    g_fox = _pad_heads(out_norm_g[:, n_ret:n_ret + n_fox])
    g_mem = out_norm_g[:, None, n_ret + n_fox:]
    w_fox = w_out[:, n_ret:n_ret + n_fox].reshape(depth, FOX_HEADS, FOX_DH, D)
    w_fox = jnp.pad(w_fox, ((0, 0), (0, 0), (0, LANE - FOX_DH), (0, 0))).reshape(depth, FOX_HEADS * LANE, D)
    w_out_p = jnp.concatenate([w_out[:, :n_ret], w_fox, w_out[:, n_ret + n_fox:]], axis=1).astype(BF16)
    dwg, dwu, dwd = dense_w_gate.astype(BF16), dense_w_up.astype(BF16), dense_w_down.astype(BF16)
    ewg, ewu, ewd = expert_w_gate.astype(BF16), expert_w_up.astype(BF16), expert_w_down.astype(BF16)
    rwt = jnp.swapaxes(router_w, 1, 2)
    rwt_hi = rwt.astype(BF16)
    rwt_lo = (rwt - rwt_hi.astype(F32)).astype(BF16)
    pad16 = lambda w: jnp.pad(w, ((0, 0), (0, 16 - N_EXPERTS), (0, 0)))
    rw = jnp.concatenate([pad16(rwt_hi), pad16(rwt_lo)], axis=1)

    mkt, mvs = _mem_prep(mem, mem_norm_g[:, None, :], w_mem_kv.astype(BF16), mkg, bd)

    xt = x.reshape(T, D)
    for l in range(depth):
        rq, rk, rv, rg, fqt, fka, fvt, mq = _in_proj(
            xt, attn_norm_g[l][None], w_in_p[l], cos, sin, fqg[l], fkg[l], mqg[l], fb[l], in_consts, S)
        ret_o = _retention(rq, rk, rv, rg, ret_tables, g_ret[l], B, S)
        fox_o = _fox(fqt, fka, fvt, g_fox[l], B, S)
        j = l // 2
        if l % 2 == 0:
            xt = _out_proj(ret_o, fox_o, mq, mkt, mvs, bd, g_mem[l], w_out_p[l], xt, ffn_norm_g[l][None], l, S,
                           dense=(dwg, dwu, dwd, j))
        else:
            xm, h2, routet, route, counts = _out_proj(ret_o, fox_o, mq, mkt, mvs, bd, g_mem[l], w_out_p[l], xt,
                                                      ffn_norm_g[l][None], l, S, router=(rw[j], tri_strict))
            xt = _moe_layer(h2, xm, routet, route, counts, ewg, ewu, ewd, j)
    return xt.reshape(B, S, D)
```
